```python
import functools
import jax, jax.numpy as jnp
from jax import lax
import numpy as np

D_MODEL = 2048
BATCH = 2
SEQ = 4096
DEPTH = 2
DEC_BATCH = 8
DEC_SEQ = 1
PAST_LEN = 16384
PAGE_SIZE = 128

N_EVEN = (DEPTH + 1) // 2
N_ODD = DEPTH // 2
BRANCH = D_MODEL // 2
HA = 4
DVA = BRANCH // HA
DKA = DVA // 2
GLA_RANK = 16
GLA_TAU = 16.0
GLA_CHUNK = 64
HB = 4
DVB = BRANCH // HB
DKB = DVB // 2
RET_CHUNK = 128
ROPE_BASE = 10000.0
HC = 8
DC = BRANCH // HC
SB_BLOCK = 128
SB_OFFSET_NEAR = 4.0
SB_OFFSET_FAR = 10.0
DG = 8
DGC = BRANCH // DG
D_CHUNK = 128
RMS_EPS = 1e-6
EVEN_SIZES = (HA * DKA, HA * DKA, HA * DVA, GLA_RANK, HA * DVA, HB * DKB, HB * DKB, HB * DVB, HB * DVB)
ODD_SIZES = (HC * DC, HC * DC, HC * DC, HC * DC, BRANCH, BRANCH, BRANCH)
W_EVEN = HA * DVA + HB * DVB
W_ODD = HC * DC + BRANCH

kernel_name = 'hybrid_gla_retnet_stickbreak_chunkmlp_step'


def rms_norm(x, g, eps=RMS_EPS):
    xf = x.astype(jnp.float32)
    y = xf * lax.rsqrt(jnp.mean(xf * xf, axis=-1, keepdims=True) + eps)
    return (y * g.astype(jnp.float32)).astype(x.dtype)


def split_cols(x, sizes):
    return jnp.split(x, np.cumsum(sizes)[:-1].tolist(), axis=-1)


def ada_modulation(c, w, b):
    m = jax.nn.silu(c) @ w + b
    shift, scale, gate = jnp.split(m[:, None, :], 3, axis=-1)
    return shift, scale, gate


def rotary(x, pos):
    half = x.shape[-1] // 2
    inv = ROPE_BASE ** (-jnp.arange(half, dtype=jnp.float32) / half)
    ang = pos.astype(jnp.float32)[:, None] * inv[None, :]
    cos = jnp.cos(ang)[None, :, None, :]
    sin = jnp.sin(ang)[None, :, None, :]
    x1, x2 = x[..., :half], x[..., half:]
    return jnp.concatenate([x1 * cos - x2 * sin, x1 * sin + x2 * cos], axis=-1)


def retention_log_gamma():
    return jnp.log1p(-jnp.exp2(-5.0 - jnp.arange(HB, dtype=jnp.float32)))


def gla_chunk(q, k, v, log_a, state):
    C = q.shape[1]
    b = jnp.cumsum(log_a, axis=1)
    o_inter = jnp.einsum('bthk,bhkv->bthv', q * jnp.exp(b), state)
    causal = jnp.tril(jnp.ones((C, C), dtype=bool))[None, :, :, None, None]
    diff = b[:, :, None] - b[:, None, :]
    decay = jnp.exp(jnp.where(causal, diff, -jnp.inf))
    scores = jnp.einsum('btshk,bthk,bshk->bhts', decay, q, k)
    o_intra = jnp.einsum('bhts,bshv->bthv', scores, v)
    b_last = b[:, -1]
    k_dec = k * jnp.exp(b_last[:, None] - b)
    new_state = jnp.exp(b_last)[..., None] * state + jnp.einsum('bshk,bshv->bhkv', k_dec, v)
    return o_inter + o_intra, new_state


def retention_chunk(q, k, v, state, log_gamma):
    C = q.shape[1]
    idx = jnp.arange(C, dtype=jnp.float32)
    inter_decay = jnp.exp((idx + 1.0)[:, None] * log_gamma[None, :])
    o_inter = jnp.einsum('bthk,bhkv->bthv', q, state) * inter_decay[None, :, :, None]
    rel = idx[:, None] - idx[None, :]
    decay = jnp.exp(jnp.where((rel >= 0)[..., None], rel[..., None] * log_gamma, -jnp.inf))
    scores = jnp.einsum('bthk,bshk->bhts', q, k) * jnp.transpose(decay, (2, 0, 1))[None]
    o_intra = jnp.einsum('bhts,bshv->bthv', scores, v)
    k_decay = jnp.exp((C - 1.0 - idx)[:, None] * log_gamma[None, :])
    new_state = (jnp.exp(C * log_gamma)[:, None, None] * state
                 + jnp.einsum('bshk,bshv->bhkv', k * k_decay[None, :, :, None], v))
    return o_inter + o_intra, new_state


def chunked_recurrence(chunk_fn, state0, xs, chunk):
    B, T = xs[0].shape[:2]
    L = min(chunk, T)
    n = T // L
    xs_c = tuple(a.reshape(B, n, L, *a.shape[2:]).swapaxes(0, 1) for a in xs)

    def step(s, xc):
        o, s = chunk_fn(*xc, s)
        return s, o

    s, o = lax.scan(step, state0.astype(jnp.float32), xs_c)
    return o.swapaxes(0, 1).reshape(B, T, *o.shape[3:]), s


def stick_breaking_block(q, k, v, q_pos, k_pos, offset):
    z = (jnp.einsum('bqhd,bshd->bhqs', q, k) * (q.shape[-1] ** -0.5)
         + offset.astype(jnp.float32)[None, :, None, None])
    mask = (k_pos[None, :] < q_pos[:, None])[None, None]
    log_beta = jax.nn.log_sigmoid(z)
    log_rest = jnp.where(mask, jax.nn.log_sigmoid(-z), 0.0)
    between = lax.cumsum(log_rest, axis=3, reverse=True) - log_rest
    weight = jnp.where(mask, jnp.exp(log_beta + between), 0.0)
    return jnp.einsum('bhqs,bshd->bqhd', weight, v)


def stick_breaking_sweep(q, k, v, n_past, offset):
    B, T, H, d = q.shape
    L = min(SB_BLOCK, T)
    n = T // L
    k_pos = jnp.arange(k.shape[1])
    q_blocks = q.reshape(B, n, L, H, d).swapaxes(0, 1)
    q_pos = (n_past + jnp.arange(T)).reshape(n, L)
    out = lax.map(lambda a: stick_breaking_block(a[0], k, v, a[1], k_pos, offset), (q_blocks, q_pos))
    return out.swapaxes(0, 1).reshape(B, T, H, d)


def chunk_spatial_gate(u, v, w_s, b_s):
    B, T, G, c = v.shape
    L = min(D_CHUNK, T)
    n = T // L
    w = jnp.tril(w_s[:, :L, :L])
    vc = v.reshape(B, n, L, G, c)
    mixed = jnp.einsum('gts,bnsgc->bntgc', w, vc) + jnp.transpose(b_s[:, :L])[None, None, :, :, None]
    return u * mixed.reshape(B, T, G, c)


def even_layer(x, c, pos, gla_s0, ret_s0, norm_g, w_ada, b_ada, w_in, w_a2, b_a,
               gla_norm_g, ret_norm_g, w_out):
    B, T, _ = x.shape
    f32 = jnp.float32
    shift, scale, gate = ada_modulation(c, w_ada, b_ada)
    h = rms_norm(x, norm_g) * (1.0 + scale) + shift
    qa, ka, va, ra, ga, qb, kb, vb, gb = split_cols(h @ w_in, EVEN_SIZES)
    heads = lambda t, H: t.astype(f32).reshape(B, T, H, -1)
    log_a = jax.nn.log_sigmoid((ra @ w_a2 + b_a).astype(f32)) / GLA_TAU
    oa, gla_s = chunked_recurrence(
        gla_chunk, gla_s0,
        (heads(qa, HA) * DKA ** -0.5, heads(ka, HA), heads(va, HA), log_a.reshape(B, T, HA, DKA)),
        GLA_CHUNK)
    oa = rms_norm(oa, gla_norm_g.reshape(HA, DVA)).reshape(B, T, -1) * jax.nn.silu(ga.astype(f32))
    qr = rotary(heads(qb, HB), pos) * DKB ** -0.5
    kr = rotary(heads(kb, HB), pos)
    ob, ret_s = chunked_recurrence(
        functools.partial(retention_chunk, log_gamma=retention_log_gamma()), ret_s0,
        (qr, kr, heads(vb, HB)), RET_CHUNK)
    ob = rms_norm(ob, ret_norm_g.reshape(HB, DVB)).reshape(B, T, -1) * jax.nn.silu(gb.astype(f32))
    mixed = jnp.concatenate([oa, ob], axis=-1).astype(x.dtype) @ w_out
    return x + gate * mixed, gla_s, ret_s


def odd_layer(x, c, k_past, v_past, norm_g, w_ada, b_ada, w_in, q_norm_g, k_norm_g,
              sb_offset, v_norm_g, w_s, b_s, w_out):
    B, T, _ = x.shape
    f32 = jnp.float32
    shift, scale, gate = ada_modulation(c, w_ada, b_ada)
    h = rms_norm(x, norm_g) * (1.0 + scale) + shift
    qc, kc, vc, gc, ud, vd, gd = split_cols(h @ w_in, ODD_SIZES)
    heads = lambda t: t.astype(f32).reshape(B, T, HC, DC)
    qc = rms_norm(heads(qc), q_norm_g)
    kc = rms_norm(heads(kc), k_norm_g)
    vc = heads(vc)
    if k_past is None:
        k_all, v_all, n_past = kc, vc, 0
    else:
        k_all = jnp.concatenate([k_past.astype(f32), kc], axis=1)
        v_all = jnp.concatenate([v_past.astype(f32), vc], axis=1)
        n_past = k_past.shape[1]
    oc = stick_breaking_sweep(qc, k_all, v_all, n_past, sb_offset).reshape(B, T, -1)
    oc = oc * jax.nn.silu(gc.astype(f32))
    u = jax.nn.gelu(ud.astype(f32))
    vn = rms_norm(jax.nn.gelu(vd.astype(f32)), v_norm_g)
    od = chunk_spatial_gate(u.reshape(B, T, DG, DGC), vn.reshape(B, T, DG, DGC),
                            w_s.astype(f32), b_s.astype(f32)).reshape(B, T, -1)
    od = od * jax.nn.silu(gd.astype(f32))
    mixed = jnp.concatenate([oc, od], axis=-1).astype(x.dtype) @ w_out
    return x + gate * mixed, kc, vc, vn


def setup_inputs(seed: int = 0) -> dict:
    key = jax.random.key(seed)
    ks = iter(jax.random.split(key, 64))
    f32 = jnp.float32

    def normal(shape, scale=1.0):
        return jax.random.normal(next(ks), shape, f32) * scale

    def gain(shape):
        return 1.0 + normal(shape, 0.02)

    n_pages = PAST_LEN // PAGE_SIZE
    n_used = DEC_BATCH * n_pages
    n_pool = n_used + max(1, n_used // 4)
    perm = jax.random.permutation(next(ks), n_pool)
    page_table = perm[:n_used].reshape(DEC_BATCH, n_pages).astype(jnp.int32)
    p_even = sum(EVEN_SIZES)
    p_odd = sum(ODD_SIZES)
    sb_offset = (-jnp.linspace(SB_OFFSET_NEAR, SB_OFFSET_FAR, HC, dtype=f32)[None, :]
                 + normal((N_ODD, HC), 0.1))
    return {
        'x_prompt': normal((BATCH, SEQ, D_MODEL)),
        'x_sample': normal((DEC_BATCH, DEC_SEQ, D_MODEL)),
        'state_gla': normal((N_EVEN, DEC_BATCH, HA, DKA, DVA)),
        'state_ret': normal((N_EVEN, DEC_BATCH, HB, DKB, DVB)),
        'cache_k': normal((N_ODD, n_pool, PAGE_SIZE, HC, DC)),
        'cache_v': normal((N_ODD, n_pool, PAGE_SIZE, HC, DC)),
        'page_table': page_table,
        'c_prompt': normal((BATCH, D_MODEL)),
        'c_sample': normal((DEC_BATCH, D_MODEL)),
        'e_norm_g': gain((N_EVEN, D_MODEL)),
        'e_w_ada': normal((N_EVEN, D_MODEL, 3 * D_MODEL), 0.5 * D_MODEL ** -0.5),
        'e_b_ada': normal((N_EVEN, 3 * D_MODEL), 0.02),
        'e_w_in': normal((N_EVEN, D_MODEL, p_even), D_MODEL ** -0.5),
        'e_w_a2': normal((N_EVEN, GLA_RANK, HA * DKA), GLA_RANK ** -0.5),
        'e_b_a': normal((N_EVEN, HA * DKA), 0.1),
        'e_gla_norm_g': gain((N_EVEN, HA * DVA)),
        'e_ret_norm_g': gain((N_EVEN, HB * DVB)),
        'e_w_out': normal((N_EVEN, W_EVEN, D_MODEL), W_EVEN ** -0.5),
        'o_norm_g': gain((N_ODD, D_MODEL)),
        'o_w_ada': normal((N_ODD, D_MODEL, 3 * D_MODEL), 0.5 * D_MODEL ** -0.5),
        'o_b_ada': normal((N_ODD, 3 * D_MODEL), 0.02),
        'o_w_in': normal((N_ODD, D_MODEL, p_odd), D_MODEL ** -0.5),
        'o_q_norm_g': gain((N_ODD, DC)),
        'o_k_norm_g': gain((N_ODD, DC)),
        'o_sb_offset': sb_offset,
        'o_v_norm_g': gain((N_ODD, BRANCH)),
        'o_w_s': normal((N_ODD, DG, D_CHUNK, D_CHUNK), D_CHUNK ** -0.5),
        'o_b_s': 1.0 + normal((N_ODD, DG, D_CHUNK), 0.1),
        'o_w_out': normal((N_ODD, W_ODD, D_MODEL), W_ODD ** -0.5),
    }


def reference(x_prompt, x_sample, state_gla, state_ret, cache_k, cache_v, page_table,
              c_prompt, c_sample, e_norm_g, e_w_ada, e_b_ada, e_w_in, e_w_a2, e_b_a,
              e_gla_norm_g, e_ret_norm_g, e_w_out, o_norm_g, o_w_ada, o_b_ada, o_w_in,
              o_q_norm_g, o_k_norm_g, o_sb_offset, o_v_norm_g, o_w_s, o_b_s, o_w_out):
    f32 = jnp.float32
    dec_b = page_table.shape[0]
    n_past = page_table.shape[1] * cache_k.shape[2]
    pos_prompt = jnp.arange(x_prompt.shape[1])
    pos_sample = n_past + jnp.arange(x_sample.shape[1])
    xp, xs = x_prompt, x_sample
    gla_p, gla_s, ret_p, ret_s = [], [], [], []
    k_p, v_p, k_s, v_s, chunk_v = [], [], [], [], []
    for layer in range(DEPTH):
        i = layer // 2
        if layer % 2 == 0:
            w = (e_norm_g[i], e_w_ada[i], e_b_ada[i], e_w_in[i], e_w_a2[i], e_b_a[i],
                 e_gla_norm_g[i], e_ret_norm_g[i], e_w_out[i])
            zero_gla = jnp.zeros((xp.shape[0], HA, DKA, DVA), f32)
            zero_ret = jnp.zeros((xp.shape[0], HB, DKB, DVB), f32)
            xp, sg, sr = even_layer(xp, c_prompt, pos_prompt, zero_gla, zero_ret, *w)
            gla_p.append(sg)
            ret_p.append(sr)
            xs, sg, sr = even_layer(xs, c_sample, pos_sample, state_gla[i], state_ret[i], *w)
            gla_s.append(sg)
            ret_s.append(sr)
        else:
            w = (o_norm_g[i], o_w_ada[i], o_b_ada[i], o_w_in[i], o_q_norm_g[i], o_k_norm_g[i],
                 o_sb_offset[i], o_v_norm_g[i], o_w_s[i], o_b_s[i], o_w_out[i])
            xp, kk, vv, _ = odd_layer(xp, c_prompt, None, None, *w)
            k_p.append(kk)
            v_p.append(vv)
            k_past = cache_k[i][page_table].reshape(dec_b, n_past, HC, DC)
            v_past = cache_v[i][page_table].reshape(dec_b, n_past, HC, DC)
            xs, kk, vv, cv = odd_layer(xs, c_sample, k_past, v_past, *w)
            k_s.append(kk)
            v_s.append(vv)
            chunk_v.append(cv)
    return (xp, xs, jnp.stack(gla_p), jnp.stack(gla_s), jnp.stack(ret_p), jnp.stack(ret_s),
            jnp.stack(k_p), jnp.stack(v_p), jnp.stack(k_s), jnp.stack(v_s), jnp.stack(chunk_v))
```

```python
import functools

import jax
import jax.numpy as jnp
import numpy as np
from jax import lax
from jax.experimental import pallas as pl
from jax.experimental.pallas import tpu as pltpu

F32 = jnp.float32
BF16 = jnp.bfloat16

HA = 4
HB = 4
HC = 8
DG = 8
GLA_RANK = 16
GLA_TAU = 16.0
GLA_CHUNK = 64
RET_CHUNK = 128
ROPE_BASE = 10000.0
SB_BLOCK = 128
D_CHUNK = 128
RMS_EPS = 1e-6

LANES = 128
MXU_DIM = 256
VMEM_LIMIT = 48 * 1024 * 1024


def _cparams(sem):
    return pltpu.CompilerParams(dimension_semantics=sem, vmem_limit_bytes=VMEM_LIMIT)


def _dot(a, b):
    return jnp.dot(a, b, preferred_element_type=F32)


def _dot_nt(a, b):
    return lax.dot_general(a, b, (((1,), (1,)), ((), ())), preferred_element_type=F32)


def _dot_tn(a, b):
    return lax.dot_general(a, b, (((0,), (0,)), ((), ())), preferred_element_type=F32)


def _sigmoid(x):
    return 1.0 / (1.0 + jnp.exp(-x))


def _silu(x):
    return x * _sigmoid(x)


def _gelu(x):
    c = np.float32(np.sqrt(2.0 / np.pi))
    return 0.5 * x * (1.0 + jnp.tanh(c * (x + 0.044715 * (x * x * x))))


def _softplus_neg_abs(x):
    return jnp.log(1.0 + jnp.exp(-jnp.abs(x)))


def _split_dot(x, ones_bf16, left=False):
    hi = x.astype(BF16)
    lo = (x - hi.astype(F32)).astype(BF16)
    if left:
        return _dot(ones_bf16, hi) + _dot(ones_bf16, lo)
    return _dot(hi, ones_bf16) + _dot(lo, ones_bf16)


def _rms(x, g):
    ms = jnp.mean(x * x, axis=-1, keepdims=True)
    return x * lax.rsqrt(ms + RMS_EPS) * g


def _ada_kernel(c_ref, w_ref, b_ref, o_ref):
    s = _silu(c_ref[...]).astype(BF16)
    o_ref[...] = _dot(s, w_ref[...].astype(BF16)) + b_ref[...]


def _ada(c_rows, w, b):
    r, d = c_rows.shape
    n = w.shape[1]
    tn = 512
    return pl.pallas_call(
        _ada_kernel,
        out_shape=jax.ShapeDtypeStruct((r, n), F32),
        grid=(n // tn,),
        in_specs=[pl.BlockSpec((r, d), lambda j: (0, 0)),
                  pl.BlockSpec((d, tn), lambda j: (0, j)),
                  pl.BlockSpec((1, tn), lambda j: (0, j))],
        out_specs=pl.BlockSpec((r, tn), lambda j: (0, j)),
        compiler_params=_cparams(("arbitrary",)),
        name="ada",
    )(c_rows, w, b.reshape(1, n))


def _proj_kernel(x_ref, shift_ref, scale_ref, g_ref, w_ref, o_ref, h_ref):
    @pl.when(pl.program_id(1) == 0)
    def _():
        h = _rms(x_ref[...], g_ref[...]) * (1.0 + scale_ref[...]) + shift_ref[...]
        h_ref[...] = h.astype(BF16)

    o_ref[...] = _dot(h_ref[...], w_ref[...])


def _proj(x2d, mod, norm_g, w_bf16, rows_per_batch, tm, tn):
    m, d = x2d.shape
    n = w_bf16.shape[1]
    if rows_per_batch is None:
        shift_spec = pl.BlockSpec((tm, d), lambda i, j: (i, 0))
        scale_spec = pl.BlockSpec((tm, d), lambda i, j: (i, 1))
    else:
        tpb = rows_per_batch // tm
        shift_spec = pl.BlockSpec((None, 1, d), lambda i, j: (i // tpb, 0, 0))
        scale_spec = pl.BlockSpec((None, 1, d), lambda i, j: (i // tpb, 0, 1))
    return pl.pallas_call(
        _proj_kernel,
        out_shape=jax.ShapeDtypeStruct((m, n), F32),
        grid=(m // tm, n // tn),
        in_specs=[pl.BlockSpec((tm, d), lambda i, j: (i, 0)),
                  shift_spec, scale_spec,
                  pl.BlockSpec((1, d), lambda i, j: (0, 0)),
                  pl.BlockSpec((d, tn), lambda i, j: (0, j))],
        out_specs=pl.BlockSpec((tm, tn), lambda i, j: (i, j)),
        scratch_shapes=[pltpu.VMEM((tm, d), BF16)],
        compiler_params=_cparams(("arbitrary", "arbitrary")),
        name="proj",
    )(x2d, mod, mod, norm_g.reshape(1, d), w_bf16)


def _outproj_kernel(ma_ref, mb_ref, wa_ref, wb_ref, x_ref, gate_ref, o_ref):
    mixed = _dot(ma_ref[...], wa_ref[...]) + _dot(mb_ref[...], wb_ref[...])
    o_ref[...] = x_ref[...] + gate_ref[...] * mixed


def _outproj(ma, ca, mb, cb, w_bf16, x2d, mod, rows_per_batch, tm, tn):
    m, d = x2d.shape
    half = w_bf16.shape[0] // 2
    gcol = 2 * (d // tn)
    if rows_per_batch is None:
        gate_spec = pl.BlockSpec((tm, tn), lambda i, j: (i, gcol + j))
    else:
        tpb = rows_per_batch // tm
        gate_spec = pl.BlockSpec((None, 1, tn), lambda i, j: (i // tpb, 0, gcol + j))
    return pl.pallas_call(
        _outproj_kernel,
        out_shape=jax.ShapeDtypeStruct((m, d), F32),
        grid=(m // tm, d // tn),
        in_specs=[pl.BlockSpec((tm, half), lambda i, j: (i, ca)),
                  pl.BlockSpec((tm, half), lambda i, j: (i, cb)),
                  pl.BlockSpec((half, tn), lambda i, j: (0, j)),
                  pl.BlockSpec((half, tn), lambda i, j: (1, j)),
                  pl.BlockSpec((tm, tn), lambda i, j: (i, j)),
                  gate_spec],
        out_specs=pl.BlockSpec((tm, tn), lambda i, j: (i, j)),
        compiler_params=_cparams(("arbitrary", "arbitrary")),
        name="outproj",
    )(ma, mb, w_bf16, w_bf16, x2d, mod)


def _gla_kernel(q_ref, k_ref, v_ref, ra_ref, g_ref, wa2_ref, ba_ref, ng_ref,
                o_ref, st_ref, stt_ref, *, n_sub, dk):
    c = pl.program_id(2)
    cs = GLA_CHUNK

    @pl.when(c == 0)
    def _():
        stt_ref[...] = jnp.zeros_like(stt_ref)

    row = lax.broadcasted_iota(jnp.int32, (cs, cs), 0)
    col = lax.broadcasted_iota(jnp.int32, (cs, cs), 1)
    causal = row >= col
    tri = jnp.where(causal, 1.0, 0.0).astype(BF16)
    wa2 = wa2_ref[...]
    ba = ba_ref[...]
    ng = ng_ref[...]
    qscale = np.float32(dk ** -0.5)

    def body(i, carry):
        r0 = pl.multiple_of(i * cs, cs)
        q = q_ref[pl.ds(r0, cs), :] * qscale
        k = k_ref[pl.ds(r0, cs), :]
        v = v_ref[pl.ds(r0, cs), :].astype(BF16)
        pre = _dot(ra_ref[pl.ds(r0, cs), :].astype(BF16), wa2) + ba
        la = (jnp.minimum(pre, 0.0) - _softplus_neg_abs(pre)) * np.float32(1.0 / GLA_TAU)
        b = _split_dot(la, tri, left=True)
        qe = (q * jnp.exp(b)).astype(BF16)
        ke = (k * jnp.exp(-b)).astype(BF16)
        s = jnp.where(causal, _dot_nt(qe, ke), 0.0)
        stt = stt_ref[...]
        o = _dot(s.astype(BF16), v) + _dot_nt(qe, stt.astype(BF16))
        bl = b[cs - 1:cs, :]
        kd = (k * jnp.exp(bl - b)).astype(BF16)
        stt_ref[...] = stt * jnp.exp(bl) + _dot_tn(v, kd)
        og = _rms(o, ng) * _silu(g_ref[pl.ds(r0, cs), :])
        o_ref[pl.ds(r0, cs), :] = og.astype(o_ref.dtype)
        return carry

    lax.fori_loop(0, n_sub, body, 0)

    @pl.when(c == pl.num_programs(2) - 1)
    def _():
        st_ref[...] = stt_ref[...].T


def _gla_prompt(y3, wa2_pad, b_a, norm_g, cols, tc):
    bsz, t, _ = y3.shape
    dk, dv = LANES, 2 * LANES
    cq, ck, cv, cra, cg = cols
    kern = functools.partial(_gla_kernel, n_sub=tc // GLA_CHUNK, dk=dk)
    return pl.pallas_call(
        kern,
        out_shape=(jax.ShapeDtypeStruct((bsz, t, HA * dv), BF16),
                   jax.ShapeDtypeStruct((bsz, HA, dk, dv), F32)),
        grid=(bsz, HA, t // tc),
        in_specs=[pl.BlockSpec((None, tc, dk), lambda b, h, c: (b, c, cq + h)),
                  pl.BlockSpec((None, tc, dk), lambda b, h, c: (b, c, ck + h)),
                  pl.BlockSpec((None, tc, dv), lambda b, h, c: (b, c, cv // 2 + h)),
                  pl.BlockSpec((None, tc, LANES), lambda b, h, c: (b, c, cra)),
                  pl.BlockSpec((None, tc, dv), lambda b, h, c: (b, c, cg // 2 + h)),
                  pl.BlockSpec((LANES, dk), lambda b, h, c: (0, h)),
                  pl.BlockSpec((1, dk), lambda b, h, c: (0, h)),
                  pl.BlockSpec((1, dv), lambda b, h, c: (0, h))],
        out_specs=(pl.BlockSpec((None, tc, dv), lambda b, h, c: (b, c, h)),
                   pl.BlockSpec((None, None, dk, dv), lambda b, h, c: (b, h, 0, 0))),
        scratch_shapes=[pltpu.VMEM((dv, dk), F32)],
        compiler_params=_cparams(("arbitrary", "arbitrary", "arbitrary")),
        name="gla_prompt",
    )(y3, y3, y3, y3, y3, wa2_pad, b_a.reshape(1, -1), norm_g.reshape(1, -1))


def _rot(x, cosf, sinf):
    return x * cosf + pltpu.roll(x, x.shape[-1] // 2, 1) * sinf


def _ret_kernel(q_ref, k_ref, v_ref, g_ref, cos_ref, sin_ref, lg_ref, ng_ref,
                o_ref, st_ref, stt_ref, *, n_sub, dk):
    c = pl.program_id(2)
    cs = RET_CHUNK

    @pl.when(c == 0)
    def _():
        stt_ref[...] = jnp.zeros_like(stt_ref)

    lg = lg_ref[...][:, 0:1]
    row = lax.broadcasted_iota(jnp.int32, (cs, cs), 0)
    col = lax.broadcasted_iota(jnp.int32, (cs, cs), 1)
    rel = (row - col).astype(F32)
    decay = jnp.where(row >= col, jnp.exp(rel * lg), 0.0)
    ridx = lax.broadcasted_iota(jnp.int32, (cs, 1), 0).astype(F32)
    inter = jnp.exp((ridx + 1.0) * lg)
    kdec = jnp.exp((np.float32(cs - 1.0) - ridx) * lg)
    sdec = jnp.exp(np.float32(cs) * lg)
    ng = ng_ref[...]
    qscale = np.float32(dk ** -0.5)

    def body(i, carry):
        r0 = pl.multiple_of(i * cs, cs)
        cosf = cos_ref[pl.ds(r0, cs), :]
        sinf = sin_ref[pl.ds(r0, cs), :]
        qr = (_rot(q_ref[pl.ds(r0, cs), :], cosf, sinf) * qscale).astype(BF16)
        kr = _rot(k_ref[pl.ds(r0, cs), :], cosf, sinf)
        v = v_ref[pl.ds(r0, cs), :].astype(BF16)
        s = _dot_nt(qr, kr.astype(BF16)) * decay
        stt = stt_ref[...]
        o = _dot(s.astype(BF16), v) + _dot_nt(qr, stt.astype(BF16)) * inter
        stt_ref[...] = stt * sdec + _dot_tn(v, (kr * kdec).astype(BF16))
        og = _rms(o, ng) * _silu(g_ref[pl.ds(r0, cs), :])
        o_ref[pl.ds(r0, cs), :] = og.astype(o_ref.dtype)
        return carry

    lax.fori_loop(0, n_sub, body, 0)

    @pl.when(c == pl.num_programs(2) - 1)
    def _():
        st_ref[...] = stt_ref[...].T


def _ret_prompt(y3, cosf, sinf, lg_rows, norm_g, cols, tc):
    bsz, t, _ = y3.shape
    dk, dv = LANES, 2 * LANES
    cq, ck, cv, cg = cols
    kern = functools.partial(_ret_kernel, n_sub=tc // RET_CHUNK, dk=dk)
    return pl.pallas_call(
        kern,
        out_shape=(jax.ShapeDtypeStruct((bsz, t, HB * dv), BF16),
                   jax.ShapeDtypeStruct((bsz, HB, dk, dv), F32)),
        grid=(bsz, HB, t // tc),
        in_specs=[pl.BlockSpec((None, tc, dk), lambda b, h, c: (b, c, cq + h)),
                  pl.BlockSpec((None, tc, dk), lambda b, h, c: (b, c, ck + h)),
                  pl.BlockSpec((None, tc, dv), lambda b, h, c: (b, c, cv // 2 + h)),
                  pl.BlockSpec((None, tc, dv), lambda b, h, c: (b, c, cg // 2 + h)),
                  pl.BlockSpec((tc, dk), lambda b, h, c: (c, 0)),
                  pl.BlockSpec((tc, dk), lambda b, h, c: (c, 0)),
                  pl.BlockSpec((None, 1, LANES), lambda b, h, c: (h, 0, 0)),
                  pl.BlockSpec((1, dv), lambda b, h, c: (0, h))],
        out_specs=(pl.BlockSpec((None, tc, dv), lambda b, h, c: (b, c, h)),
                   pl.BlockSpec((None, None, dk, dv), lambda b, h, c: (b, h, 0, 0))),
        scratch_shapes=[pltpu.VMEM((dv, dk), F32)],
        compiler_params=_cparams(("arbitrary", "arbitrary", "arbitrary")),
        name="ret_prompt",
    )(y3, y3, y3, y3, cosf, sinf, lg_rows, norm_g.reshape(1, -1))


def _row_to_col(x_row):
    n = x_row.shape[-1]
    r = lax.broadcasted_iota(jnp.int32, (n, n), 0)
    c = lax.broadcasted_iota(jnp.int32, (n, n), 1)
    return jnp.sum(jnp.where(r == c, jnp.broadcast_to(x_row, (n, n)), 0.0), axis=1, keepdims=True)


def _even_decode_kernel(y_ref, sg_ref, sr_ref, wa2_ref, ba_ref, gng_ref, rng_ref, cos_ref, sin_ref,
                        lg_ref, o_ref, sgo_ref, sro_ref, *, cols):
    dk, dv = LANES, 2 * LANES
    cqa, cka, cva, cra, cga, cqb, ckb, cvb, cgb = [c * LANES for c in cols]
    y = y_ref[...]
    pre = _dot(y[:, cra:cra + LANES].astype(BF16), wa2_ref[...]) + ba_ref[...]
    la = (jnp.minimum(pre, 0.0) - _softplus_neg_abs(pre)) * np.float32(1.0 / GLA_TAU)
    alpha = jnp.exp(la)
    cosf = cos_ref[...]
    sinf = sin_ref[...]
    gam = jnp.exp(lg_ref[...])
    outs = []
    for h in range(HA):
        q = y[:, cqa + h * dk:cqa + (h + 1) * dk] * np.float32(dk ** -0.5)
        k = y[:, cka + h * dk:cka + (h + 1) * dk]
        v = y[:, cva + h * dv:cva + (h + 1) * dv]
        s_new = _row_to_col(alpha[:, h * dk:(h + 1) * dk]) * sg_ref[h] + _row_to_col(k) * v
        sgo_ref[h] = s_new
        o = jnp.sum(_row_to_col(q) * s_new, axis=0, keepdims=True)
        g = y[:, cga + h * dv:cga + (h + 1) * dv]
        outs.append(_rms(o, gng_ref[:, h * dv:(h + 1) * dv]) * _silu(g))
    for h in range(HB):
        q = _rot(y[:, cqb + h * dk:cqb + (h + 1) * dk], cosf, sinf) * np.float32(dk ** -0.5)
        k = _rot(y[:, ckb + h * dk:ckb + (h + 1) * dk], cosf, sinf)
        v = y[:, cvb + h * dv:cvb + (h + 1) * dv]
        s_new = _row_to_col(gam[:, h * dk:(h + 1) * dk]) * sr_ref[h] + _row_to_col(k) * v
        sro_ref[h] = s_new
        o = jnp.sum(_row_to_col(q) * s_new, axis=0, keepdims=True)
        g = y[:, cgb + h * dv:cgb + (h + 1) * dv]
        outs.append(_rms(o, rng_ref[:, h * dv:(h + 1) * dv]) * _silu(g))
    o_ref[...] = jnp.concatenate(outs, axis=-1).astype(o_ref.dtype)


def _even_decode(y_s, state_gla, state_ret, wa2_pad, b_a, gla_ng, ret_ng, cos_row, sin_row, lg_row, cols):
    bsz, n = y_s.shape
    dk, dv = LANES, 2 * LANES
    wide = (HA + HB) * dv
    kern = functools.partial(_even_decode_kernel, cols=cols)
    full = lambda a: pl.BlockSpec(a.shape, lambda b: (0,) * a.ndim)
    b_a2, gng, rng_ = b_a.reshape(1, -1), gla_ng.reshape(1, -1), ret_ng.reshape(1, -1)
    mixed, sg, sr = pl.pallas_call(
        kern,
        out_shape=(jax.ShapeDtypeStruct((bsz, 1, wide), BF16),
                   jax.ShapeDtypeStruct(state_gla.shape, F32),
                   jax.ShapeDtypeStruct(state_ret.shape, F32)),
        grid=(bsz,),
        in_specs=[pl.BlockSpec((None, 1, n), lambda b: (b, 0, 0)),
                  pl.BlockSpec((None, HA, dk, dv), lambda b: (b, 0, 0, 0)),
                  pl.BlockSpec((None, HB, dk, dv), lambda b: (b, 0, 0, 0)),
                  full(wa2_pad), full(b_a2), full(gng), full(rng_), full(cos_row), full(sin_row), full(lg_row)],
        out_specs=(pl.BlockSpec((None, 1, wide), lambda b: (b, 0, 0)),
                   pl.BlockSpec((None, HA, dk, dv), lambda b: (b, 0, 0, 0)),
                   pl.BlockSpec((None, HB, dk, dv), lambda b: (b, 0, 0, 0))),
        compiler_params=_cparams(("arbitrary",)),
        name="even_decode",
    )(y_s.reshape(bsz, 1, n), state_gla, state_ret, wa2_pad, b_a2, gng, rng_, cos_row, sin_row, lg_row)
    return mixed.reshape(bsz, wide), sg, sr


def _qkv_kernel(q_ref, k_ref, v_ref, qg_ref, kg_ref, qn_ref, kn_ref, knb_ref, vo_ref, vb_ref):
    qn_ref[...] = _rms(q_ref[...], qg_ref[...]).astype(BF16)
    kn = _rms(k_ref[...], kg_ref[...])
    kn_ref[...] = kn
    knb_ref[...] = kn.astype(BF16)
    v = v_ref[...]
    vo_ref[...] = v
    vb_ref[...] = v.astype(BF16)


def _qkv_prompt(y2, q_g, k_g, tm):
    m = y2.shape[0]
    dc = LANES
    blk = lambda off: pl.BlockSpec((tm, dc), lambda i, h: (i, off + h))
    gspec = pl.BlockSpec((1, dc), lambda i, h: (0, 0))
    return pl.pallas_call(
        _qkv_kernel,
        out_shape=(jax.ShapeDtypeStruct((m, HC * dc), BF16),
                   jax.ShapeDtypeStruct((m, HC * dc), F32),
                   jax.ShapeDtypeStruct((m, HC * dc), BF16),
                   jax.ShapeDtypeStruct((m, HC * dc), F32),
                   jax.ShapeDtypeStruct((m, HC * dc), BF16)),
        grid=(m // tm, HC),
        in_specs=[blk(0), blk(HC), blk(2 * HC), gspec, gspec],
        out_specs=(blk(0), blk(0), blk(0), blk(0), blk(0)),
        compiler_params=_cparams(("arbitrary", "arbitrary")),
        name="qkv_prompt",
    )(y2, y2, y2, q_g.reshape(1, dc), k_g.reshape(1, dc))


def _sb_kernel(off_ref, q_ref, k_ref, v_ref, g_ref, o_ref, acc_ref, carry_ref, *, tq, tk, sub):
    h = pl.program_id(1)
    qi = pl.program_id(2)
    kr = pl.program_id(3)
    kb = qi - kr

    @pl.when(kr == 0)
    def _():
        acc_ref[...] = jnp.zeros_like(acc_ref)
        carry_ref[...] = jnp.zeros_like(carry_ref)

    @pl.when(kb >= 0)
    def _():
        q = q_ref[...]
        off = off_ref[h]
        scale = np.float32(q.shape[-1] ** -0.5)
        r = lax.broadcasted_iota(jnp.int32, (sub, sub), 0)
        c = lax.broadcasted_iota(jnp.int32, (sub, sub), 1)
        upper = jnp.where(r >= c, 1.0, 0.0).astype(BF16)
        qpos = qi * tq + lax.broadcasted_iota(jnp.int32, (tq, sub), 0)
        lanepos = lax.broadcasted_iota(jnp.int32, (tq, sub), 1)
        acc = acc_ref[...]
        carry = carry_ref[...]
        for s_i in reversed(range(tk // sub)):
            ks = k_ref[s_i * sub:(s_i + 1) * sub, :]
            vs = v_ref[s_i * sub:(s_i + 1) * sub, :]
            z = _dot_nt(q, ks) * scale + off
            sp = _softplus_neg_abs(z)
            mask = (kb * tk + s_i * sub + lanepos) < qpos
            log_beta = jnp.minimum(z, 0.0) - sp
            log_rest = jnp.where(mask, -jnp.maximum(z, 0.0) - sp, 0.0)
            incl = _split_dot(log_rest, upper)
            between = incl - log_rest + carry
            w = jnp.where(mask, jnp.exp(log_beta + between), 0.0)
            acc = acc + _dot(w.astype(BF16), vs)
            carry = carry + incl[:, 0:1]
        acc_ref[...] = acc
        carry_ref[...] = carry

    @pl.when(kr == pl.num_programs(3) - 1)
    def _():
        o_ref[...] = (acc_ref[...] * _silu(g_ref[...])).astype(o_ref.dtype)


def _sb_prompt(qn, kn, vb, y3, sb_offset, cg, tq, tk, sub):
    bsz, t, _ = qn.shape
    dc = LANES
    nq, nk = t // tq, t // tk
    assert tq == tk
    kern = functools.partial(_sb_kernel, tq=tq, tk=tk, sub=sub)
    kv_map = lambda b, h, qi, kr, off: (b, jnp.maximum(qi - kr, 0), h)
    return pl.pallas_call(
        kern,
        out_shape=jax.ShapeDtypeStruct((bsz, t, HC * dc), BF16),
        grid_spec=pltpu.PrefetchScalarGridSpec(
            num_scalar_prefetch=1,
            grid=(bsz, HC, nq, nk),
            in_specs=[pl.BlockSpec((None, tq, dc), lambda b, h, qi, kr, off: (b, qi, h)),
                      pl.BlockSpec((None, tk, dc), kv_map),
                      pl.BlockSpec((None, tk, dc), kv_map),
                      pl.BlockSpec((None, tq, dc), lambda b, h, qi, kr, off: (b, qi, cg + h))],
            out_specs=pl.BlockSpec((None, tq, dc), lambda b, h, qi, kr, off: (b, qi, h)),
            scratch_shapes=[pltpu.VMEM((tq, dc), F32), pltpu.VMEM((tq, 1), F32)]),
        compiler_params=_cparams(("arbitrary",) * 4),
        name="sb_prompt",
    )(sb_offset, qn, kn, vb, y3)


def _cmlp_kernel(u_ref, v_ref, g_ref, vg_ref, ws_ref, bs_ref, o_ref, *, n_chunk):
    cs = D_CHUNK
    u = _gelu(u_ref[...])
    vn = _rms(_gelu(v_ref[...]), vg_ref[...]).astype(BF16)
    gate = _silu(g_ref[...])
    bs = bs_ref[...]
    r = lax.broadcasted_iota(jnp.int32, (cs, cs), 0)
    c = lax.broadcasted_iota(jnp.int32, (cs, cs), 1)
    for g in range(DG):
        w = jnp.where(r >= c, ws_ref[g], 0.0).astype(BF16)
        lo, hi = g * LANES, (g + 1) * LANES
        for ch in range(n_chunk):
            r0, r1 = ch * cs, (ch + 1) * cs
            mixed = _dot(w, vn[r0:r1, lo:hi]) + bs[:, lo:hi]
            o_ref[r0:r1, lo:hi] = (u[r0:r1, lo:hi] * mixed * gate[r0:r1, lo:hi]).astype(o_ref.dtype)


def _cmlp_prompt(y3, v_norm_g, w_s, bs_full, cols, tm):
    bsz, t, _ = y3.shape
    br = DG * LANES
    cu, cv, cg = cols
    kern = functools.partial(_cmlp_kernel, n_chunk=tm // D_CHUNK)
    return pl.pallas_call(
        kern,
        out_shape=jax.ShapeDtypeStruct((bsz, t, br), BF16),
        grid=(bsz, t // tm),
        in_specs=[pl.BlockSpec((None, tm, br), lambda b, i: (b, i, cu)),
                  pl.BlockSpec((None, tm, br), lambda b, i: (b, i, cv)),
                  pl.BlockSpec((None, tm, br), lambda b, i: (b, i, cg)),
                  pl.BlockSpec((1, br), lambda b, i: (0, 0)),
                  pl.BlockSpec((DG, D_CHUNK, D_CHUNK), lambda b, i: (0, 0, 0)),
                  pl.BlockSpec((D_CHUNK, br), lambda b, i: (0, 0))],
        out_specs=pl.BlockSpec((None, tm, br), lambda b, i: (b, i, 0)),
        compiler_params=_cparams(("arbitrary", "arbitrary")),
        name="cmlp_prompt",
    )(y3, y3, y3, v_norm_g.reshape(1, br), w_s, bs_full)


def _odd_decode_kernel(y_ref, qg_ref, kg_ref, vg_ref, w0_ref, b0_ref, qn_ref, kn_ref, vo_ref, cv_ref, od_ref):
    br = HC * LANES
    y = y_ref[...]
    for h in range(HC):
        lo, hi = h * LANES, (h + 1) * LANES
        qn_ref[:, lo:hi] = _rms(y[:, lo:hi], qg_ref[...])
        kn_ref[:, lo:hi] = _rms(y[:, br + lo:br + hi], kg_ref[...])
    vo_ref[...] = y[:, 2 * br:3 * br]
    u = _gelu(y[:, 4 * br:5 * br])
    vn = _rms(_gelu(y[:, 5 * br:6 * br]), vg_ref[...])
    cv_ref[...] = vn
    od = u * (w0_ref[...] * vn + b0_ref[...]) * _silu(y[:, 6 * br:7 * br])
    od_ref[...] = od.astype(od_ref.dtype)


def _odd_decode(y_s, q_g, k_g, v_norm_g, w0_row, b0_row):
    s = y_s.shape[0]
    br = HC * LANES
    args = (y_s, q_g.reshape(1, -1), k_g.reshape(1, -1), v_norm_g.reshape(1, -1), w0_row, b0_row)
    full = lambda a: pl.BlockSpec(a.shape, lambda i: (0,) * a.ndim)
    osd = lambda dt: jax.ShapeDtypeStruct((s, br), dt)
    ospec = pl.BlockSpec((s, br), lambda i: (0, 0))
    return pl.pallas_call(
        _odd_decode_kernel,
        out_shape=(osd(F32), osd(F32), osd(F32), osd(F32), osd(BF16)),
        grid=(1,),
        in_specs=[full(a) for a in args],
        out_specs=(ospec,) * 5,
        compiler_params=_cparams(("arbitrary",)),
        name="odd_decode",
    )(*args)


def _paged_kernel(pt_ref, q_ref, k_ref, v_ref, g_ref, off_ref, o_ref, qbd_ref, acc_ref, carry_ref):
    j = pl.program_id(1)
    dc = LANES
    width = HC * dc
    head_of_lane = lax.broadcasted_iota(jnp.int32, (HC, width), 1) // dc
    own = head_of_lane == lax.broadcasted_iota(jnp.int32, (HC, width), 0)

    @pl.when(j == 0)
    def _():
        qbd_ref[...] = jnp.where(own, jnp.broadcast_to(q_ref[...], (HC, width)), 0.0).astype(BF16)
        acc_ref[...] = jnp.zeros_like(acc_ref)
        carry_ref[...] = jnp.zeros_like(carry_ref)

    page = k_ref.shape[0]
    r = lax.broadcasted_iota(jnp.int32, (page, page), 0)
    c = lax.broadcasted_iota(jnp.int32, (page, page), 1)
    upper = jnp.where(r >= c, 1.0, 0.0).astype(BF16)
    z = _dot_nt(qbd_ref[...], k_ref[...].astype(BF16)) * np.float32(dc ** -0.5) + off_ref[...]
    sp = _softplus_neg_abs(z)
    log_beta = jnp.minimum(z, 0.0) - sp
    log_rest = -jnp.maximum(z, 0.0) - sp
    incl = _split_dot(log_rest, upper)
    carry = carry_ref[...]
    w = jnp.exp(log_beta + incl - log_rest + carry)
    acc_ref[...] += _dot(w.astype(BF16), v_ref[...].astype(BF16))
    carry_ref[...] = carry + incl[:, 0:1]

    @pl.when(j == pl.num_programs(1) - 1)
    def _():
        o = jnp.sum(jnp.where(own, acc_ref[...], 0.0), axis=0, keepdims=True)
        o_ref[...] = (o * _silu(g_ref[...])).astype(o_ref.dtype)


def _paged_attention(qn_s, gate_s, cache_k, cache_v, page_table, sb_offset):
    s, width = qn_s.shape
    n_pages = page_table.shape[1]
    page = cache_k.shape[1]
    kv_map = lambda b, j, pt: (pt[b, n_pages - 1 - j], 0, 0)
    row = pl.BlockSpec((None, 1, width), lambda b, j, pt: (b, 0, 0))
    out = pl.pallas_call(
        _paged_kernel,
        out_shape=jax.ShapeDtypeStruct((s, 1, width), BF16),
        grid_spec=pltpu.PrefetchScalarGridSpec(
            num_scalar_prefetch=1,
            grid=(s, n_pages),
            in_specs=[row,
                      pl.BlockSpec((None, page, width), kv_map),
                      pl.BlockSpec((None, page, width), kv_map),
                      row,
                      pl.BlockSpec((HC, 1), lambda b, j, pt: (0, 0))],
            out_specs=row,
            scratch_shapes=[pltpu.VMEM((HC, width), BF16), pltpu.VMEM((HC, width), F32),
                            pltpu.VMEM((HC, 1), F32)]),
        compiler_params=_cparams(("arbitrary", "arbitrary")),
        name="paged_attention",
    )(page_table, qn_s.reshape(s, 1, width), cache_k, cache_v, gate_s.reshape(s, 1, width),
      sb_offset.reshape(HC, 1))
    return out.reshape(s, width)


def _rope_tables(pos, half):
    inv = ROPE_BASE ** (-jnp.arange(half, dtype=F32) / half)
    ang = pos.astype(F32)[:, None] * inv[None, :]
    cos, sin = jnp.cos(ang), jnp.sin(ang)
    return jnp.concatenate([cos, cos], axis=-1), jnp.concatenate([-sin, sin], axis=-1)


def kernel(x_prompt, x_sample, state_gla, state_ret, cache_k, cache_v, page_table, c_prompt, c_sample,
           e_norm_g, e_w_ada, e_b_ada, e_w_in, e_w_a2, e_b_a, e_gla_norm_g, e_ret_norm_g, e_w_out,
           o_norm_g, o_w_ada, o_b_ada, o_w_in, o_q_norm_g, o_k_norm_g, o_sb_offset, o_v_norm_g,
           o_w_s, o_b_s, o_w_out):
    bsz, t, d = x_prompt.shape
    s = x_sample.shape[0]
    br = d // 2
    n_past = page_table.shape[1] * cache_k.shape[2]
    dk = LANES

    c_rows = jnp.concatenate([c_sample, c_prompt, jnp.zeros((16 - s - bsz, d), F32)], axis=0)

    n_qkv = 2 * HA * dk + HA * 2 * dk
    pad = MXU_DIM - GLA_RANK
    cols_gla = (0, HA, 2 * HA, n_qkv // LANES, (n_qkv + MXU_DIM) // LANES)
    base_b = (n_qkv + MXU_DIM) // LANES + HA * 2
    cols_ret = (base_b, base_b + HB, base_b + 2 * HB, base_b + 2 * HB + HB * 2)
    cols_dec = cols_gla + cols_ret

    xp = x_prompt.reshape(bsz * t, d)
    xs = x_sample.reshape(s, d)
    pos_p = jnp.arange(t)
    pos_s = n_past + jnp.arange(x_sample.shape[1])
    cos_p, sin_p = _rope_tables(pos_p, dk // 2)
    cos_s, sin_s = _rope_tables(pos_s, dk // 2)
    log_gamma = jnp.log1p(-jnp.exp2(-5.0 - jnp.arange(HB, dtype=F32)))
    lg_rows = jnp.broadcast_to(log_gamma[:, None, None], (HB, 1, LANES))
    lg_row = jnp.repeat(log_gamma, dk).reshape(1, HB * dk)

    outs = {}
    i = 0
    mod = _ada(c_rows, e_w_ada[i], e_b_ada[i])
    mod_s, mod_p = mod[:s], mod[s:s + bsz].reshape(bsz, 1, 3 * d)
    w_in = e_w_in[i]
    w_in_b = jnp.concatenate([w_in[:, :n_qkv + GLA_RANK], jnp.zeros((d, pad), F32),
                              w_in[:, n_qkv + GLA_RANK:]], axis=1).astype(BF16)
    wa2_pad = jnp.concatenate([e_w_a2[i], jnp.zeros((LANES - GLA_RANK, HA * dk), F32)], axis=0).astype(BF16)
    w_out_b = e_w_out[i].astype(BF16)

    y = _proj(xp, mod_p, e_norm_g[i], w_in_b, t, 512, 1280)
    y3 = y.reshape(bsz, t, -1)
    oa, gla_p = _gla_prompt(y3, wa2_pad, e_b_a[i], e_gla_norm_g[i], cols_gla, 512)
    ob, ret_p = _ret_prompt(y3, cos_p, sin_p, lg_rows, e_ret_norm_g[i], cols_ret, 512)
    xp = _outproj(oa.reshape(bsz * t, br), 0, ob.reshape(bsz * t, br), 0, w_out_b, xp, mod_p, t, 512, 1024)

    y_s = _proj(xs, mod_s, e_norm_g[i], w_in_b, None, s, 1280)
    mixed_s, gla_s, ret_s = _even_decode(y_s, state_gla[i], state_ret[i], wa2_pad, e_b_a[i],
                                         e_gla_norm_g[i], e_ret_norm_g[i], cos_s, sin_s, lg_row, cols_dec)
    xs = _outproj(mixed_s, 0, mixed_s, 1, w_out_b, xs, mod_s, None, s, 1024)

    mod = _ada(c_rows, o_w_ada[i], o_b_ada[i])
    mod_s, mod_p = mod[:s], mod[s:s + bsz].reshape(bsz, 1, 3 * d)
    w_in_b = o_w_in[i].astype(BF16)
    w_out_b = o_w_out[i].astype(BF16)
    bs_full = jnp.repeat(jnp.transpose(o_b_s[i]), LANES, axis=1)

    y = _proj(xp, mod_p, o_norm_g[i], w_in_b, t, 512, 1024)
    y3 = y.reshape(bsz, t, -1)
    qn, kn, knb, vo, vb = _qkv_prompt(y, o_q_norm_g[i], o_k_norm_g[i], 1024)
    shp = (bsz, t, br)
    oc = _sb_prompt(qn.reshape(shp), knb.reshape(shp), vb.reshape(shp), y3, o_sb_offset[i],
                    3 * HC, 512, 512, 256)
    od = _cmlp_prompt(y3, o_v_norm_g[i], o_w_s[i], bs_full, (4, 5, 6), 256)
    xp = _outproj(oc.reshape(bsz * t, br), 0, od.reshape(bsz * t, br), 0, w_out_b, xp, mod_p, t, 512, 1024)

    y_s = _proj(xs, mod_s, o_norm_g[i], w_in_b, None, s, 1024)
    w0_row = jnp.repeat(o_w_s[i][:, 0, 0], LANES).reshape(1, br)
    b0_row = jnp.repeat(o_b_s[i][:, 0], LANES).reshape(1, br)
    qn_s, kn_s, vo_s, cv_s, od_s = _odd_decode(y_s, o_q_norm_g[i], o_k_norm_g[i], o_v_norm_g[i], w0_row, b0_row)
    pool = cache_k.shape[1]
    oc_s = _paged_attention(qn_s, y_s[:, 3 * br:4 * br], cache_k[i].reshape(pool, -1, br),
                            cache_v[i].reshape(pool, -1, br), page_table, o_sb_offset[i])
    xs = _outproj(oc_s, 0, od_s, 0, w_out_b, xs, mod_s, None, s, 1024)

    dc = LANES
    return (xp.reshape(bsz, t, d), xs.reshape(s, 1, d),
            gla_p[None], gla_s[None], ret_p[None], ret_s[None],
            kn.reshape(1, bsz, t, HC, dc), vo.reshape(1, bsz, t, HC, dc),
            kn_s.reshape(1, s, 1, HC, dc), vo_s.reshape(1, s, 1, HC, dc),
            cv_s.reshape(1, s, 1, br))
```

```python
import functools

import jax
import jax.numpy as jnp
import numpy as np
from jax import lax
from jax.experimental import pallas as pl
from jax.experimental.pallas import tpu as pltpu

F32 = jnp.float32
BF16 = jnp.bfloat16

HA = 4
HB = 4
HC = 8
DG = 8
GLA_RANK = 16
GLA_TAU = 16.0
GLA_CHUNK = 64
RET_CHUNK = 128
ROPE_BASE = 10000.0
D_CHUNK = 128
RMS_EPS = 1e-6

LANES = 128
MXU_DIM = 256
VMEM_LIMIT = 48 * 1024 * 1024
LOG2E = np.float32(np.log2(np.e))
PAGES_PER_STEP = 8


def _cparams(sem):
    return pltpu.CompilerParams(dimension_semantics=sem, vmem_limit_bytes=VMEM_LIMIT)


def _dot(a, b):
    return jnp.dot(a, b, preferred_element_type=F32)


def _dot_nt(a, b):
    return lax.dot_general(a, b, (((1,), (1,)), ((), ())), preferred_element_type=F32)


def _dot_tn(a, b):
    return lax.dot_general(a, b, (((0,), (0,)), ((), ())), preferred_element_type=F32)


def _sigmoid(x):
    return 1.0 / (1.0 + jnp.exp(-x))


def _silu(x):
    return x * _sigmoid(x)


def _gelu(x):
    c = np.float32(np.sqrt(2.0 / np.pi))
    return 0.5 * x * (1.0 + jnp.tanh(c * (x + 0.044715 * (x * x * x))))


def _softplus_neg_abs(x):
    return jnp.log(1.0 + jnp.exp(-jnp.abs(x)))


def _neg_log2_rest(z2):
    return jnp.maximum(z2, 0.0) + jnp.log(1.0 + jnp.exp2(-jnp.abs(z2))) * LOG2E


def _split_dot(x, ones_bf16, left=False):
    hi = x.astype(BF16)
    lo = (x - hi.astype(F32)).astype(BF16)
    if left:
        return _dot(ones_bf16, hi) + _dot(ones_bf16, lo)
    return _dot(hi, ones_bf16) + _dot(lo, ones_bf16)


def _rms(x, g):
    ms = jnp.mean(x * x, axis=-1, keepdims=True)
    return x * lax.rsqrt(ms + RMS_EPS) * g


def _ada_kernel(c_ref, w_ref, b_ref, o_ref):
    s = _silu(c_ref[...]).astype(BF16)
    o_ref[...] = _dot(s, w_ref[...].astype(BF16)) + b_ref[...]


def _ada(c_rows, w, b):
    r, d = c_rows.shape
    n = w.shape[1]
    tn = 512
    return pl.pallas_call(
        _ada_kernel,
        out_shape=jax.ShapeDtypeStruct((r, n), F32),
        grid=(n // tn,),
        in_specs=[pl.BlockSpec((r, d), lambda j: (0, 0)),
                  pl.BlockSpec((d, tn), lambda j: (0, j)),
                  pl.BlockSpec((1, tn), lambda j: (0, j))],
        out_specs=pl.BlockSpec((r, tn), lambda j: (0, j)),
        compiler_params=_cparams(("arbitrary",)),
        name="ada",
    )(c_rows, w, b.reshape(1, n))


def _proj_kernel(x_ref, shift_ref, scale_ref, g_ref, w_ref, o_ref, h_ref):
    @pl.when(pl.program_id(1) == 0)
    def _():
        h = _rms(x_ref[...], g_ref[...]) * (1.0 + scale_ref[...]) + shift_ref[...]
        h_ref[...] = h.astype(BF16)

    o_ref[...] = _dot(h_ref[...], w_ref[...])


def _proj(x2d, mod, norm_g, w_bf16, rows_per_batch, tm, tn):
    m, d = x2d.shape
    n = w_bf16.shape[1]
    if rows_per_batch is None:
        shift_spec = pl.BlockSpec((tm, d), lambda i, j: (i, 0))
        scale_spec = pl.BlockSpec((tm, d), lambda i, j: (i, 1))
    else:
        tpb = rows_per_batch // tm
        shift_spec = pl.BlockSpec((None, 1, d), lambda i, j: (i // tpb, 0, 0))
        scale_spec = pl.BlockSpec((None, 1, d), lambda i, j: (i // tpb, 0, 1))
    return pl.pallas_call(
        _proj_kernel,
        out_shape=jax.ShapeDtypeStruct((m, n), F32),
        grid=(m // tm, n // tn),
        in_specs=[pl.BlockSpec((tm, d), lambda i, j: (i, 0)),
                  shift_spec, scale_spec,
                  pl.BlockSpec((1, d), lambda i, j: (0, 0)),
                  pl.BlockSpec((d, tn), lambda i, j: (0, j))],
        out_specs=pl.BlockSpec((tm, tn), lambda i, j: (i, j)),
        scratch_shapes=[pltpu.VMEM((tm, d), BF16)],
        compiler_params=_cparams(("arbitrary", "arbitrary")),
        name="proj",
    )(x2d, mod, mod, norm_g.reshape(1, d), w_bf16)


def _outproj_kernel(ma_ref, mb_ref, wa_ref, wb_ref, x_ref, gate_ref, o_ref):
    mixed = _dot(ma_ref[...], wa_ref[...]) + _dot(mb_ref[...], wb_ref[...])
    o_ref[...] = x_ref[...] + gate_ref[...] * mixed


def _outproj(ma, ca, mb, cb, w_bf16, x2d, mod, rows_per_batch, tm, tn):
    m, d = x2d.shape
    half = w_bf16.shape[0] // 2
    gcol = 2 * (d // tn)
    if rows_per_batch is None:
        gate_spec = pl.BlockSpec((tm, tn), lambda i, j: (i, gcol + j))
    else:
        tpb = rows_per_batch // tm
        gate_spec = pl.BlockSpec((None, 1, tn), lambda i, j: (i // tpb, 0, gcol + j))
    return pl.pallas_call(
        _outproj_kernel,
        out_shape=jax.ShapeDtypeStruct((m, d), F32),
        grid=(m // tm, d // tn),
        in_specs=[pl.BlockSpec((tm, half), lambda i, j: (i, ca)),
                  pl.BlockSpec((tm, half), lambda i, j: (i, cb)),
                  pl.BlockSpec((half, tn), lambda i, j: (0, j)),
                  pl.BlockSpec((half, tn), lambda i, j: (1, j)),
                  pl.BlockSpec((tm, tn), lambda i, j: (i, j)),
                  gate_spec],
        out_specs=pl.BlockSpec((tm, tn), lambda i, j: (i, j)),
        compiler_params=_cparams(("arbitrary", "arbitrary")),
        name="outproj",
    )(ma, mb, w_bf16, w_bf16, x2d, mod)


def _gla_kernel(q_ref, k_ref, v_ref, ra_ref, g_ref, wa2_ref, ba_ref, ng_ref,
                o_ref, st_ref, stt_ref, *, n_sub, dk, dv):
    c = pl.program_id(1)
    cs = GLA_CHUNK

    @pl.when(c == 0)
    def _():
        stt_ref[...] = jnp.zeros_like(stt_ref)

    row = lax.broadcasted_iota(jnp.int32, (cs, cs), 0)
    col = lax.broadcasted_iota(jnp.int32, (cs, cs), 1)
    causal = row >= col
    tri = jnp.where(causal, 1.0, 0.0).astype(BF16)
    qscale = np.float32(dk ** -0.5)

    def body(i, carry):
        r0 = pl.multiple_of(i * cs, cs)
        rows = pl.ds(r0, cs)
        pre = _dot(ra_ref[rows, :].astype(BF16), wa2_ref[...]) + ba_ref[...]
        la = (jnp.minimum(pre, 0.0) - _softplus_neg_abs(pre)) * np.float32(1.0 / GLA_TAU)
        b = _split_dot(la, tri, left=True)
        bl = b[cs - 1:cs, :]
        k = k_ref[rows, :]
        qe = (q_ref[rows, :] * qscale * jnp.exp(b)).astype(BF16)
        ke = (k * jnp.exp(-b)).astype(BF16)
        kd = (k * jnp.exp(bl - b)).astype(BF16)
        ebl = jnp.exp(bl)
        for h in range(HA):
            ks, vs = slice(h * dk, (h + 1) * dk), slice(h * dv, (h + 1) * dv)
            v = v_ref[rows, vs].astype(BF16)
            s = jnp.where(causal, _dot_nt(qe[:, ks], ke[:, ks]), 0.0)
            stt = stt_ref[h]
            o = _dot(s.astype(BF16), v) + _dot_nt(qe[:, ks], stt.astype(BF16))
            stt_ref[h] = stt * ebl[:, ks] + _dot_tn(v, kd[:, ks])
            og = _rms(o, ng_ref[:, vs]) * _silu(g_ref[rows, vs])
            o_ref[rows, vs] = og.astype(o_ref.dtype)
        return carry

    lax.fori_loop(0, n_sub, body, 0)

    @pl.when(c == pl.num_programs(1) - 1)
    def _():
        for h in range(HA):
            st_ref[h] = stt_ref[h].T


def _gla_prompt(y3, wa2_pad, b_a, norm_g, tc):
    bsz, t, n = y3.shape
    dk, dv = LANES, 2 * LANES
    wk, wv = HA * dk, HA * dv
    kern = functools.partial(_gla_kernel, n_sub=tc // GLA_CHUNK, dk=dk, dv=dv)
    return pl.pallas_call(
        kern,
        out_shape=(jax.ShapeDtypeStruct((bsz, t, wv), BF16),
                   jax.ShapeDtypeStruct((bsz, HA, dk, dv), F32)),
        grid=(bsz, t // tc),
        in_specs=[pl.BlockSpec((None, tc, wk), lambda b, c: (b, c, 0)),
                  pl.BlockSpec((None, tc, wk), lambda b, c: (b, c, 1)),
                  pl.BlockSpec((None, tc, wv), lambda b, c: (b, c, 1)),
                  pl.BlockSpec((None, tc, LANES), lambda b, c: (b, c, (n - MXU_DIM) // LANES)),
                  pl.BlockSpec((None, tc, wv), lambda b, c: (b, c, 2)),
                  pl.BlockSpec((LANES, wk), lambda b, c: (0, 0)),
                  pl.BlockSpec((1, wk), lambda b, c: (0, 0)),
                  pl.BlockSpec((1, wv), lambda b, c: (0, 0))],
        out_specs=(pl.BlockSpec((None, tc, wv), lambda b, c: (b, c, 0)),
                   pl.BlockSpec((None, HA, dk, dv), lambda b, c: (b, 0, 0, 0))),
        scratch_shapes=[pltpu.VMEM((HA, dv, dk), F32)],
        compiler_params=_cparams(("arbitrary", "arbitrary")),
        name="gla_prompt",
    )(y3, y3, y3, y3, y3, wa2_pad, b_a.reshape(1, -1), norm_g.reshape(1, -1))


def _rot(x, cosf, sinf):
    return x * cosf + pltpu.roll(x, x.shape[-1] // 2, 1) * sinf


def _ret_kernel(q_ref, k_ref, v_ref, g_ref, cos_ref, sin_ref, lg_ref, ng_ref,
                o_ref, st_ref, stt_ref, *, n_sub, dk, dv):
    c = pl.program_id(1)
    cs = RET_CHUNK

    @pl.when(c == 0)
    def _():
        stt_ref[...] = jnp.zeros_like(stt_ref)

    row = lax.broadcasted_iota(jnp.int32, (cs, cs), 0)
    col = lax.broadcasted_iota(jnp.int32, (cs, cs), 1)
    rel = (row - col).astype(F32)
    ridx = lax.broadcasted_iota(jnp.int32, (cs, 1), 0).astype(F32)
    lgs = [lg_ref[:, h * dk:h * dk + 1] for h in range(HB)]
    decay = [jnp.where(row >= col, jnp.exp(rel * lg), 0.0) for lg in lgs]
    inter = [jnp.exp((ridx + 1.0) * lg) for lg in lgs]
    kdec = [jnp.exp((np.float32(cs - 1.0) - ridx) * lg) for lg in lgs]
    sdec = [jnp.exp(np.float32(cs) * lg) for lg in lgs]
    qscale = np.float32(dk ** -0.5)

    def body(i, carry):
        r0 = pl.multiple_of(i * cs, cs)
        rows = pl.ds(r0, cs)
        cosf = cos_ref[rows, :]
        sinf = sin_ref[rows, :]
        for h in range(HB):
            ks, vs = slice(h * dk, (h + 1) * dk), slice(h * dv, (h + 1) * dv)
            qr = (_rot(q_ref[rows, ks], cosf, sinf) * qscale).astype(BF16)
            kr = _rot(k_ref[rows, ks], cosf, sinf)
            v = v_ref[rows, vs].astype(BF16)
            s = _dot_nt(qr, kr.astype(BF16)) * decay[h]
            stt = stt_ref[h]
            o = _dot(s.astype(BF16), v) + _dot_nt(qr, stt.astype(BF16)) * inter[h]
            stt_ref[h] = stt * sdec[h] + _dot_tn(v, (kr * kdec[h]).astype(BF16))
            og = _rms(o, ng_ref[:, vs]) * _silu(g_ref[rows, vs])
            o_ref[rows, vs] = og.astype(o_ref.dtype)
        return carry

    lax.fori_loop(0, n_sub, body, 0)

    @pl.when(c == pl.num_programs(1) - 1)
    def _():
        for h in range(HB):
            st_ref[h] = stt_ref[h].T


def _ret_prompt(y3, cosf, sinf, lg_row, norm_g, col0, tc):
    bsz, t, _ = y3.shape
    dk, dv = LANES, 2 * LANES
    wk, wv = HB * dk, HB * dv
    cq, cv = col0 // wk, (col0 + 2 * wk) // wv
    kern = functools.partial(_ret_kernel, n_sub=tc // RET_CHUNK, dk=dk, dv=dv)
    return pl.pallas_call(
        kern,
        out_shape=(jax.ShapeDtypeStruct((bsz, t, wv), BF16),
                   jax.ShapeDtypeStruct((bsz, HB, dk, dv), F32)),
        grid=(bsz, t // tc),
        in_specs=[pl.BlockSpec((None, tc, wk), lambda b, c: (b, c, cq)),
                  pl.BlockSpec((None, tc, wk), lambda b, c: (b, c, cq + 1)),
                  pl.BlockSpec((None, tc, wv), lambda b, c: (b, c, cv)),
                  pl.BlockSpec((None, tc, wv), lambda b, c: (b, c, cv + 1)),
                  pl.BlockSpec((tc, dk), lambda b, c: (c, 0)),
                  pl.BlockSpec((tc, dk), lambda b, c: (c, 0)),
                  pl.BlockSpec((1, wk), lambda b, c: (0, 0)),
                  pl.BlockSpec((1, wv), lambda b, c: (0, 0))],
        out_specs=(pl.BlockSpec((None, tc, wv), lambda b, c: (b, c, 0)),
                   pl.BlockSpec((None, HB, dk, dv), lambda b, c: (b, 0, 0, 0))),
        scratch_shapes=[pltpu.VMEM((HB, dv, dk), F32)],
        compiler_params=_cparams(("arbitrary", "arbitrary")),
        name="ret_prompt",
    )(y3, y3, y3, y3, cosf, sinf, lg_row, norm_g.reshape(1, -1))


def _row_to_col(x_row):
    n = x_row.shape[-1]
    r = lax.broadcasted_iota(jnp.int32, (n, n), 0)
    c = lax.broadcasted_iota(jnp.int32, (n, n), 1)
    return jnp.sum(jnp.where(r == c, jnp.broadcast_to(x_row, (n, n)), 0.0), axis=1, keepdims=True)


def _even_decode_kernel(y_ref, sg_ref, sr_ref, wa2_ref, ba_ref, gng_ref, rng_ref, cos_ref, sin_ref,
                        lg_ref, o_ref, sgo_ref, sro_ref, *, cols):
    dk, dv = LANES, 2 * LANES
    cqa, cka, cva, cga, cqb, ckb, cvb, cgb, cra = cols
    y = y_ref[...]
    pre = _dot(y[:, cra:cra + LANES].astype(BF16), wa2_ref[...]) + ba_ref[...]
    la = (jnp.minimum(pre, 0.0) - _softplus_neg_abs(pre)) * np.float32(1.0 / GLA_TAU)
    alpha = jnp.exp(la)
    cosf = cos_ref[...]
    sinf = sin_ref[...]
    gam = jnp.exp(lg_ref[...])
    outs = []
    for h in range(HA):
        q = y[:, cqa + h * dk:cqa + (h + 1) * dk] * np.float32(dk ** -0.5)
        k = y[:, cka + h * dk:cka + (h + 1) * dk]
        v = y[:, cva + h * dv:cva + (h + 1) * dv]
        s_new = _row_to_col(alpha[:, h * dk:(h + 1) * dk]) * sg_ref[h] + _row_to_col(k) * v
        sgo_ref[h] = s_new
        o = jnp.sum(_row_to_col(q) * s_new, axis=0, keepdims=True)
        g = y[:, cga + h * dv:cga + (h + 1) * dv]
        outs.append(_rms(o, gng_ref[:, h * dv:(h + 1) * dv]) * _silu(g))
    for h in range(HB):
        q = _rot(y[:, cqb + h * dk:cqb + (h + 1) * dk], cosf, sinf) * np.float32(dk ** -0.5)
        k = _rot(y[:, ckb + h * dk:ckb + (h + 1) * dk], cosf, sinf)
        v = y[:, cvb + h * dv:cvb + (h + 1) * dv]
        s_new = _row_to_col(gam[:, h * dk:(h + 1) * dk]) * sr_ref[h] + _row_to_col(k) * v
        sro_ref[h] = s_new
        o = jnp.sum(_row_to_col(q) * s_new, axis=0, keepdims=True)
        g = y[:, cgb + h * dv:cgb + (h + 1) * dv]
        outs.append(_rms(o, rng_ref[:, h * dv:(h + 1) * dv]) * _silu(g))
    o_ref[...] = jnp.concatenate(outs, axis=-1).astype(o_ref.dtype)


def _even_decode(y_s, state_gla, state_ret, wa2_pad, b_a, gla_ng, ret_ng, cos_row, sin_row, lg_row, cols):
    bsz, n = y_s.shape
    dk, dv = LANES, 2 * LANES
    wide = (HA + HB) * dv
    kern = functools.partial(_even_decode_kernel, cols=cols)
    full = lambda a: pl.BlockSpec(a.shape, lambda b: (0,) * a.ndim)
    b_a2, gng, rng_ = b_a.reshape(1, -1), gla_ng.reshape(1, -1), ret_ng.reshape(1, -1)
    mixed, sg, sr = pl.pallas_call(
        kern,
        out_shape=(jax.ShapeDtypeStruct((bsz, 1, wide), BF16),
                   jax.ShapeDtypeStruct(state_gla.shape, F32),
                   jax.ShapeDtypeStruct(state_ret.shape, F32)),
        grid=(bsz,),
        in_specs=[pl.BlockSpec((None, 1, n), lambda b: (b, 0, 0)),
                  pl.BlockSpec((None, HA, dk, dv), lambda b: (b, 0, 0, 0)),
                  pl.BlockSpec((None, HB, dk, dv), lambda b: (b, 0, 0, 0)),
                  full(wa2_pad), full(b_a2), full(gng), full(rng_), full(cos_row), full(sin_row), full(lg_row)],
        out_specs=(pl.BlockSpec((None, 1, wide), lambda b: (b, 0, 0)),
                   pl.BlockSpec((None, HA, dk, dv), lambda b: (b, 0, 0, 0)),
                   pl.BlockSpec((None, HB, dk, dv), lambda b: (b, 0, 0, 0))),
        compiler_params=_cparams(("arbitrary",)),
        name="even_decode",
    )(y_s.reshape(bsz, 1, n), state_gla, state_ret, wa2_pad, b_a2, gng, rng_, cos_row, sin_row, lg_row)
    return mixed.reshape(bsz, wide), sg, sr


def _qkv_kernel(q_ref, k_ref, v_ref, qg_ref, kg_ref, qn_ref, kn_ref, knb_ref, vo_ref, vb_ref):
    qscale = np.float32(q_ref.shape[-1] ** -0.5) * LOG2E
    qn_ref[...] = (_rms(q_ref[...], qg_ref[...]) * qscale).astype(BF16)
    kn = _rms(k_ref[...], kg_ref[...])
    kn_ref[...] = kn
    knb_ref[...] = kn.astype(BF16)
    v = v_ref[...]
    vo_ref[...] = v
    vb_ref[...] = v.astype(BF16)


def _qkv_prompt(y2, q_g, k_g, tm):
    m = y2.shape[0]
    dc = LANES
    blk = lambda off: pl.BlockSpec((tm, dc), lambda i, h: (i, off + h))
    gspec = pl.BlockSpec((1, dc), lambda i, h: (0, 0))
    return pl.pallas_call(
        _qkv_kernel,
        out_shape=(jax.ShapeDtypeStruct((m, HC * dc), BF16),
                   jax.ShapeDtypeStruct((m, HC * dc), F32),
                   jax.ShapeDtypeStruct((m, HC * dc), BF16),
                   jax.ShapeDtypeStruct((m, HC * dc), F32),
                   jax.ShapeDtypeStruct((m, HC * dc), BF16)),
        grid=(m // tm, HC),
        in_specs=[blk(0), blk(HC), blk(2 * HC), gspec, gspec],
        out_specs=(blk(0), blk(0), blk(0), blk(0), blk(0)),
        compiler_params=_cparams(("arbitrary", "arbitrary")),
        name="qkv_prompt",
    )(y2, y2, y2, q_g.reshape(1, dc), k_g.reshape(1, dc))


def _sb_kernel(qi_ref, kb_ref, off_ref, q_ref, k_ref, v_ref, g_ref, o_ref, acc_ref, run_ref, *, tk, sub):
    h = pl.program_id(1)
    p = pl.program_id(2)
    qi = qi_ref[p]
    kb = kb_ref[p]
    tq = q_ref.shape[0]

    @pl.when(kb == qi)
    def _():
        acc_ref[...] = jnp.zeros_like(acc_ref)
        run_ref[...] = jnp.zeros_like(run_ref)

    def block(masked):
        off = off_ref[h] * LOG2E
        r = lax.broadcasted_iota(jnp.int32, (sub, sub), 0)
        c = lax.broadcasted_iota(jnp.int32, (sub, sub), 1)
        upper = jnp.where(r >= c, 1.0, 0.0).astype(BF16)
        q = q_ref[...]
        acc = acc_ref[...]
        run = run_ref[...]
        for s_i in reversed(range(tk // sub)):
            ks = k_ref[s_i * sub:(s_i + 1) * sub, :]
            vs = v_ref[s_i * sub:(s_i + 1) * sub, :]
            z = _dot_nt(q, ks) + off
            rest = _neg_log2_rest(z)
            if masked:
                qrow = lax.broadcasted_iota(jnp.int32, (tq, sub), 0)
                kcol = lax.broadcasted_iota(jnp.int32, (tq, sub), 1) + s_i * sub
                mask = kcol < qrow
                rest = jnp.where(mask, rest, 0.0)
            incl = _split_dot(rest, upper)
            w = jnp.exp2(z - incl - run)
            if masked:
                w = jnp.where(mask, w, 0.0)
            acc = acc + _dot(w.astype(BF16), vs)
            run = run + incl[:, 0:1]
        acc_ref[...] = acc
        run_ref[...] = run

    pl.when(kb == qi)(functools.partial(block, True))
    pl.when(kb < qi)(functools.partial(block, False))

    @pl.when(kb == 0)
    def _():
        o_ref[...] = (acc_ref[...] * _silu(g_ref[...])).astype(o_ref.dtype)


def _sb_prompt(qn, kn, vb, y3, sb_offset, cg, tq, sub):
    bsz, t, _ = qn.shape
    dc = LANES
    nq = t // tq
    pairs = [(qi, kb) for qi in range(nq) for kb in range(qi, -1, -1)]
    qi_tab = jnp.asarray([p[0] for p in pairs], jnp.int32)
    kb_tab = jnp.asarray([p[1] for p in pairs], jnp.int32)
    kern = functools.partial(_sb_kernel, tk=tq, sub=sub)
    q_map = lambda b, h, p, qi, kb, off: (b, qi[p], h)
    kv_map = lambda b, h, p, qi, kb, off: (b, kb[p], h)
    return pl.pallas_call(
        kern,
        out_shape=jax.ShapeDtypeStruct((bsz, t, HC * dc), BF16),
        grid_spec=pltpu.PrefetchScalarGridSpec(
            num_scalar_prefetch=3,
            grid=(bsz, HC, len(pairs)),
            in_specs=[pl.BlockSpec((None, tq, dc), q_map),
                      pl.BlockSpec((None, tq, dc), kv_map),
                      pl.BlockSpec((None, tq, dc), kv_map),
                      pl.BlockSpec((None, tq, dc), lambda b, h, p, qi, kb, off: (b, qi[p], cg + h))],
            out_specs=pl.BlockSpec((None, tq, dc), q_map),
            scratch_shapes=[pltpu.VMEM((tq, dc), F32), pltpu.VMEM((tq, 1), F32)]),
        compiler_params=_cparams(("arbitrary",) * 3),
        name="sb_prompt",
    )(qi_tab, kb_tab, sb_offset, qn, kn, vb, y3)


def _cmlp_kernel(u_ref, v_ref, g_ref, vg_ref, ws_ref, bs_ref, o_ref, *, n_chunk):
    cs = D_CHUNK
    u = _gelu(u_ref[...])
    vn = _rms(_gelu(v_ref[...]), vg_ref[...]).astype(BF16)
    gate = _silu(g_ref[...])
    bs = bs_ref[...]
    r = lax.broadcasted_iota(jnp.int32, (cs, cs), 0)
    c = lax.broadcasted_iota(jnp.int32, (cs, cs), 1)
    for g in range(DG):
        w = jnp.where(r >= c, ws_ref[g], 0.0).astype(BF16)
        lo, hi = g * LANES, (g + 1) * LANES
        for ch in range(n_chunk):
            r0, r1 = ch * cs, (ch + 1) * cs
            mixed = _dot(w, vn[r0:r1, lo:hi]) + bs[:, lo:hi]
            o_ref[r0:r1, lo:hi] = (u[r0:r1, lo:hi] * mixed * gate[r0:r1, lo:hi]).astype(o_ref.dtype)


def _cmlp_prompt(y3, v_norm_g, w_s, bs_full, cols, tm):
    bsz, t, _ = y3.shape
    br = DG * LANES
    cu, cv, cg = cols
    kern = functools.partial(_cmlp_kernel, n_chunk=tm // D_CHUNK)
    return pl.pallas_call(
        kern,
        out_shape=jax.ShapeDtypeStruct((bsz, t, br), BF16),
        grid=(bsz, t // tm),
        in_specs=[pl.BlockSpec((None, tm, br), lambda b, i: (b, i, cu)),
                  pl.BlockSpec((None, tm, br), lambda b, i: (b, i, cv)),
                  pl.BlockSpec((None, tm, br), lambda b, i: (b, i, cg)),
                  pl.BlockSpec((1, br), lambda b, i: (0, 0)),
                  pl.BlockSpec((DG, D_CHUNK, D_CHUNK), lambda b, i: (0, 0, 0)),
                  pl.BlockSpec((D_CHUNK, br), lambda b, i: (0, 0))],
        out_specs=pl.BlockSpec((None, tm, br), lambda b, i: (b, i, 0)),
        compiler_params=_cparams(("arbitrary", "arbitrary")),
        name="cmlp_prompt",
    )(y3, y3, y3, v_norm_g.reshape(1, br), w_s, bs_full)


def _odd_decode_kernel(y_ref, qg_ref, kg_ref, vg_ref, w0_ref, b0_ref, qn_ref, kn_ref, vo_ref, cv_ref, od_ref):
    br = HC * LANES
    y = y_ref[...]
    for h in range(HC):
        lo, hi = h * LANES, (h + 1) * LANES
        qn_ref[:, lo:hi] = _rms(y[:, lo:hi], qg_ref[...])
        kn_ref[:, lo:hi] = _rms(y[:, br + lo:br + hi], kg_ref[...])
    vo_ref[...] = y[:, 2 * br:3 * br]
    u = _gelu(y[:, 4 * br:5 * br])
    vn = _rms(_gelu(y[:, 5 * br:6 * br]), vg_ref[...])
    cv_ref[...] = vn
    od = u * (w0_ref[...] * vn + b0_ref[...]) * _silu(y[:, 6 * br:7 * br])
    od_ref[...] = od.astype(od_ref.dtype)


def _odd_decode(y_s, q_g, k_g, v_norm_g, w0_row, b0_row):
    s = y_s.shape[0]
    br = HC * LANES
    args = (y_s, q_g.reshape(1, -1), k_g.reshape(1, -1), v_norm_g.reshape(1, -1), w0_row, b0_row)
    full = lambda a: pl.BlockSpec(a.shape, lambda i: (0,) * a.ndim)
    osd = lambda dt: jax.ShapeDtypeStruct((s, br), dt)
    ospec = pl.BlockSpec((s, br), lambda i: (0, 0))
    return pl.pallas_call(
        _odd_decode_kernel,
        out_shape=(osd(F32), osd(F32), osd(F32), osd(F32), osd(BF16)),
        grid=(1,),
        in_specs=[full(a) for a in args],
        out_specs=(ospec,) * 5,
        compiler_params=_cparams(("arbitrary",)),
        name="odd_decode",
    )(*args)


def _strided_suffix_sums(x, stride):
    n = x.shape[-1]
    lane = lax.broadcasted_iota(jnp.int32, x.shape, 1)
    inc, tot = x, x
    k = stride
    while k < n:
        inc = inc + jnp.where(lane < n - k, pltpu.roll(inc, n - k, 1), 0.0)
        tot = tot + pltpu.roll(tot, n - k, 1)
        k *= 2
    return inc, tot


def _paged_kernel(pt_ref, q_ref, g_ref, off_ref, *refs, n_pp):
    k_refs, v_refs = refs[:n_pp], refs[n_pp:2 * n_pp]
    o_ref, acc_ref, run_ref = refs[2 * n_pp:]
    j = pl.program_id(1)
    page, hc, dc = k_refs[0].shape
    rows = page * hc
    lane = lax.broadcasted_iota(jnp.int32, (hc, rows), 1)
    own = (lane & (hc - 1)) == lax.broadcasted_iota(jnp.int32, (hc, rows), 0)

    @pl.when(j == 0)
    def _():
        acc_ref[...] = jnp.zeros_like(acc_ref)
        run_ref[...] = jnp.zeros_like(run_ref)

    q = (q_ref[...] * (np.float32(dc ** -0.5) * LOG2E)).astype(BF16)
    off = off_ref[...] * LOG2E
    acc = acc_ref[...]
    run = run_ref[...]
    for p in reversed(range(n_pp)):
        kf = k_refs[p][...].reshape(rows, dc).astype(BF16)
        s = _dot_nt(q, kf)
        z = jnp.sum(jnp.where(own, s, 0.0), axis=0, keepdims=True) + off
        incl, tot = _strided_suffix_sums(_neg_log2_rest(z), hc)
        w = jnp.exp2(z - incl - run)
        run = run + tot
        wm = jnp.where(own, jnp.broadcast_to(w, (hc, rows)), 0.0).astype(BF16)
        acc = acc + _dot(wm, v_refs[p][...].reshape(rows, dc).astype(BF16))
    acc_ref[...] = acc
    run_ref[...] = run

    @pl.when(j == pl.num_programs(1) - 1)
    def _():
        o_ref[...] = (acc_ref[...] * _silu(g_ref[...])).astype(o_ref.dtype)


def _paged_attention(qn_s, gate_s, cache_k, cache_v, layer, page_table, sb_offset):
    s, hc, dc = qn_s.shape
    n_pages = page_table.shape[1]
    page = cache_k.shape[2]
    n_pp = PAGES_PER_STEP
    assert hc & (hc - 1) == 0 and n_pages % n_pp == 0
    rows = page * hc

    def kv_spec(p):
        return pl.BlockSpec((None, None, page, hc, dc),
                            lambda b, j, pt: (layer, pt[b, n_pages - (j + 1) * n_pp + p], 0, 0, 0))

    head_blk = pl.BlockSpec((None, hc, dc), lambda b, j, pt: (b, 0, 0))
    return pl.pallas_call(
        functools.partial(_paged_kernel, n_pp=n_pp),
        out_shape=jax.ShapeDtypeStruct((s, hc, dc), BF16),
        grid_spec=pltpu.PrefetchScalarGridSpec(
            num_scalar_prefetch=1,
            grid=(s, n_pages // n_pp),
            in_specs=[head_blk, head_blk, pl.BlockSpec((1, rows), lambda b, j, pt: (0, 0))]
            + [kv_spec(p) for p in range(n_pp)] * 2,
            out_specs=head_blk,
            scratch_shapes=[pltpu.VMEM((hc, dc), F32), pltpu.VMEM((1, rows), F32)]),
        compiler_params=_cparams(("arbitrary", "arbitrary")),
        name="paged_attention",
    )(page_table, qn_s, gate_s, jnp.tile(sb_offset, page).reshape(1, rows),
      *([cache_k] * n_pp), *([cache_v] * n_pp))


def _rope_tables(pos, half):
    inv = ROPE_BASE ** (-jnp.arange(half, dtype=F32) / half)
    ang = pos.astype(F32)[:, None] * inv[None, :]
    cos, sin = jnp.cos(ang), jnp.sin(ang)
    return jnp.concatenate([cos, cos], axis=-1), jnp.concatenate([-sin, sin], axis=-1)


def kernel(x_prompt, x_sample, state_gla, state_ret, cache_k, cache_v, page_table, c_prompt, c_sample,
           e_norm_g, e_w_ada, e_b_ada, e_w_in, e_w_a2, e_b_a, e_gla_norm_g, e_ret_norm_g, e_w_out,
           o_norm_g, o_w_ada, o_b_ada, o_w_in, o_q_norm_g, o_k_norm_g, o_sb_offset, o_v_norm_g,
           o_w_s, o_b_s, o_w_out):
    bsz, t, d = x_prompt.shape
    s = x_sample.shape[0]
    br = d // 2
    n_past = page_table.shape[1] * cache_k.shape[2]
    dk = LANES

    c_rows = jnp.concatenate([c_sample, c_prompt, jnp.zeros((16 - s - bsz, d), F32)], axis=0)

    n_qkv = 2 * HA * dk + HA * 2 * dk
    n_main = e_w_in.shape[2] - GLA_RANK
    col_ret = n_qkv + HA * 2 * dk
    cols_dec = (0, HA * dk, 2 * HA * dk, n_qkv,
                col_ret, col_ret + HB * dk, col_ret + 2 * HB * dk, col_ret + 2 * HB * dk + HB * 2 * dk,
                n_main)

    xp = x_prompt.reshape(bsz * t, d)
    xs = x_sample.reshape(s, d)
    pos_p = jnp.arange(t)
    pos_s = n_past + jnp.arange(x_sample.shape[1])
    cos_p, sin_p = _rope_tables(pos_p, dk // 2)
    cos_s, sin_s = _rope_tables(pos_s, dk // 2)
    log_gamma = jnp.log1p(-jnp.exp2(-5.0 - jnp.arange(HB, dtype=F32)))
    lg_row = jnp.repeat(log_gamma, dk).reshape(1, HB * dk)

    i = 0
    mod = _ada(c_rows, e_w_ada[i], e_b_ada[i])
    mod_s, mod_p = mod[:s], mod[s:s + bsz].reshape(bsz, 1, 3 * d)
    w_in = e_w_in[i]
    w_in_b = jnp.concatenate([w_in[:, :n_qkv], w_in[:, n_qkv + GLA_RANK:], w_in[:, n_qkv:n_qkv + GLA_RANK],
                              jnp.zeros((d, MXU_DIM - GLA_RANK), F32)], axis=1).astype(BF16)
    wa2_pad = jnp.concatenate([e_w_a2[i], jnp.zeros((LANES - GLA_RANK, HA * dk), F32)], axis=0).astype(BF16)
    w_out_b = e_w_out[i].astype(BF16)

    y = _proj(xp, mod_p, e_norm_g[i], w_in_b, t, 512, 1280)
    y3 = y.reshape(bsz, t, -1)
    oa, gla_p = _gla_prompt(y3, wa2_pad, e_b_a[i], e_gla_norm_g[i], 512)
    ob, ret_p = _ret_prompt(y3, cos_p, sin_p, lg_row, e_ret_norm_g[i], col_ret, 512)
    xp = _outproj(oa.reshape(bsz * t, br), 0, ob.reshape(bsz * t, br), 0, w_out_b, xp, mod_p, t, 512, 1024)

    y_s = _proj(xs, mod_s, e_norm_g[i], w_in_b, None, s, 1280)
    mixed_s, gla_s, ret_s = _even_decode(y_s, state_gla[i], state_ret[i], wa2_pad, e_b_a[i],
                                         e_gla_norm_g[i], e_ret_norm_g[i], cos_s, sin_s, lg_row, cols_dec)
    xs = _outproj(mixed_s, 0, mixed_s, 1, w_out_b, xs, mod_s, None, s, 1024)

    mod = _ada(c_rows, o_w_ada[i], o_b_ada[i])
    mod_s, mod_p = mod[:s], mod[s:s + bsz].reshape(bsz, 1, 3 * d)
    w_in_b = o_w_in[i].astype(BF16)
    w_out_b = o_w_out[i].astype(BF16)
    bs_full = jnp.repeat(jnp.transpose(o_b_s[i]), LANES, axis=1)

    y = _proj(xp, mod_p, o_norm_g[i], w_in_b, t, 512, 1024)
    y3 = y.reshape(bsz, t, -1)
    qn, kn, knb, vo, vb = _qkv_prompt(y, o_q_norm_g[i], o_k_norm_g[i], 1024)
    shp = (bsz, t, br)
    oc = _sb_prompt(qn.reshape(shp), knb.reshape(shp), vb.reshape(shp), y3, o_sb_offset[i], 3 * HC, 512, 256)
    od = _cmlp_prompt(y3, o_v_norm_g[i], o_w_s[i], bs_full, (4, 5, 6), 256)
    xp = _outproj(oc.reshape(bsz * t, br), 0, od.reshape(bsz * t, br), 0, w_out_b, xp, mod_p, t, 512, 1024)

    y_s = _proj(xs, mod_s, o_norm_g[i], w_in_b, None, s, 1024)
    w0_row = jnp.repeat(o_w_s[i][:, 0, 0], LANES).reshape(1, br)
    b0_row = jnp.repeat(o_b_s[i][:, 0], LANES).reshape(1, br)
    qn_s, kn_s, vo_s, cv_s, od_s = _odd_decode(y_s, o_q_norm_g[i], o_k_norm_g[i], o_v_norm_g[i], w0_row, b0_row)
    dc = LANES
    oc_s = _paged_attention(qn_s.reshape(s, HC, dc), y_s[:, 3 * br:4 * br].reshape(s, HC, dc),
                            cache_k, cache_v, i, page_table, o_sb_offset[i]).reshape(s, br)
    xs = _outproj(oc_s, 0, od_s, 0, w_out_b, xs, mod_s, None, s, 1024)

    return (xp.reshape(bsz, t, d), xs.reshape(s, 1, d),
            gla_p[None], gla_s[None], ret_p[None], ret_s[None],
            kn.reshape(1, bsz, t, HC, dc), vo.reshape(1, bsz, t, HC, dc),
            kn_s.reshape(1, s, 1, HC, dc), vo_s.reshape(1, s, 1, HC, dc),
            cv_s.reshape(1, s, 1, br))
```

```python
import functools

import jax
import jax.numpy as jnp
import numpy as np
from jax import lax
from jax.experimental import pallas as pl
from jax.experimental.pallas import tpu as pltpu

F32 = jnp.float32
BF16 = jnp.bfloat16

HA = 4
HB = 4
HC = 8
DG = 8
GLA_RANK = 16
GLA_TAU = 16.0
GLA_CHUNK = 64
RET_CHUNK = 128
ROPE_BASE = 10000.0
D_CHUNK = 128
RMS_EPS = 1e-6

LANES = 128
MXU_DIM = 256
VMEM_LIMIT = 56 * 1024 * 1024
LOG2E = np.float32(np.log2(np.e))
PAGES_PER_STEP = 8


def _cparams(sem):
    return pltpu.CompilerParams(dimension_semantics=sem, vmem_limit_bytes=VMEM_LIMIT)


def _dot(a, b):
    return jnp.dot(a, b, preferred_element_type=F32)


def _dot_nt(a, b):
    return lax.dot_general(a, b, (((1,), (1,)), ((), ())), preferred_element_type=F32)


def _dot_tn(a, b):
    return lax.dot_general(a, b, (((0,), (0,)), ((), ())), preferred_element_type=F32)


def _sigmoid(x):
    return 1.0 / (1.0 + jnp.exp(-x))


def _silu(x):
    return x * _sigmoid(x)


def _gelu(x):
    c = np.float32(np.sqrt(2.0 / np.pi))
    return 0.5 * x * (1.0 + jnp.tanh(c * (x + 0.044715 * (x * x * x))))


def _softplus_neg_abs(x):
    return jnp.log(1.0 + jnp.exp(-jnp.abs(x)))


def _neg_log2_rest(z2):
    return jnp.maximum(z2, 0.0) + jnp.log(1.0 + jnp.exp2(-jnp.abs(z2))) * LOG2E


def _split_dot(x, ones_bf16, left=False):
    hi = x.astype(BF16)
    lo = (x - hi.astype(F32)).astype(BF16)
    if left:
        return _dot(ones_bf16, hi) + _dot(ones_bf16, lo)
    return _dot(hi, ones_bf16) + _dot(lo, ones_bf16)


def _rms(x, g):
    ms = jnp.mean(x * x, axis=-1, keepdims=True)
    return x * lax.rsqrt(ms + RMS_EPS) * g


def _ada_kernel(c_ref, w_ref, b_ref, o_ref):
    s = _silu(c_ref[...]).astype(BF16)
    o_ref[...] = _dot(s, w_ref[...].astype(BF16)) + b_ref[...]


def _ada(c_rows, w, b):
    r, d = c_rows.shape
    n = w.shape[1]
    tn = 512
    return pl.pallas_call(
        _ada_kernel,
        out_shape=jax.ShapeDtypeStruct((r, n), F32),
        grid=(n // tn,),
        in_specs=[pl.BlockSpec((r, d), lambda j: (0, 0)),
                  pl.BlockSpec((d, tn), lambda j: (0, j)),
                  pl.BlockSpec((1, tn), lambda j: (0, j))],
        out_specs=pl.BlockSpec((r, tn), lambda j: (0, j)),
        compiler_params=_cparams(("arbitrary",)),
        name="ada",
    )(c_rows, w, b.reshape(1, n))


def _proj_kernel(x_ref, shift_ref, scale_ref, g_ref, w_ref, o_ref, h_ref):
    @pl.when(pl.program_id(1) == 0)
    def _():
        h = _rms(x_ref[...], g_ref[...]) * (1.0 + scale_ref[...]) + shift_ref[...]
        h_ref[...] = h.astype(BF16)

    o_ref[...] = _dot(h_ref[...], w_ref[...])


def _proj(x2d, mod, norm_g, w_bf16, rows_per_batch, tm, tn):
    m, d = x2d.shape
    n = w_bf16.shape[1]
    if rows_per_batch is None:
        shift_spec = pl.BlockSpec((tm, d), lambda i, j: (i, 0))
        scale_spec = pl.BlockSpec((tm, d), lambda i, j: (i, 1))
    else:
        tpb = rows_per_batch // tm
        shift_spec = pl.BlockSpec((None, 1, d), lambda i, j: (i // tpb, 0, 0))
        scale_spec = pl.BlockSpec((None, 1, d), lambda i, j: (i // tpb, 0, 1))
    return pl.pallas_call(
        _proj_kernel,
        out_shape=jax.ShapeDtypeStruct((m, n), F32),
        grid=(m // tm, n // tn),
        in_specs=[pl.BlockSpec((tm, d), lambda i, j: (i, 0)),
                  shift_spec, scale_spec,
                  pl.BlockSpec((1, d), lambda i, j: (0, 0)),
                  pl.BlockSpec((d, tn), lambda i, j: (0, j))],
        out_specs=pl.BlockSpec((tm, tn), lambda i, j: (i, j)),
        scratch_shapes=[pltpu.VMEM((tm, d), BF16)],
        compiler_params=_cparams(("arbitrary", "arbitrary")),
        name="proj",
    )(x2d, mod, mod, norm_g.reshape(1, d), w_bf16)


def _outproj_kernel(ma_ref, mb_ref, wa_ref, wb_ref, x_ref, gate_ref, o_ref):
    mixed = _dot(ma_ref[...], wa_ref[...]) + _dot(mb_ref[...], wb_ref[...])
    o_ref[...] = x_ref[...] + gate_ref[...] * mixed


def _outproj(ma, ca, mb, cb, w_bf16, x2d, mod, rows_per_batch, tm, tn):
    m, d = x2d.shape
    half = w_bf16.shape[0] // 2
    gcol = 2 * (d // tn)
    if rows_per_batch is None:
        gate_spec = pl.BlockSpec((tm, tn), lambda i, j: (i, gcol + j))
    else:
        tpb = rows_per_batch // tm
        gate_spec = pl.BlockSpec((None, 1, tn), lambda i, j: (i // tpb, 0, gcol + j))
    return pl.pallas_call(
        _outproj_kernel,
        out_shape=jax.ShapeDtypeStruct((m, d), F32),
        grid=(m // tm, d // tn),
        in_specs=[pl.BlockSpec((tm, half), lambda i, j: (i, ca)),
                  pl.BlockSpec((tm, half), lambda i, j: (i, cb)),
                  pl.BlockSpec((half, tn), lambda i, j: (0, j)),
                  pl.BlockSpec((half, tn), lambda i, j: (1, j)),
                  pl.BlockSpec((tm, tn), lambda i, j: (i, j)),
                  gate_spec],
        out_specs=pl.BlockSpec((tm, tn), lambda i, j: (i, j)),
        compiler_params=_cparams(("arbitrary", "arbitrary")),
        name="outproj",
    )(ma, mb, w_bf16, w_bf16, x2d, mod)


def _gla_kernel(q_ref, k_ref, v_ref, ra_ref, g_ref, wa2_ref, ba_ref, ng_ref,
                o_ref, st_ref, stt_ref, *, n_sub, dk, dv):
    c = pl.program_id(1)
    cs = GLA_CHUNK

    @pl.when(c == 0)
    def _():
        stt_ref[...] = jnp.zeros_like(stt_ref)

    row = lax.broadcasted_iota(jnp.int32, (cs, cs), 0)
    col = lax.broadcasted_iota(jnp.int32, (cs, cs), 1)
    causal = row >= col
    tri = jnp.where(causal, 1.0, 0.0).astype(BF16)
    qscale = np.float32(dk ** -0.5)

    def body(i, carry):
        r0 = pl.multiple_of(i * cs, cs)
        rows = pl.ds(r0, cs)
        pre = _dot(ra_ref[rows, :].astype(BF16), wa2_ref[...]) + ba_ref[...]
        la = (jnp.minimum(pre, 0.0) - _softplus_neg_abs(pre)) * np.float32(1.0 / GLA_TAU)
        b = _split_dot(la, tri, left=True)
        bl = b[cs - 1:cs, :]
        k = k_ref[rows, :]
        qe = (q_ref[rows, :] * qscale * jnp.exp(b)).astype(BF16)
        ke = (k * jnp.exp(-b)).astype(BF16)
        kd = (k * jnp.exp(bl - b)).astype(BF16)
        ebl = jnp.exp(bl)
        for h in range(HA):
            ks, vs = slice(h * dk, (h + 1) * dk), slice(h * dv, (h + 1) * dv)
            v = v_ref[rows, vs].astype(BF16)
            s = jnp.where(causal, _dot_nt(qe[:, ks], ke[:, ks]), 0.0)
            stt = stt_ref[h]
            o = _dot(s.astype(BF16), v) + _dot_nt(qe[:, ks], stt.astype(BF16))
            stt_ref[h] = stt * ebl[:, ks] + _dot_tn(v, kd[:, ks])
            og = _rms(o, ng_ref[:, vs]) * _silu(g_ref[rows, vs])
            o_ref[rows, vs] = og.astype(o_ref.dtype)
        return carry

    lax.fori_loop(0, n_sub, body, 0)

    @pl.when(c == pl.num_programs(1) - 1)
    def _():
        for h in range(HA):
            st_ref[h] = stt_ref[h].T


def _gla_prompt(y3, wa2_pad, b_a, norm_g, tc):
    bsz, t, n = y3.shape
    dk, dv = LANES, 2 * LANES
    wk, wv = HA * dk, HA * dv
    kern = functools.partial(_gla_kernel, n_sub=tc // GLA_CHUNK, dk=dk, dv=dv)
    return pl.pallas_call(
        kern,
        out_shape=(jax.ShapeDtypeStruct((bsz, t, wv), BF16),
                   jax.ShapeDtypeStruct((bsz, HA, dk, dv), F32)),
        grid=(bsz, t // tc),
        in_specs=[pl.BlockSpec((None, tc, wk), lambda b, c: (b, c, 0)),
                  pl.BlockSpec((None, tc, wk), lambda b, c: (b, c, 1)),
                  pl.BlockSpec((None, tc, wv), lambda b, c: (b, c, 1)),
                  pl.BlockSpec((None, tc, LANES), lambda b, c: (b, c, (n - MXU_DIM) // LANES)),
                  pl.BlockSpec((None, tc, wv), lambda b, c: (b, c, 2)),
                  pl.BlockSpec((LANES, wk), lambda b, c: (0, 0)),
                  pl.BlockSpec((1, wk), lambda b, c: (0, 0)),
                  pl.BlockSpec((1, wv), lambda b, c: (0, 0))],
        out_specs=(pl.BlockSpec((None, tc, wv), lambda b, c: (b, c, 0)),
                   pl.BlockSpec((None, HA, dk, dv), lambda b, c: (b, 0, 0, 0))),
        scratch_shapes=[pltpu.VMEM((HA, dv, dk), F32)],
        compiler_params=_cparams(("arbitrary", "arbitrary")),
        name="gla_prompt",
    )(y3, y3, y3, y3, y3, wa2_pad, b_a.reshape(1, -1), norm_g.reshape(1, -1))


def _rot(x, cosf, sinf):
    return x * cosf + pltpu.roll(x, x.shape[-1] // 2, 1) * sinf


def _ret_kernel(q_ref, k_ref, v_ref, g_ref, cos_ref, sin_ref, lg_ref, ng_ref,
                o_ref, st_ref, stt_ref, *, n_sub, dk, dv):
    c = pl.program_id(1)
    cs = RET_CHUNK

    @pl.when(c == 0)
    def _():
        stt_ref[...] = jnp.zeros_like(stt_ref)

    row = lax.broadcasted_iota(jnp.int32, (cs, cs), 0)
    col = lax.broadcasted_iota(jnp.int32, (cs, cs), 1)
    rel = (row - col).astype(F32)
    ridx = lax.broadcasted_iota(jnp.int32, (cs, 1), 0).astype(F32)
    lgs = [lg_ref[:, h * dk:h * dk + 1] for h in range(HB)]
    decay = [jnp.where(row >= col, jnp.exp(rel * lg), 0.0) for lg in lgs]
    inter = [jnp.exp((ridx + 1.0) * lg) for lg in lgs]
    kdec = [jnp.exp((np.float32(cs - 1.0) - ridx) * lg) for lg in lgs]
    sdec = [jnp.exp(np.float32(cs) * lg) for lg in lgs]
    qscale = np.float32(dk ** -0.5)

    def body(i, carry):
        r0 = pl.multiple_of(i * cs, cs)
        rows = pl.ds(r0, cs)
        cosf = cos_ref[rows, :]
        sinf = sin_ref[rows, :]
        for h in range(HB):
            ks, vs = slice(h * dk, (h + 1) * dk), slice(h * dv, (h + 1) * dv)
            qr = (_rot(q_ref[rows, ks], cosf, sinf) * qscale).astype(BF16)
            kr = _rot(k_ref[rows, ks], cosf, sinf)
            v = v_ref[rows, vs].astype(BF16)
            s = _dot_nt(qr, kr.astype(BF16)) * decay[h]
            stt = stt_ref[h]
            o = _dot(s.astype(BF16), v) + _dot_nt(qr, stt.astype(BF16)) * inter[h]
            stt_ref[h] = stt * sdec[h] + _dot_tn(v, (kr * kdec[h]).astype(BF16))
            og = _rms(o, ng_ref[:, vs]) * _silu(g_ref[rows, vs])
            o_ref[rows, vs] = og.astype(o_ref.dtype)
        return carry

    lax.fori_loop(0, n_sub, body, 0)

    @pl.when(c == pl.num_programs(1) - 1)
    def _():
        for h in range(HB):
            st_ref[h] = stt_ref[h].T


def _ret_prompt(y3, cosf, sinf, lg_row, norm_g, col0, tc):
    bsz, t, _ = y3.shape
    dk, dv = LANES, 2 * LANES
    wk, wv = HB * dk, HB * dv
    cq, cv = col0 // wk, (col0 + 2 * wk) // wv
    kern = functools.partial(_ret_kernel, n_sub=tc // RET_CHUNK, dk=dk, dv=dv)
    return pl.pallas_call(
        kern,
        out_shape=(jax.ShapeDtypeStruct((bsz, t, wv), BF16),
                   jax.ShapeDtypeStruct((bsz, HB, dk, dv), F32)),
        grid=(bsz, t // tc),
        in_specs=[pl.BlockSpec((None, tc, wk), lambda b, c: (b, c, cq)),
                  pl.BlockSpec((None, tc, wk), lambda b, c: (b, c, cq + 1)),
                  pl.BlockSpec((None, tc, wv), lambda b, c: (b, c, cv)),
                  pl.BlockSpec((None, tc, wv), lambda b, c: (b, c, cv + 1)),
                  pl.BlockSpec((tc, dk), lambda b, c: (c, 0)),
                  pl.BlockSpec((tc, dk), lambda b, c: (c, 0)),
                  pl.BlockSpec((1, wk), lambda b, c: (0, 0)),
                  pl.BlockSpec((1, wv), lambda b, c: (0, 0))],
        out_specs=(pl.BlockSpec((None, tc, wv), lambda b, c: (b, c, 0)),
                   pl.BlockSpec((None, HB, dk, dv), lambda b, c: (b, 0, 0, 0))),
        scratch_shapes=[pltpu.VMEM((HB, dv, dk), F32)],
        compiler_params=_cparams(("arbitrary", "arbitrary")),
        name="ret_prompt",
    )(y3, y3, y3, y3, cosf, sinf, lg_row, norm_g.reshape(1, -1))


def _row_to_col(x_row):
    n = x_row.shape[-1]
    r = lax.broadcasted_iota(jnp.int32, (n, n), 0)
    c = lax.broadcasted_iota(jnp.int32, (n, n), 1)
    return jnp.sum(jnp.where(r == c, jnp.broadcast_to(x_row, (n, n)), 0.0), axis=1, keepdims=True)


def _even_decode_kernel(y_ref, sg_ref, sr_ref, wa2_ref, ba_ref, gng_ref, rng_ref, cos_ref, sin_ref,
                        lg_ref, o_ref, sgo_ref, sro_ref, *, cols):
    dk, dv = LANES, 2 * LANES
    cqa, cka, cva, cga, cqb, ckb, cvb, cgb, cra = cols
    y = y_ref[...]
    pre = _dot(y[:, cra:cra + LANES].astype(BF16), wa2_ref[...]) + ba_ref[...]
    la = (jnp.minimum(pre, 0.0) - _softplus_neg_abs(pre)) * np.float32(1.0 / GLA_TAU)
    alpha = jnp.exp(la)
    cosf = cos_ref[...]
    sinf = sin_ref[...]
    gam = jnp.exp(lg_ref[...])
    outs = []
    for h in range(HA):
        q = y[:, cqa + h * dk:cqa + (h + 1) * dk] * np.float32(dk ** -0.5)
        k = y[:, cka + h * dk:cka + (h + 1) * dk]
        v = y[:, cva + h * dv:cva + (h + 1) * dv]
        s_new = _row_to_col(alpha[:, h * dk:(h + 1) * dk]) * sg_ref[h] + _row_to_col(k) * v
        sgo_ref[h] = s_new
        o = jnp.sum(_row_to_col(q) * s_new, axis=0, keepdims=True)
        g = y[:, cga + h * dv:cga + (h + 1) * dv]
        outs.append(_rms(o, gng_ref[:, h * dv:(h + 1) * dv]) * _silu(g))
    for h in range(HB):
        q = _rot(y[:, cqb + h * dk:cqb + (h + 1) * dk], cosf, sinf) * np.float32(dk ** -0.5)
        k = _rot(y[:, ckb + h * dk:ckb + (h + 1) * dk], cosf, sinf)
        v = y[:, cvb + h * dv:cvb + (h + 1) * dv]
        s_new = _row_to_col(gam[:, h * dk:(h + 1) * dk]) * sr_ref[h] + _row_to_col(k) * v
        sro_ref[h] = s_new
        o = jnp.sum(_row_to_col(q) * s_new, axis=0, keepdims=True)
        g = y[:, cgb + h * dv:cgb + (h + 1) * dv]
        outs.append(_rms(o, rng_ref[:, h * dv:(h + 1) * dv]) * _silu(g))
    o_ref[...] = jnp.concatenate(outs, axis=-1).astype(o_ref.dtype)


def _even_decode(y_s, state_gla, state_ret, wa2_pad, b_a, gla_ng, ret_ng, cos_row, sin_row, lg_row, cols):
    bsz, n = y_s.shape
    dk, dv = LANES, 2 * LANES
    wide = (HA + HB) * dv
    kern = functools.partial(_even_decode_kernel, cols=cols)
    full = lambda a: pl.BlockSpec(a.shape, lambda b: (0,) * a.ndim)
    b_a2, gng, rng_ = b_a.reshape(1, -1), gla_ng.reshape(1, -1), ret_ng.reshape(1, -1)
    mixed, sg, sr = pl.pallas_call(
        kern,
        out_shape=(jax.ShapeDtypeStruct((bsz, 1, wide), BF16),
                   jax.ShapeDtypeStruct(state_gla.shape, F32),
                   jax.ShapeDtypeStruct(state_ret.shape, F32)),
        grid=(bsz,),
        in_specs=[pl.BlockSpec((None, 1, n), lambda b: (b, 0, 0)),
                  pl.BlockSpec((None, HA, dk, dv), lambda b: (b, 0, 0, 0)),
                  pl.BlockSpec((None, HB, dk, dv), lambda b: (b, 0, 0, 0)),
                  full(wa2_pad), full(b_a2), full(gng), full(rng_), full(cos_row), full(sin_row), full(lg_row)],
        out_specs=(pl.BlockSpec((None, 1, wide), lambda b: (b, 0, 0)),
                   pl.BlockSpec((None, HA, dk, dv), lambda b: (b, 0, 0, 0)),
                   pl.BlockSpec((None, HB, dk, dv), lambda b: (b, 0, 0, 0))),
        compiler_params=_cparams(("arbitrary",)),
        name="even_decode",
    )(y_s.reshape(bsz, 1, n), state_gla, state_ret, wa2_pad, b_a2, gng, rng_, cos_row, sin_row, lg_row)
    return mixed.reshape(bsz, wide), sg, sr


def _qkv_kernel(q_ref, k_ref, v_ref, qg_ref, kg_ref, qn_ref, kn_ref, knb_ref, vo_ref, vb_ref):
    qscale = np.float32(q_ref.shape[-1] ** -0.5) * LOG2E
    qn_ref[...] = (_rms(q_ref[...], qg_ref[...]) * qscale).astype(BF16)
    kn = _rms(k_ref[...], kg_ref[...])
    kn_ref[...] = kn
    knb_ref[...] = kn.astype(BF16)
    v = v_ref[...]
    vo_ref[...] = v
    vb_ref[...] = v.astype(BF16)


def _qkv_prompt(y2, q_g, k_g, tm):
    m = y2.shape[0]
    dc = LANES
    blk = lambda off: pl.BlockSpec((tm, dc), lambda i, h: (i, off + h))
    gspec = pl.BlockSpec((1, dc), lambda i, h: (0, 0))
    return pl.pallas_call(
        _qkv_kernel,
        out_shape=(jax.ShapeDtypeStruct((m, HC * dc), BF16),
                   jax.ShapeDtypeStruct((m, HC * dc), F32),
                   jax.ShapeDtypeStruct((m, HC * dc), BF16),
                   jax.ShapeDtypeStruct((m, HC * dc), F32),
                   jax.ShapeDtypeStruct((m, HC * dc), BF16)),
        grid=(m // tm, HC),
        in_specs=[blk(0), blk(HC), blk(2 * HC), gspec, gspec],
        out_specs=(blk(0), blk(0), blk(0), blk(0), blk(0)),
        compiler_params=_cparams(("arbitrary", "arbitrary")),
        name="qkv_prompt",
    )(y2, y2, y2, q_g.reshape(1, dc), k_g.reshape(1, dc))


def _sb_kernel(qi_ref, kb_ref, off_ref, q_ref, k_ref, v_ref, g_ref, o_ref, acc_ref, run_ref, *, sub, hp):
    hg = pl.program_id(1)
    p = pl.program_id(2)
    qi = qi_ref[p]
    kb = kb_ref[p]
    tq = q_ref.shape[0]
    dc = q_ref.shape[1] // hp

    @pl.when(kb == qi)
    def _():
        acc_ref[...] = jnp.zeros_like(acc_ref)
        run_ref[...] = jnp.zeros_like(run_ref)

    def block(masked):
        r = lax.broadcasted_iota(jnp.int32, (sub, sub), 0)
        c = lax.broadcasted_iota(jnp.int32, (sub, sub), 1)
        upper = jnp.where(r >= c, 1.0, 0.0).astype(BF16)
        for hh in range(hp):
            off = off_ref[hg * hp + hh] * LOG2E
            hs = slice(hh * dc, (hh + 1) * dc)
            for s_i in reversed(range(tq // sub)):
                r0 = s_i * sub if masked else 0
                ks = k_ref[s_i * sub:(s_i + 1) * sub, hs]
                vs = v_ref[s_i * sub:(s_i + 1) * sub, hs]
                z = _dot_nt(q_ref[r0:, hs], ks) + off
                rest = _neg_log2_rest(z)
                if masked:
                    mask = (lax.broadcasted_iota(jnp.int32, z.shape, 1)
                            < lax.broadcasted_iota(jnp.int32, z.shape, 0))
                    rest = jnp.where(mask, rest, 0.0)
                incl = _split_dot(rest, upper)
                run = run_ref[hh, r0:, :]
                w = jnp.exp2(z - incl - run)
                if masked:
                    w = jnp.where(mask, w, 0.0)
                acc_ref[r0:, hs] += _dot(w.astype(BF16), vs)
                run_ref[hh, r0:, :] = run + incl[:, 0:1]

    pl.when(kb == qi)(functools.partial(block, True))
    pl.when(kb < qi)(functools.partial(block, False))

    @pl.when(kb == 0)
    def _():
        o_ref[...] = (acc_ref[...] * _silu(g_ref[...])).astype(o_ref.dtype)


def _sb_prompt(qn, kn, vb, y3, sb_offset, cg, tq, sub, hp):
    bsz, t, _ = qn.shape
    dc = LANES
    wide = hp * dc
    nq = t // tq
    pairs = [(qi, kb) for qi in range(nq) for kb in range(qi, -1, -1)]
    qi_tab = jnp.asarray([p[0] for p in pairs], jnp.int32)
    kb_tab = jnp.asarray([p[1] for p in pairs], jnp.int32)
    kern = functools.partial(_sb_kernel, sub=sub, hp=hp)
    q_map = lambda b, h, p, qi, kb, off: (b, qi[p], h)
    kv_map = lambda b, h, p, qi, kb, off: (b, kb[p], h)
    return pl.pallas_call(
        kern,
        out_shape=jax.ShapeDtypeStruct((bsz, t, HC * dc), BF16),
        grid_spec=pltpu.PrefetchScalarGridSpec(
            num_scalar_prefetch=3,
            grid=(bsz, HC // hp, len(pairs)),
            in_specs=[pl.BlockSpec((None, tq, wide), q_map),
                      pl.BlockSpec((None, tq, wide), kv_map),
                      pl.BlockSpec((None, tq, wide), kv_map),
                      pl.BlockSpec((None, tq, wide), lambda b, h, p, qi, kb, off: (b, qi[p], cg // wide + h))],
            out_specs=pl.BlockSpec((None, tq, wide), q_map),
            scratch_shapes=[pltpu.VMEM((tq, wide), F32), pltpu.VMEM((hp, tq, 1), F32)]),
        compiler_params=_cparams(("arbitrary",) * 3),
        name="sb_prompt",
    )(qi_tab, kb_tab, sb_offset, qn, kn, vb, y3)


def _cmlp_kernel(u_ref, v_ref, g_ref, vg_ref, ws_ref, bs_ref, o_ref, *, n_chunk):
    cs = D_CHUNK
    u = _gelu(u_ref[...])
    vn = _rms(_gelu(v_ref[...]), vg_ref[...]).astype(BF16)
    gate = _silu(g_ref[...])
    bs = bs_ref[...]
    r = lax.broadcasted_iota(jnp.int32, (cs, cs), 0)
    c = lax.broadcasted_iota(jnp.int32, (cs, cs), 1)
    for g in range(DG):
        w = jnp.where(r >= c, ws_ref[g], 0.0).astype(BF16)
        lo, hi = g * LANES, (g + 1) * LANES
        for ch in range(n_chunk):
            r0, r1 = ch * cs, (ch + 1) * cs
            mixed = _dot(w, vn[r0:r1, lo:hi]) + bs[:, lo:hi]
            o_ref[r0:r1, lo:hi] = (u[r0:r1, lo:hi] * mixed * gate[r0:r1, lo:hi]).astype(o_ref.dtype)


def _cmlp_prompt(y3, v_norm_g, w_s, bs_full, cols, tm):
    bsz, t, _ = y3.shape
    br = DG * LANES
    cu, cv, cg = cols
    kern = functools.partial(_cmlp_kernel, n_chunk=tm // D_CHUNK)
    return pl.pallas_call(
        kern,
        out_shape=jax.ShapeDtypeStruct((bsz, t, br), BF16),
        grid=(bsz, t // tm),
        in_specs=[pl.BlockSpec((None, tm, br), lambda b, i: (b, i, cu)),
                  pl.BlockSpec((None, tm, br), lambda b, i: (b, i, cv)),
                  pl.BlockSpec((None, tm, br), lambda b, i: (b, i, cg)),
                  pl.BlockSpec((1, br), lambda b, i: (0, 0)),
                  pl.BlockSpec((DG, D_CHUNK, D_CHUNK), lambda b, i: (0, 0, 0)),
                  pl.BlockSpec((D_CHUNK, br), lambda b, i: (0, 0))],
        out_specs=pl.BlockSpec((None, tm, br), lambda b, i: (b, i, 0)),
        compiler_params=_cparams(("arbitrary", "arbitrary")),
        name="cmlp_prompt",
    )(y3, y3, y3, v_norm_g.reshape(1, br), w_s, bs_full)


def _odd_decode_kernel(y_ref, qg_ref, kg_ref, vg_ref, w0_ref, b0_ref, qn_ref, kn_ref, vo_ref, cv_ref, od_ref):
    br = HC * LANES
    y = y_ref[...]
    for h in range(HC):
        lo, hi = h * LANES, (h + 1) * LANES
        qn_ref[:, lo:hi] = _rms(y[:, lo:hi], qg_ref[...])
        kn_ref[:, lo:hi] = _rms(y[:, br + lo:br + hi], kg_ref[...])
    vo_ref[...] = y[:, 2 * br:3 * br]
    u = _gelu(y[:, 4 * br:5 * br])
    vn = _rms(_gelu(y[:, 5 * br:6 * br]), vg_ref[...])
    cv_ref[...] = vn
    od = u * (w0_ref[...] * vn + b0_ref[...]) * _silu(y[:, 6 * br:7 * br])
    od_ref[...] = od.astype(od_ref.dtype)


def _odd_decode(y_s, q_g, k_g, v_norm_g, w0_row, b0_row):
    s = y_s.shape[0]
    br = HC * LANES
    args = (y_s, q_g.reshape(1, -1), k_g.reshape(1, -1), v_norm_g.reshape(1, -1), w0_row, b0_row)
    full = lambda a: pl.BlockSpec(a.shape, lambda i: (0,) * a.ndim)
    osd = lambda dt: jax.ShapeDtypeStruct((s, br), dt)
    ospec = pl.BlockSpec((s, br), lambda i: (0, 0))
    return pl.pallas_call(
        _odd_decode_kernel,
        out_shape=(osd(F32), osd(F32), osd(F32), osd(F32), osd(BF16)),
        grid=(1,),
        in_specs=[full(a) for a in args],
        out_specs=(ospec,) * 5,
        compiler_params=_cparams(("arbitrary",)),
        name="odd_decode",
    )(*args)


def _strided_suffix_sums(x, stride, axis):
    n = x.shape[axis]
    idx = lax.broadcasted_iota(jnp.int32, x.shape, axis)
    inc, tot = x, x
    k = stride
    while k < n:
        inc = inc + jnp.where(idx < n - k, pltpu.roll(inc, n - k, axis), 0.0)
        tot = tot + pltpu.roll(tot, n - k, axis)
        k *= 2
    return inc, tot


def _paged_kernel(pt_ref, q_ref, g_ref, off_ref, *refs, n_pp):
    k_refs, v_refs = refs[:n_pp], refs[n_pp:2 * n_pp]
    o_ref, acc_ref, run_ref = refs[2 * n_pp:]
    j = pl.program_id(1)
    page, hc, dc = k_refs[0].shape
    rows = page * hc
    lane = lax.broadcasted_iota(jnp.int32, (hc, rows), 1)
    own = (lane & (hc - 1)) == lax.broadcasted_iota(jnp.int32, (hc, rows), 0)

    @pl.when(j == 0)
    def _():
        acc_ref[...] = jnp.zeros_like(acc_ref)
        run_ref[...] = jnp.zeros_like(run_ref)

    q = (q_ref[...] * (np.float32(dc ** -0.5) * LOG2E)).astype(BF16)
    off = off_ref[...] * LOG2E
    prow = lax.broadcasted_iota(jnp.int32, (n_pp, rows), 0)
    z = jnp.zeros((n_pp, rows), F32)
    for p in range(n_pp):
        kf = k_refs[p][...].reshape(rows, dc).astype(BF16)
        s = _dot_nt(q, kf)
        z = jnp.where(prow == p, jnp.sum(jnp.where(own, s, 0.0), axis=0, keepdims=True), z)
    z = z + off
    incl, tot = _strided_suffix_sums(_neg_log2_rest(z), hc, 1)
    pages_incl, _ = _strided_suffix_sums(tot, 1, 0)
    run = run_ref[...]
    w = jnp.exp2(z - incl - (pages_incl - tot) - run)
    run_ref[...] = run + pages_incl[0:1, :]
    acc = acc_ref[...]
    for p in range(n_pp):
        wm = jnp.where(own, jnp.broadcast_to(w[p:p + 1, :], (hc, rows)), 0.0).astype(BF16)
        acc = acc + _dot(wm, v_refs[p][...].reshape(rows, dc).astype(BF16))
    acc_ref[...] = acc

    @pl.when(j == pl.num_programs(1) - 1)
    def _():
        o_ref[...] = (acc_ref[...] * _silu(g_ref[...])).astype(o_ref.dtype)


def _paged_attention(qn_s, gate_s, cache_k, cache_v, layer, page_table, sb_offset):
    s, hc, dc = qn_s.shape
    n_pages = page_table.shape[1]
    page = cache_k.shape[2]
    n_pp = PAGES_PER_STEP
    assert hc & (hc - 1) == 0 and n_pages % n_pp == 0
    rows = page * hc

    def kv_spec(p):
        return pl.BlockSpec((None, None, page, hc, dc),
                            lambda b, j, pt: (layer, pt[b, n_pages - (j + 1) * n_pp + p], 0, 0, 0))

    head_blk = pl.BlockSpec((None, hc, dc), lambda b, j, pt: (b, 0, 0))
    return pl.pallas_call(
        functools.partial(_paged_kernel, n_pp=n_pp),
        out_shape=jax.ShapeDtypeStruct((s, hc, dc), BF16),
        grid_spec=pltpu.PrefetchScalarGridSpec(
            num_scalar_prefetch=1,
            grid=(s, n_pages // n_pp),
            in_specs=[head_blk, head_blk, pl.BlockSpec((1, rows), lambda b, j, pt: (0, 0))]
            + [kv_spec(p) for p in range(n_pp)] * 2,
            out_specs=head_blk,
            scratch_shapes=[pltpu.VMEM((hc, dc), F32), pltpu.VMEM((1, rows), F32)]),
        compiler_params=_cparams(("arbitrary", "arbitrary")),
        name="paged_attention",
    )(page_table, qn_s, gate_s, jnp.tile(sb_offset, page).reshape(1, rows),
      *([cache_k] * n_pp), *([cache_v] * n_pp))


def _rope_tables(pos, half):
    inv = ROPE_BASE ** (-jnp.arange(half, dtype=F32) / half)
    ang = pos.astype(F32)[:, None] * inv[None, :]
    cos, sin = jnp.cos(ang), jnp.sin(ang)
    return jnp.concatenate([cos, cos], axis=-1), jnp.concatenate([-sin, sin], axis=-1)


def kernel(x_prompt, x_sample, state_gla, state_ret, cache_k, cache_v, page_table, c_prompt, c_sample,
           e_norm_g, e_w_ada, e_b_ada, e_w_in, e_w_a2, e_b_a, e_gla_norm_g, e_ret_norm_g, e_w_out,
           o_norm_g, o_w_ada, o_b_ada, o_w_in, o_q_norm_g, o_k_norm_g, o_sb_offset, o_v_norm_g,
           o_w_s, o_b_s, o_w_out):
    bsz, t, d = x_prompt.shape
    s = x_sample.shape[0]
    br = d // 2
    n_past = page_table.shape[1] * cache_k.shape[2]
    dk = LANES

    c_rows = jnp.concatenate([c_sample, c_prompt, jnp.zeros((16 - s - bsz, d), F32)], axis=0)

    n_qkv = 2 * HA * dk + HA * 2 * dk
    n_main = e_w_in.shape[2] - GLA_RANK
    col_ret = n_qkv + HA * 2 * dk
    cols_dec = (0, HA * dk, 2 * HA * dk, n_qkv,
                col_ret, col_ret + HB * dk, col_ret + 2 * HB * dk, col_ret + 2 * HB * dk + HB * 2 * dk,
                n_main)

    xp = x_prompt.reshape(bsz * t, d)
    xs = x_sample.reshape(s, d)
    pos_p = jnp.arange(t)
    pos_s = n_past + jnp.arange(x_sample.shape[1])
    cos_p, sin_p = _rope_tables(pos_p, dk // 2)
    cos_s, sin_s = _rope_tables(pos_s, dk // 2)
    log_gamma = jnp.log1p(-jnp.exp2(-5.0 - jnp.arange(HB, dtype=F32)))
    lg_row = jnp.repeat(log_gamma, dk).reshape(1, HB * dk)

    i = 0
    mod = _ada(c_rows, e_w_ada[i], e_b_ada[i])
    mod_s, mod_p = mod[:s], mod[s:s + bsz].reshape(bsz, 1, 3 * d)
    w_in = e_w_in[i]
    w_in_b = jnp.concatenate([w_in[:, :n_qkv], w_in[:, n_qkv + GLA_RANK:], w_in[:, n_qkv:n_qkv + GLA_RANK],
                              jnp.zeros((d, MXU_DIM - GLA_RANK), F32)], axis=1).astype(BF16)
    wa2_pad = jnp.concatenate([e_w_a2[i], jnp.zeros((LANES - GLA_RANK, HA * dk), F32)], axis=0).astype(BF16)
    w_out_b = e_w_out[i].astype(BF16)

    y = _proj(xp, mod_p, e_norm_g[i], w_in_b, t, 1024, 1280)
    y3 = y.reshape(bsz, t, -1)
    oa, gla_p = _gla_prompt(y3, wa2_pad, e_b_a[i], e_gla_norm_g[i], 512)
    ob, ret_p = _ret_prompt(y3, cos_p, sin_p, lg_row, e_ret_norm_g[i], col_ret, 512)
    xp = _outproj(oa.reshape(bsz * t, br), 0, ob.reshape(bsz * t, br), 0, w_out_b, xp, mod_p, t, 512, d)

    y_s = _proj(xs, mod_s, e_norm_g[i], w_in_b, None, s, 1280)
    mixed_s, gla_s, ret_s = _even_decode(y_s, state_gla[i], state_ret[i], wa2_pad, e_b_a[i],
                                         e_gla_norm_g[i], e_ret_norm_g[i], cos_s, sin_s, lg_row, cols_dec)
    xs = _outproj(mixed_s, 0, mixed_s, 1, w_out_b, xs, mod_s, None, s, 1024)

    mod = _ada(c_rows, o_w_ada[i], o_b_ada[i])
    mod_s, mod_p = mod[:s], mod[s:s + bsz].reshape(bsz, 1, 3 * d)
    w_in_b = o_w_in[i].astype(BF16)
    w_out_b = o_w_out[i].astype(BF16)
    bs_full = jnp.repeat(jnp.transpose(o_b_s[i]), LANES, axis=1)

    y = _proj(xp, mod_p, o_norm_g[i], w_in_b, t, 1024, 1024)
    y3 = y.reshape(bsz, t, -1)
    qn, kn, knb, vo, vb = _qkv_prompt(y, o_q_norm_g[i], o_k_norm_g[i], 1024)
    shp = (bsz, t, br)
    oc = _sb_prompt(qn.reshape(shp), knb.reshape(shp), vb.reshape(shp), y3, o_sb_offset[i], 3 * br, 512, 256, 2)
    od = _cmlp_prompt(y3, o_v_norm_g[i], o_w_s[i], bs_full, (4, 5, 6), 256)
    xp = _outproj(oc.reshape(bsz * t, br), 0, od.reshape(bsz * t, br), 0, w_out_b, xp, mod_p, t, 512, d)

    y_s = _proj(xs, mod_s, o_norm_g[i], w_in_b, None, s, 1024)
    w0_row = jnp.repeat(o_w_s[i][:, 0, 0], LANES).reshape(1, br)
    b0_row = jnp.repeat(o_b_s[i][:, 0], LANES).reshape(1, br)
    qn_s, kn_s, vo_s, cv_s, od_s = _odd_decode(y_s, o_q_norm_g[i], o_k_norm_g[i], o_v_norm_g[i], w0_row, b0_row)
    dc = LANES
    oc_s = _paged_attention(qn_s.reshape(s, HC, dc), y_s[:, 3 * br:4 * br].reshape(s, HC, dc),
                            cache_k, cache_v, i, page_table, o_sb_offset[i]).reshape(s, br)
    xs = _outproj(oc_s, 0, od_s, 0, w_out_b, xs, mod_s, None, s, 1024)

    return (xp.reshape(bsz, t, d), xs.reshape(s, 1, d),
            gla_p[None], gla_s[None], ret_p[None], ret_s[None],
            kn.reshape(1, bsz, t, HC, dc), vo.reshape(1, bsz, t, HC, dc),
            kn_s.reshape(1, s, 1, HC, dc), vo_s.reshape(1, s, 1, HC, dc),
            cv_s.reshape(1, s, 1, br))
```

```python
import functools

import jax
import jax.numpy as jnp
import numpy as np
from jax import lax
from jax.experimental import pallas as pl
from jax.experimental.pallas import tpu as pltpu

F32 = jnp.float32
BF16 = jnp.bfloat16

HA = 4
HB = 4
HC = 8
DG = 8
GLA_RANK = 16
GLA_TAU = 16.0
GLA_CHUNK = 64
RET_CHUNK = 128
ROPE_BASE = 10000.0
D_CHUNK = 128
RMS_EPS = 1e-6

LANES = 128
MXU_DIM = 256
VMEM_LIMIT = 56 * 1024 * 1024
LOG2E = np.float32(np.log2(np.e))
PAGES_PER_STEP = 8


def _cparams(sem):
    return pltpu.CompilerParams(dimension_semantics=sem, vmem_limit_bytes=VMEM_LIMIT)


def _dot(a, b):
    return jnp.dot(a, b, preferred_element_type=F32)


def _dot_nt(a, b):
    return lax.dot_general(a, b, (((1,), (1,)), ((), ())), preferred_element_type=F32)


def _dot_tn(a, b):
    return lax.dot_general(a, b, (((0,), (0,)), ((), ())), preferred_element_type=F32)


def _sigmoid(x):
    return 1.0 / (1.0 + jnp.exp(-x))


def _silu(x):
    return x * _sigmoid(x)


def _gelu(x):
    c = np.float32(np.sqrt(2.0 / np.pi))
    return 0.5 * x * (1.0 + jnp.tanh(c * (x + 0.044715 * (x * x * x))))


def _softplus_neg_abs(x):
    return jnp.log(1.0 + jnp.exp(-jnp.abs(x)))


def _neg_log2_rest(z2):
    return jnp.maximum(z2, 0.0) + jnp.log(1.0 + jnp.exp2(-jnp.abs(z2))) * LOG2E


def _split_dot(x, ones_bf16, left=False):
    hi = x.astype(BF16)
    lo = (x - hi.astype(F32)).astype(BF16)
    if left:
        return _dot(ones_bf16, hi) + _dot(ones_bf16, lo)
    return _dot(hi, ones_bf16) + _dot(lo, ones_bf16)


def _rms(x, g):
    ms = jnp.mean(x * x, axis=-1, keepdims=True)
    return x * lax.rsqrt(ms + RMS_EPS) * g


def _ada_kernel(c_ref, w_ref, b_ref, o_ref):
    s = _silu(c_ref[...]).astype(BF16)
    o_ref[...] = _dot(s, w_ref[...].astype(BF16)) + b_ref[...]


def _ada(c_rows, w, b):
    r, d = c_rows.shape
    n = w.shape[1]
    tn = 512
    return pl.pallas_call(
        _ada_kernel,
        out_shape=jax.ShapeDtypeStruct((r, n), F32),
        grid=(n // tn,),
        in_specs=[pl.BlockSpec((r, d), lambda j: (0, 0)),
                  pl.BlockSpec((d, tn), lambda j: (0, j)),
                  pl.BlockSpec((1, tn), lambda j: (0, j))],
        out_specs=pl.BlockSpec((r, tn), lambda j: (0, j)),
        compiler_params=_cparams(("arbitrary",)),
        name="ada",
    )(c_rows, w, b.reshape(1, n))


def _proj_kernel(x_ref, shift_ref, scale_ref, g_ref, w_ref, o_ref, h_ref):
    @pl.when(pl.program_id(1) == 0)
    def _():
        h = _rms(x_ref[...], g_ref[...]) * (1.0 + scale_ref[...]) + shift_ref[...]
        h_ref[...] = h.astype(BF16)

    o_ref[...] = _dot(h_ref[...], w_ref[...])


def _proj(x2d, mod, norm_g, w_bf16, rows_per_batch, tm, tn):
    m, d = x2d.shape
    n = w_bf16.shape[1]
    if rows_per_batch is None:
        shift_spec = pl.BlockSpec((tm, d), lambda i, j: (i, 0))
        scale_spec = pl.BlockSpec((tm, d), lambda i, j: (i, 1))
    else:
        tpb = rows_per_batch // tm
        shift_spec = pl.BlockSpec((None, 1, d), lambda i, j: (i // tpb, 0, 0))
        scale_spec = pl.BlockSpec((None, 1, d), lambda i, j: (i // tpb, 0, 1))
    return pl.pallas_call(
        _proj_kernel,
        out_shape=jax.ShapeDtypeStruct((m, n), F32),
        grid=(m // tm, n // tn),
        in_specs=[pl.BlockSpec((tm, d), lambda i, j: (i, 0)),
                  shift_spec, scale_spec,
                  pl.BlockSpec((1, d), lambda i, j: (0, 0)),
                  pl.BlockSpec((d, tn), lambda i, j: (0, j))],
        out_specs=pl.BlockSpec((tm, tn), lambda i, j: (i, j)),
        scratch_shapes=[pltpu.VMEM((tm, d), BF16)],
        compiler_params=_cparams(("arbitrary", "arbitrary")),
        name="proj",
    )(x2d, mod, mod, norm_g.reshape(1, d), w_bf16)


def _outproj_kernel(ma_ref, mb_ref, wa_ref, wb_ref, x_ref, gate_ref, o_ref):
    mixed = _dot(ma_ref[...], wa_ref[...]) + _dot(mb_ref[...], wb_ref[...])
    o_ref[...] = x_ref[...] + gate_ref[...] * mixed


def _outproj(ma, ca, mb, cb, w_bf16, x2d, mod, rows_per_batch, tm, tn):
    m, d = x2d.shape
    half = w_bf16.shape[0] // 2
    gcol = 2 * (d // tn)
    if rows_per_batch is None:
        gate_spec = pl.BlockSpec((tm, tn), lambda i, j: (i, gcol + j))
    else:
        tpb = rows_per_batch // tm
        gate_spec = pl.BlockSpec((None, 1, tn), lambda i, j: (i // tpb, 0, gcol + j))
    return pl.pallas_call(
        _outproj_kernel,
        out_shape=jax.ShapeDtypeStruct((m, d), F32),
        grid=(m // tm, d // tn),
        in_specs=[pl.BlockSpec((tm, half), lambda i, j: (i, ca)),
                  pl.BlockSpec((tm, half), lambda i, j: (i, cb)),
                  pl.BlockSpec((half, tn), lambda i, j: (0, j)),
                  pl.BlockSpec((half, tn), lambda i, j: (1, j)),
                  pl.BlockSpec((tm, tn), lambda i, j: (i, j)),
                  gate_spec],
        out_specs=pl.BlockSpec((tm, tn), lambda i, j: (i, j)),
        compiler_params=_cparams(("arbitrary", "arbitrary")),
        name="outproj",
    )(ma, mb, w_bf16, w_bf16, x2d, mod)


def _gla_kernel(q_ref, k_ref, v_ref, ra_ref, g_ref, wa2_ref, ba_ref, ng_ref,
                o_ref, st_ref, stt_ref, *, n_sub, dk, dv):
    c = pl.program_id(1)
    cs = GLA_CHUNK

    @pl.when(c == 0)
    def _():
        stt_ref[...] = jnp.zeros_like(stt_ref)

    row = lax.broadcasted_iota(jnp.int32, (cs, cs), 0)
    col = lax.broadcasted_iota(jnp.int32, (cs, cs), 1)
    causal = row >= col
    tri = jnp.where(causal, 1.0, 0.0).astype(BF16)
    qscale = np.float32(dk ** -0.5)

    def body(i, carry):
        r0 = pl.multiple_of(i * cs, cs)
        rows = pl.ds(r0, cs)
        pre = _dot(ra_ref[rows, :].astype(BF16), wa2_ref[...]) + ba_ref[...]
        la = (jnp.minimum(pre, 0.0) - _softplus_neg_abs(pre)) * np.float32(1.0 / GLA_TAU)
        b = _split_dot(la, tri, left=True)
        bl = b[cs - 1:cs, :]
        k = k_ref[rows, :]
        qe = (q_ref[rows, :] * qscale * jnp.exp(b)).astype(BF16)
        ke = (k * jnp.exp(-b)).astype(BF16)
        kd = (k * jnp.exp(bl - b)).astype(BF16)
        ebl = jnp.exp(bl)
        for h in range(HA):
            ks, vs = slice(h * dk, (h + 1) * dk), slice(h * dv, (h + 1) * dv)
            v = v_ref[rows, vs].astype(BF16)
            s = jnp.where(causal, _dot_nt(qe[:, ks], ke[:, ks]), 0.0)
            stt = stt_ref[h]
            o = _dot(s.astype(BF16), v) + _dot_nt(qe[:, ks], stt.astype(BF16))
            stt_ref[h] = stt * ebl[:, ks] + _dot_tn(v, kd[:, ks])
            og = _rms(o, ng_ref[:, vs]) * _silu(g_ref[rows, vs])
            o_ref[rows, vs] = og.astype(o_ref.dtype)
        return carry

    lax.fori_loop(0, n_sub, body, 0)

    @pl.when(c == pl.num_programs(1) - 1)
    def _():
        for h in range(HA):
            st_ref[h] = stt_ref[h].T


def _gla_prompt(y3, wa2_pad, b_a, norm_g, tc):
    bsz, t, n = y3.shape
    dk, dv = LANES, 2 * LANES
    wk, wv = HA * dk, HA * dv
    kern = functools.partial(_gla_kernel, n_sub=tc // GLA_CHUNK, dk=dk, dv=dv)
    return pl.pallas_call(
        kern,
        out_shape=(jax.ShapeDtypeStruct((bsz, t, wv), BF16),
                   jax.ShapeDtypeStruct((bsz, HA, dk, dv), F32)),
        grid=(bsz, t // tc),
        in_specs=[pl.BlockSpec((None, tc, wk), lambda b, c: (b, c, 0)),
                  pl.BlockSpec((None, tc, wk), lambda b, c: (b, c, 1)),
                  pl.BlockSpec((None, tc, wv), lambda b, c: (b, c, 1)),
                  pl.BlockSpec((None, tc, LANES), lambda b, c: (b, c, (n - MXU_DIM) // LANES)),
                  pl.BlockSpec((None, tc, wv), lambda b, c: (b, c, 2)),
                  pl.BlockSpec((LANES, wk), lambda b, c: (0, 0)),
                  pl.BlockSpec((1, wk), lambda b, c: (0, 0)),
                  pl.BlockSpec((1, wv), lambda b, c: (0, 0))],
        out_specs=(pl.BlockSpec((None, tc, wv), lambda b, c: (b, c, 0)),
                   pl.BlockSpec((None, HA, dk, dv), lambda b, c: (b, 0, 0, 0))),
        scratch_shapes=[pltpu.VMEM((HA, dv, dk), F32)],
        compiler_params=_cparams(("arbitrary", "arbitrary")),
        name="gla_prompt",
    )(y3, y3, y3, y3, y3, wa2_pad, b_a.reshape(1, -1), norm_g.reshape(1, -1))


def _rot(x, cosf, sinf):
    return x * cosf + pltpu.roll(x, x.shape[-1] // 2, 1) * sinf


def _ret_kernel(q_ref, k_ref, v_ref, g_ref, cos_ref, sin_ref, lg_ref, ng_ref,
                o_ref, st_ref, stt_ref, *, n_sub, dk, dv):
    c = pl.program_id(1)
    cs = RET_CHUNK

    @pl.when(c == 0)
    def _():
        stt_ref[...] = jnp.zeros_like(stt_ref)

    row = lax.broadcasted_iota(jnp.int32, (cs, cs), 0)
    col = lax.broadcasted_iota(jnp.int32, (cs, cs), 1)
    rel = (row - col).astype(F32)
    ridx = lax.broadcasted_iota(jnp.int32, (cs, 1), 0).astype(F32)
    lgs = [lg_ref[:, h * dk:h * dk + 1] for h in range(HB)]
    decay = [jnp.where(row >= col, jnp.exp(rel * lg), 0.0) for lg in lgs]
    inter = [jnp.exp((ridx + 1.0) * lg) for lg in lgs]
    kdec = [jnp.exp((np.float32(cs - 1.0) - ridx) * lg) for lg in lgs]
    sdec = [jnp.exp(np.float32(cs) * lg) for lg in lgs]
    qscale = np.float32(dk ** -0.5)

    def body(i, carry):
        r0 = pl.multiple_of(i * cs, cs)
        rows = pl.ds(r0, cs)
        cosf = cos_ref[rows, :]
        sinf = sin_ref[rows, :]
        for h in range(HB):
            ks, vs = slice(h * dk, (h + 1) * dk), slice(h * dv, (h + 1) * dv)
            qr = (_rot(q_ref[rows, ks], cosf, sinf) * qscale).astype(BF16)
            kr = _rot(k_ref[rows, ks], cosf, sinf)
            v = v_ref[rows, vs].astype(BF16)
            s = _dot_nt(qr, kr.astype(BF16)) * decay[h]
            stt = stt_ref[h]
            o = _dot(s.astype(BF16), v) + _dot_nt(qr, stt.astype(BF16)) * inter[h]
            stt_ref[h] = stt * sdec[h] + _dot_tn(v, (kr * kdec[h]).astype(BF16))
            og = _rms(o, ng_ref[:, vs]) * _silu(g_ref[rows, vs])
            o_ref[rows, vs] = og.astype(o_ref.dtype)
        return carry

    lax.fori_loop(0, n_sub, body, 0)

    @pl.when(c == pl.num_programs(1) - 1)
    def _():
        for h in range(HB):
            st_ref[h] = stt_ref[h].T


def _ret_prompt(y3, cosf, sinf, lg_row, norm_g, col0, tc):
    bsz, t, _ = y3.shape
    dk, dv = LANES, 2 * LANES
    wk, wv = HB * dk, HB * dv
    cq, cv = col0 // wk, (col0 + 2 * wk) // wv
    kern = functools.partial(_ret_kernel, n_sub=tc // RET_CHUNK, dk=dk, dv=dv)
    return pl.pallas_call(
        kern,
        out_shape=(jax.ShapeDtypeStruct((bsz, t, wv), BF16),
                   jax.ShapeDtypeStruct((bsz, HB, dk, dv), F32)),
        grid=(bsz, t // tc),
        in_specs=[pl.BlockSpec((None, tc, wk), lambda b, c: (b, c, cq)),
                  pl.BlockSpec((None, tc, wk), lambda b, c: (b, c, cq + 1)),
                  pl.BlockSpec((None, tc, wv), lambda b, c: (b, c, cv)),
                  pl.BlockSpec((None, tc, wv), lambda b, c: (b, c, cv + 1)),
                  pl.BlockSpec((tc, dk), lambda b, c: (c, 0)),
                  pl.BlockSpec((tc, dk), lambda b, c: (c, 0)),
                  pl.BlockSpec((1, wk), lambda b, c: (0, 0)),
                  pl.BlockSpec((1, wv), lambda b, c: (0, 0))],
        out_specs=(pl.BlockSpec((None, tc, wv), lambda b, c: (b, c, 0)),
                   pl.BlockSpec((None, HB, dk, dv), lambda b, c: (b, 0, 0, 0))),
        scratch_shapes=[pltpu.VMEM((HB, dv, dk), F32)],
        compiler_params=_cparams(("arbitrary", "arbitrary")),
        name="ret_prompt",
    )(y3, y3, y3, y3, cosf, sinf, lg_row, norm_g.reshape(1, -1))


def _row_to_col(x_row):
    n = x_row.shape[-1]
    r = lax.broadcasted_iota(jnp.int32, (n, n), 0)
    c = lax.broadcasted_iota(jnp.int32, (n, n), 1)
    return jnp.sum(jnp.where(r == c, jnp.broadcast_to(x_row, (n, n)), 0.0), axis=1, keepdims=True)


def _even_decode_kernel(y_ref, sg_ref, sr_ref, wa2_ref, ba_ref, gng_ref, rng_ref, cos_ref, sin_ref,
                        lg_ref, o_ref, sgo_ref, sro_ref, *, cols):
    dk, dv = LANES, 2 * LANES
    cqa, cka, cva, cga, cqb, ckb, cvb, cgb, cra = cols
    y = y_ref[...]
    pre = _dot(y[:, cra:cra + LANES].astype(BF16), wa2_ref[...]) + ba_ref[...]
    la = (jnp.minimum(pre, 0.0) - _softplus_neg_abs(pre)) * np.float32(1.0 / GLA_TAU)
    alpha = jnp.exp(la)
    cosf = cos_ref[...]
    sinf = sin_ref[...]
    gam = jnp.exp(lg_ref[...])
    outs = []
    for h in range(HA):
        q = y[:, cqa + h * dk:cqa + (h + 1) * dk] * np.float32(dk ** -0.5)
        k = y[:, cka + h * dk:cka + (h + 1) * dk]
        v = y[:, cva + h * dv:cva + (h + 1) * dv]
        s_new = _row_to_col(alpha[:, h * dk:(h + 1) * dk]) * sg_ref[h] + _row_to_col(k) * v
        sgo_ref[h] = s_new
        o = jnp.sum(_row_to_col(q) * s_new, axis=0, keepdims=True)
        g = y[:, cga + h * dv:cga + (h + 1) * dv]
        outs.append(_rms(o, gng_ref[:, h * dv:(h + 1) * dv]) * _silu(g))
    for h in range(HB):
        q = _rot(y[:, cqb + h * dk:cqb + (h + 1) * dk], cosf, sinf) * np.float32(dk ** -0.5)
        k = _rot(y[:, ckb + h * dk:ckb + (h + 1) * dk], cosf, sinf)
        v = y[:, cvb + h * dv:cvb + (h + 1) * dv]
        s_new = _row_to_col(gam[:, h * dk:(h + 1) * dk]) * sr_ref[h] + _row_to_col(k) * v
        sro_ref[h] = s_new
        o = jnp.sum(_row_to_col(q) * s_new, axis=0, keepdims=True)
        g = y[:, cgb + h * dv:cgb + (h + 1) * dv]
        outs.append(_rms(o, rng_ref[:, h * dv:(h + 1) * dv]) * _silu(g))
    o_ref[...] = jnp.concatenate(outs, axis=-1).astype(o_ref.dtype)


def _even_decode(y_s, state_gla, state_ret, wa2_pad, b_a, gla_ng, ret_ng, cos_row, sin_row, lg_row, cols):
    bsz, n = y_s.shape
    dk, dv = LANES, 2 * LANES
    wide = (HA + HB) * dv
    kern = functools.partial(_even_decode_kernel, cols=cols)
    full = lambda a: pl.BlockSpec(a.shape, lambda b: (0,) * a.ndim)
    b_a2, gng, rng_ = b_a.reshape(1, -1), gla_ng.reshape(1, -1), ret_ng.reshape(1, -1)
    mixed, sg, sr = pl.pallas_call(
        kern,
        out_shape=(jax.ShapeDtypeStruct((bsz, 1, wide), BF16),
                   jax.ShapeDtypeStruct(state_gla.shape, F32),
                   jax.ShapeDtypeStruct(state_ret.shape, F32)),
        grid=(bsz,),
        in_specs=[pl.BlockSpec((None, 1, n), lambda b: (b, 0, 0)),
                  pl.BlockSpec((None, HA, dk, dv), lambda b: (b, 0, 0, 0)),
                  pl.BlockSpec((None, HB, dk, dv), lambda b: (b, 0, 0, 0)),
                  full(wa2_pad), full(b_a2), full(gng), full(rng_), full(cos_row), full(sin_row), full(lg_row)],
        out_specs=(pl.BlockSpec((None, 1, wide), lambda b: (b, 0, 0)),
                   pl.BlockSpec((None, HA, dk, dv), lambda b: (b, 0, 0, 0)),
                   pl.BlockSpec((None, HB, dk, dv), lambda b: (b, 0, 0, 0))),
        compiler_params=_cparams(("arbitrary",)),
        name="even_decode",
    )(y_s.reshape(bsz, 1, n), state_gla, state_ret, wa2_pad, b_a2, gng, rng_, cos_row, sin_row, lg_row)
    return mixed.reshape(bsz, wide), sg, sr


def _qkv_kernel(q_ref, k_ref, v_ref, qg_ref, kg_ref, qn_ref, kn_ref, knb_ref, vo_ref, vb_ref):
    qscale = np.float32(q_ref.shape[-1] ** -0.5) * LOG2E
    qn_ref[...] = (_rms(q_ref[...], qg_ref[...]) * qscale).astype(BF16)
    kn = _rms(k_ref[...], kg_ref[...])
    kn_ref[...] = kn
    knb_ref[...] = kn.astype(BF16)
    v = v_ref[...]
    vo_ref[...] = v
    vb_ref[...] = v.astype(BF16)


def _qkv_prompt(y2, q_g, k_g, tm):
    m = y2.shape[0]
    dc = LANES
    blk = lambda off: pl.BlockSpec((tm, dc), lambda i, h: (i, off + h))
    gspec = pl.BlockSpec((1, dc), lambda i, h: (0, 0))
    return pl.pallas_call(
        _qkv_kernel,
        out_shape=(jax.ShapeDtypeStruct((m, HC * dc), BF16),
                   jax.ShapeDtypeStruct((m, HC * dc), F32),
                   jax.ShapeDtypeStruct((m, HC * dc), BF16),
                   jax.ShapeDtypeStruct((m, HC * dc), F32),
                   jax.ShapeDtypeStruct((m, HC * dc), BF16)),
        grid=(m // tm, HC),
        in_specs=[blk(0), blk(HC), blk(2 * HC), gspec, gspec],
        out_specs=(blk(0), blk(0), blk(0), blk(0), blk(0)),
        compiler_params=_cparams(("arbitrary", "arbitrary")),
        name="qkv_prompt",
    )(y2, y2, y2, q_g.reshape(1, dc), k_g.reshape(1, dc))


def _sb_kernel(qi_ref, kb_ref, off_ref, q_ref, k_ref, v_ref, g_ref, o_ref, acc_ref, run_ref, *, sub, hp):
    hg = pl.program_id(1)
    p = pl.program_id(2)
    qi = qi_ref[p]
    kb = kb_ref[p]
    tq = q_ref.shape[0]
    dc = q_ref.shape[1] // hp

    @pl.when(kb == qi)
    def _():
        acc_ref[...] = jnp.zeros_like(acc_ref)
        run_ref[...] = jnp.zeros_like(run_ref)

    def block(masked):
        r = lax.broadcasted_iota(jnp.int32, (sub, sub), 0)
        c = lax.broadcasted_iota(jnp.int32, (sub, sub), 1)
        upper = jnp.where(r >= c, 1.0, 0.0).astype(BF16)
        units = [(hh, s_i, s_i * sub if masked else 0) for hh in range(hp) for s_i in reversed(range(tq // sub))]
        zs, masks, splits, incls, ws = [], [], [], [], []
        for hh, s_i, r0 in units:
            hs = slice(hh * dc, (hh + 1) * dc)
            off = off_ref[hg * hp + hh] * LOG2E
            zs.append(_dot_nt(q_ref[r0:, hs], k_ref[s_i * sub:(s_i + 1) * sub, hs]) + off)
        for z in zs:
            rest = _neg_log2_rest(z)
            if masked:
                mask = (lax.broadcasted_iota(jnp.int32, z.shape, 1)
                        < lax.broadcasted_iota(jnp.int32, z.shape, 0))
                rest = jnp.where(mask, rest, 0.0)
                masks.append(mask)
            hi = rest.astype(BF16)
            splits.append(jnp.concatenate([hi, (rest - hi.astype(F32)).astype(BF16)], axis=1))
        upper2 = jnp.concatenate([upper, upper], axis=0)
        for hl in splits:
            incls.append(_dot(hl, upper2))
        for u, (hh, s_i, r0) in enumerate(units):
            run = run_ref[hh, r0:, :]
            w = jnp.exp2(zs[u] - incls[u] - run)
            if masked:
                w = jnp.where(masks[u], w, 0.0)
            ws.append(w.astype(BF16))
            run_ref[hh, r0:, :] = run + incls[u][:, 0:1]
        for u, (hh, s_i, r0) in enumerate(units):
            hs = slice(hh * dc, (hh + 1) * dc)
            acc_ref[r0:, hs] += _dot(ws[u], v_ref[s_i * sub:(s_i + 1) * sub, hs])

    pl.when(kb == qi)(functools.partial(block, True))
    pl.when(kb < qi)(functools.partial(block, False))

    @pl.when(kb == 0)
    def _():
        o_ref[...] = (acc_ref[...] * _silu(g_ref[...])).astype(o_ref.dtype)


def _sb_prompt(qn, kn, vb, y3, sb_offset, cg, tq, sub, hp):
    bsz, t, _ = qn.shape
    dc = LANES
    wide = hp * dc
    nq = t // tq
    pairs = [(qi, kb) for qi in range(nq) for kb in range(qi, -1, -1)]
    qi_tab = jnp.asarray([p[0] for p in pairs], jnp.int32)
    kb_tab = jnp.asarray([p[1] for p in pairs], jnp.int32)
    kern = functools.partial(_sb_kernel, sub=sub, hp=hp)
    q_map = lambda b, h, p, qi, kb, off: (b, qi[p], h)
    kv_map = lambda b, h, p, qi, kb, off: (b, kb[p], h)
    return pl.pallas_call(
        kern,
        out_shape=jax.ShapeDtypeStruct((bsz, t, HC * dc), BF16),
        grid_spec=pltpu.PrefetchScalarGridSpec(
            num_scalar_prefetch=3,
            grid=(bsz, HC // hp, len(pairs)),
            in_specs=[pl.BlockSpec((None, tq, wide), q_map),
                      pl.BlockSpec((None, tq, wide), kv_map),
                      pl.BlockSpec((None, tq, wide), kv_map),
                      pl.BlockSpec((None, tq, wide), lambda b, h, p, qi, kb, off: (b, qi[p], cg // wide + h))],
            out_specs=pl.BlockSpec((None, tq, wide), q_map),
            scratch_shapes=[pltpu.VMEM((tq, wide), F32), pltpu.VMEM((hp, tq, 1), F32)]),
        compiler_params=_cparams(("arbitrary",) * 3),
        name="sb_prompt",
    )(qi_tab, kb_tab, sb_offset, qn, kn, vb, y3)


def _cmlp_kernel(u_ref, v_ref, g_ref, vg_ref, ws_ref, bs_ref, o_ref, *, n_chunk):
    cs = D_CHUNK
    u = _gelu(u_ref[...])
    vn = _rms(_gelu(v_ref[...]), vg_ref[...]).astype(BF16)
    gate = _silu(g_ref[...])
    bs = bs_ref[...]
    r = lax.broadcasted_iota(jnp.int32, (cs, cs), 0)
    c = lax.broadcasted_iota(jnp.int32, (cs, cs), 1)
    for g in range(DG):
        w = jnp.where(r >= c, ws_ref[g], 0.0).astype(BF16)
        lo, hi = g * LANES, (g + 1) * LANES
        for ch in range(n_chunk):
            r0, r1 = ch * cs, (ch + 1) * cs
            mixed = _dot(w, vn[r0:r1, lo:hi]) + bs[:, lo:hi]
            o_ref[r0:r1, lo:hi] = (u[r0:r1, lo:hi] * mixed * gate[r0:r1, lo:hi]).astype(o_ref.dtype)


def _cmlp_prompt(y3, v_norm_g, w_s, bs_full, cols, tm):
    bsz, t, _ = y3.shape
    br = DG * LANES
    cu, cv, cg = cols
    kern = functools.partial(_cmlp_kernel, n_chunk=tm // D_CHUNK)
    return pl.pallas_call(
        kern,
        out_shape=jax.ShapeDtypeStruct((bsz, t, br), BF16),
        grid=(bsz, t // tm),
        in_specs=[pl.BlockSpec((None, tm, br), lambda b, i: (b, i, cu)),
                  pl.BlockSpec((None, tm, br), lambda b, i: (b, i, cv)),
                  pl.BlockSpec((None, tm, br), lambda b, i: (b, i, cg)),
                  pl.BlockSpec((1, br), lambda b, i: (0, 0)),
                  pl.BlockSpec((DG, D_CHUNK, D_CHUNK), lambda b, i: (0, 0, 0)),
                  pl.BlockSpec((D_CHUNK, br), lambda b, i: (0, 0))],
        out_specs=pl.BlockSpec((None, tm, br), lambda b, i: (b, i, 0)),
        compiler_params=_cparams(("arbitrary", "arbitrary")),
        name="cmlp_prompt",
    )(y3, y3, y3, v_norm_g.reshape(1, br), w_s, bs_full)


def _odd_decode_kernel(y_ref, qg_ref, kg_ref, vg_ref, w0_ref, b0_ref, qn_ref, kn_ref, vo_ref, cv_ref, od_ref):
    br = HC * LANES
    y = y_ref[...]
    for h in range(HC):
        lo, hi = h * LANES, (h + 1) * LANES
        qn_ref[:, lo:hi] = _rms(y[:, lo:hi], qg_ref[...])
        kn_ref[:, lo:hi] = _rms(y[:, br + lo:br + hi], kg_ref[...])
    vo_ref[...] = y[:, 2 * br:3 * br]
    u = _gelu(y[:, 4 * br:5 * br])
    vn = _rms(_gelu(y[:, 5 * br:6 * br]), vg_ref[...])
    cv_ref[...] = vn
    od = u * (w0_ref[...] * vn + b0_ref[...]) * _silu(y[:, 6 * br:7 * br])
    od_ref[...] = od.astype(od_ref.dtype)


def _odd_decode(y_s, q_g, k_g, v_norm_g, w0_row, b0_row):
    s = y_s.shape[0]
    br = HC * LANES
    args = (y_s, q_g.reshape(1, -1), k_g.reshape(1, -1), v_norm_g.reshape(1, -1), w0_row, b0_row)
    full = lambda a: pl.BlockSpec(a.shape, lambda i: (0,) * a.ndim)
    osd = lambda dt: jax.ShapeDtypeStruct((s, br), dt)
    ospec = pl.BlockSpec((s, br), lambda i: (0, 0))
    return pl.pallas_call(
        _odd_decode_kernel,
        out_shape=(osd(F32), osd(F32), osd(F32), osd(F32), osd(BF16)),
        grid=(1,),
        in_specs=[full(a) for a in args],
        out_specs=(ospec,) * 5,
        compiler_params=_cparams(("arbitrary",)),
        name="odd_decode",
    )(*args)


def _strided_suffix_sums(x, stride, axis):
    n = x.shape[axis]
    idx = lax.broadcasted_iota(jnp.int32, x.shape, axis)
    inc, tot = x, x
    k = stride
    while k < n:
        inc = inc + jnp.where(idx < n - k, pltpu.roll(inc, n - k, axis), 0.0)
        tot = tot + pltpu.roll(tot, n - k, axis)
        k *= 2
    return inc, tot


def _paged_kernel(pt_ref, q_ref, g_ref, off_ref, *refs, n_pp):
    k_refs, v_refs = refs[:n_pp], refs[n_pp:2 * n_pp]
    o_ref, acc_ref, run_ref = refs[2 * n_pp:]
    j = pl.program_id(1)
    page, hc, dc = k_refs[0].shape
    rows = page * hc
    lane = lax.broadcasted_iota(jnp.int32, (hc, rows), 1)
    own = (lane & (hc - 1)) == lax.broadcasted_iota(jnp.int32, (hc, rows), 0)

    @pl.when(j == 0)
    def _():
        acc_ref[...] = jnp.zeros_like(acc_ref)
        run_ref[...] = jnp.zeros_like(run_ref)

    q = (q_ref[...] * (np.float32(dc ** -0.5) * LOG2E)).astype(BF16)
    off = off_ref[...] * LOG2E
    prow = lax.broadcasted_iota(jnp.int32, (n_pp, rows), 0)
    z = jnp.zeros((n_pp, rows), F32)
    for p in range(n_pp):
        kf = k_refs[p][...].reshape(rows, dc).astype(BF16)
        s = _dot_nt(q, kf)
        z = jnp.where(prow == p, jnp.sum(jnp.where(own, s, 0.0), axis=0, keepdims=True), z)
    z = z + off
    incl, tot = _strided_suffix_sums(_neg_log2_rest(z), hc, 1)
    pages_incl, _ = _strided_suffix_sums(tot, 1, 0)
    run = run_ref[...]
    w = jnp.exp2(z - incl - (pages_incl - tot) - run)
    run_ref[...] = run + pages_incl[0:1, :]
    acc = acc_ref[...]
    for p in range(n_pp):
        wm = jnp.where(own, jnp.broadcast_to(w[p:p + 1, :], (hc, rows)), 0.0).astype(BF16)
        acc = acc + _dot(wm, v_refs[p][...].reshape(rows, dc).astype(BF16))
    acc_ref[...] = acc

    @pl.when(j == pl.num_programs(1) - 1)
    def _():
        o_ref[...] = (acc_ref[...] * _silu(g_ref[...])).astype(o_ref.dtype)


def _paged_attention(qn_s, gate_s, cache_k, cache_v, layer, page_table, sb_offset):
    s, hc, dc = qn_s.shape
    n_pages = page_table.shape[1]
    page = cache_k.shape[2]
    n_pp = PAGES_PER_STEP
    assert hc & (hc - 1) == 0 and n_pages % n_pp == 0
    rows = page * hc

    def kv_spec(p):
        return pl.BlockSpec((None, None, page, hc, dc),
                            lambda b, j, pt: (layer, pt[b, n_pages - (j + 1) * n_pp + p], 0, 0, 0))

    head_blk = pl.BlockSpec((None, hc, dc), lambda b, j, pt: (b, 0, 0))
    return pl.pallas_call(
        functools.partial(_paged_kernel, n_pp=n_pp),
        out_shape=jax.ShapeDtypeStruct((s, hc, dc), BF16),
        grid_spec=pltpu.PrefetchScalarGridSpec(
            num_scalar_prefetch=1,
            grid=(s, n_pages // n_pp),
            in_specs=[head_blk, head_blk, pl.BlockSpec((1, rows), lambda b, j, pt: (0, 0))]
            + [kv_spec(p) for p in range(n_pp)] * 2,
            out_specs=head_blk,
            scratch_shapes=[pltpu.VMEM((hc, dc), F32), pltpu.VMEM((1, rows), F32)]),
        compiler_params=_cparams(("arbitrary", "arbitrary")),
        name="paged_attention",
    )(page_table, qn_s, gate_s, jnp.tile(sb_offset, page).reshape(1, rows),
      *([cache_k] * n_pp), *([cache_v] * n_pp))


def _rope_tables(pos, half):
    inv = ROPE_BASE ** (-jnp.arange(half, dtype=F32) / half)
    ang = pos.astype(F32)[:, None] * inv[None, :]
    cos, sin = jnp.cos(ang), jnp.sin(ang)
    return jnp.concatenate([cos, cos], axis=-1), jnp.concatenate([-sin, sin], axis=-1)


def kernel(x_prompt, x_sample, state_gla, state_ret, cache_k, cache_v, page_table, c_prompt, c_sample,
           e_norm_g, e_w_ada, e_b_ada, e_w_in, e_w_a2, e_b_a, e_gla_norm_g, e_ret_norm_g, e_w_out,
           o_norm_g, o_w_ada, o_b_ada, o_w_in, o_q_norm_g, o_k_norm_g, o_sb_offset, o_v_norm_g,
           o_w_s, o_b_s, o_w_out):
    bsz, t, d = x_prompt.shape
    s = x_sample.shape[0]
    br = d // 2
    n_past = page_table.shape[1] * cache_k.shape[2]
    dk = LANES

    c_rows = jnp.concatenate([c_sample, c_prompt, jnp.zeros((16 - s - bsz, d), F32)], axis=0)

    n_qkv = 2 * HA * dk + HA * 2 * dk
    n_main = e_w_in.shape[2] - GLA_RANK
    col_ret = n_qkv + HA * 2 * dk
    cols_dec = (0, HA * dk, 2 * HA * dk, n_qkv,
                col_ret, col_ret + HB * dk, col_ret + 2 * HB * dk, col_ret + 2 * HB * dk + HB * 2 * dk,
                n_main)

    xp = x_prompt.reshape(bsz * t, d)
    xs = x_sample.reshape(s, d)
    pos_p = jnp.arange(t)
    pos_s = n_past + jnp.arange(x_sample.shape[1])
    cos_p, sin_p = _rope_tables(pos_p, dk // 2)
    cos_s, sin_s = _rope_tables(pos_s, dk // 2)
    log_gamma = jnp.log1p(-jnp.exp2(-5.0 - jnp.arange(HB, dtype=F32)))
    lg_row = jnp.repeat(log_gamma, dk).reshape(1, HB * dk)

    i = 0
    mod = _ada(c_rows, e_w_ada[i], e_b_ada[i])
    mod_s, mod_p = mod[:s], mod[s:s + bsz].reshape(bsz, 1, 3 * d)
    w_in = e_w_in[i]
    w_in_b = jnp.concatenate([w_in[:, :n_qkv], w_in[:, n_qkv + GLA_RANK:], w_in[:, n_qkv:n_qkv + GLA_RANK],
                              jnp.zeros((d, MXU_DIM - GLA_RANK), F32)], axis=1).astype(BF16)
    wa2_pad = jnp.concatenate([e_w_a2[i], jnp.zeros((LANES - GLA_RANK, HA * dk), F32)], axis=0).astype(BF16)
    w_out_b = e_w_out[i].astype(BF16)

    y = _proj(xp, mod_p, e_norm_g[i], w_in_b, t, 1024, 1280)
    y3 = y.reshape(bsz, t, -1)
    oa, gla_p = _gla_prompt(y3, wa2_pad, e_b_a[i], e_gla_norm_g[i], 512)
    ob, ret_p = _ret_prompt(y3, cos_p, sin_p, lg_row, e_ret_norm_g[i], col_ret, 512)
    xp = _outproj(oa.reshape(bsz * t, br), 0, ob.reshape(bsz * t, br), 0, w_out_b, xp, mod_p, t, 512, d)

    y_s = _proj(xs, mod_s, e_norm_g[i], w_in_b, None, s, 1280)
    mixed_s, gla_s, ret_s = _even_decode(y_s, state_gla[i], state_ret[i], wa2_pad, e_b_a[i],
                                         e_gla_norm_g[i], e_ret_norm_g[i], cos_s, sin_s, lg_row, cols_dec)
    xs = _outproj(mixed_s, 0, mixed_s, 1, w_out_b, xs, mod_s, None, s, 1024)

    mod = _ada(c_rows, o_w_ada[i], o_b_ada[i])
    mod_s, mod_p = mod[:s], mod[s:s + bsz].reshape(bsz, 1, 3 * d)
    w_in_b = o_w_in[i].astype(BF16)
    w_out_b = o_w_out[i].astype(BF16)
    bs_full = jnp.repeat(jnp.transpose(o_b_s[i]), LANES, axis=1)

    y = _proj(xp, mod_p, o_norm_g[i], w_in_b, t, 1024, 1024)
    y3 = y.reshape(bsz, t, -1)
    qn, kn, knb, vo, vb = _qkv_prompt(y, o_q_norm_g[i], o_k_norm_g[i], 1024)
    shp = (bsz, t, br)
    oc = _sb_prompt(qn.reshape(shp), knb.reshape(shp), vb.reshape(shp), y3, o_sb_offset[i], 3 * br, 512, 256, 2)
    od = _cmlp_prompt(y3, o_v_norm_g[i], o_w_s[i], bs_full, (4, 5, 6), 256)
    xp = _outproj(oc.reshape(bsz * t, br), 0, od.reshape(bsz * t, br), 0, w_out_b, xp, mod_p, t, 512, d)

    y_s = _proj(xs, mod_s, o_norm_g[i], w_in_b, None, s, 1024)
    w0_row = jnp.repeat(o_w_s[i][:, 0, 0], LANES).reshape(1, br)
    b0_row = jnp.repeat(o_b_s[i][:, 0], LANES).reshape(1, br)
    qn_s, kn_s, vo_s, cv_s, od_s = _odd_decode(y_s, o_q_norm_g[i], o_k_norm_g[i], o_v_norm_g[i], w0_row, b0_row)
    dc = LANES
    oc_s = _paged_attention(qn_s.reshape(s, HC, dc), y_s[:, 3 * br:4 * br].reshape(s, HC, dc),
                            cache_k, cache_v, i, page_table, o_sb_offset[i]).reshape(s, br)
    xs = _outproj(oc_s, 0, od_s, 0, w_out_b, xs, mod_s, None, s, 1024)

    return (xp.reshape(bsz, t, d), xs.reshape(s, 1, d),
            gla_p[None], gla_s[None], ret_p[None], ret_s[None],
            kn.reshape(1, bsz, t, HC, dc), vo.reshape(1, bsz, t, HC, dc),
            kn_s.reshape(1, s, 1, HC, dc), vo_s.reshape(1, s, 1, HC, dc),
            cv_s.reshape(1, s, 1, br))
```

```python
import functools

import jax
import jax.numpy as jnp
import numpy as np
from jax import lax
from jax.experimental import pallas as pl
from jax.experimental.pallas import tpu as pltpu

F32 = jnp.float32
BF16 = jnp.bfloat16

HA = 4
HB = 4
HC = 8
DG = 8
GLA_RANK = 16
GLA_TAU = 16.0
GLA_CHUNK = 64
GLA_SAFE_LOG_DECAY = 40.0
RET_CHUNK = 128
ROPE_BASE = 10000.0
D_CHUNK = 128
RMS_EPS = 1e-6

LANES = 128
MXU_DIM = 256
VMEM_LIMIT = 56 * 1024 * 1024
LOG2E = np.float32(np.log2(np.e))
PAGES_PER_STEP = 8


def _cparams(sem):
    return pltpu.CompilerParams(dimension_semantics=sem, vmem_limit_bytes=VMEM_LIMIT)


def _dot(a, b):
    return jnp.dot(a, b, preferred_element_type=F32)


def _dot_nt(a, b):
    return lax.dot_general(a, b, (((1,), (1,)), ((), ())), preferred_element_type=F32)


def _dot_tn(a, b):
    return lax.dot_general(a, b, (((0,), (0,)), ((), ())), preferred_element_type=F32)


def _sigmoid(x):
    return 1.0 / (1.0 + jnp.exp(-x))


def _silu(x):
    return x * _sigmoid(x)


def _gelu(x):
    c = np.float32(np.sqrt(2.0 / np.pi))
    return 0.5 * x * (1.0 + jnp.tanh(c * (x + 0.044715 * (x * x * x))))


def _softplus_neg_abs(x):
    return jnp.log(1.0 + jnp.exp(-jnp.abs(x)))


def _neg_log2_rest(z2):
    return jnp.maximum(z2, 0.0) + jnp.log(1.0 + jnp.exp2(-jnp.abs(z2))) * LOG2E


def _split_dot(x, ones_bf16, left=False):
    hi = x.astype(BF16)
    lo = (x - hi.astype(F32)).astype(BF16)
    if left:
        return _dot(ones_bf16, hi) + _dot(ones_bf16, lo)
    return _dot(hi, ones_bf16) + _dot(lo, ones_bf16)


def _rms(x, g):
    ms = jnp.mean(x * x, axis=-1, keepdims=True)
    return x * lax.rsqrt(ms + RMS_EPS) * g


def _ada_kernel(c_ref, w_ref, b_ref, o_ref):
    s = _silu(c_ref[...]).astype(BF16)
    o_ref[...] = _dot(s, w_ref[...].astype(BF16)) + b_ref[...]


def _ada(c_rows, w, b):
    r, d = c_rows.shape
    n = w.shape[1]
    tn = 512
    return pl.pallas_call(
        _ada_kernel,
        out_shape=jax.ShapeDtypeStruct((r, n), F32),
        grid=(n // tn,),
        in_specs=[pl.BlockSpec((r, d), lambda j: (0, 0)),
                  pl.BlockSpec((d, tn), lambda j: (0, j)),
                  pl.BlockSpec((1, tn), lambda j: (0, j))],
        out_specs=pl.BlockSpec((r, tn), lambda j: (0, j)),
        compiler_params=_cparams(("arbitrary",)),
        name="ada",
    )(c_rows, w, b.reshape(1, n))


def _reorder_cast_kernel(w_ref, o_ref, *, n_head, n_rank):
    w = w_ref[...]
    n = w.shape[1]
    tail = n - n_head - n_rank
    o_ref[:, :n_head] = w[:, :n_head].astype(BF16)
    o_ref[:, n_head:n_head + tail] = w[:, n_head + n_rank:].astype(BF16)
    zeros = jnp.zeros((w.shape[0], o_ref.shape[1] - n), w.dtype)
    o_ref[:, n_head + tail:] = jnp.concatenate([w[:, n_head:n_head + n_rank], zeros], axis=1).astype(BF16)


def _reorder_cast(w3, layer, n_head, n_rank, n_out, tk):
    _, d, n = w3.shape
    return pl.pallas_call(
        functools.partial(_reorder_cast_kernel, n_head=n_head, n_rank=n_rank),
        out_shape=jax.ShapeDtypeStruct((d, n_out), BF16),
        grid=(d // tk,),
        in_specs=[pl.BlockSpec((None, tk, n), lambda r: (layer, r, 0))],
        out_specs=pl.BlockSpec((tk, n_out), lambda r: (r, 0)),
        compiler_params=_cparams(("arbitrary",)),
        name="reorder_cast",
    )(w3)


def _proj_kernel(x_ref, shift_ref, scale_ref, g_ref, w_ref, o_ref, h_ref):
    @pl.when(pl.program_id(1) == 0)
    def _():
        h = _rms(x_ref[...], g_ref[...]) * (1.0 + scale_ref[...]) + shift_ref[...]
        h_ref[...] = h.astype(BF16)

    o_ref[...] = _dot(h_ref[...], w_ref[...]).astype(o_ref.dtype)


def _proj(x2d, mod, norm_g, w_bf16, rows_per_batch, tm, tn):
    m, d = x2d.shape
    out_dtype = F32 if rows_per_batch is None else BF16
    n = w_bf16.shape[1]
    if rows_per_batch is None:
        shift_spec = pl.BlockSpec((tm, d), lambda i, j: (i, 0))
        scale_spec = pl.BlockSpec((tm, d), lambda i, j: (i, 1))
    else:
        tpb = rows_per_batch // tm
        shift_spec = pl.BlockSpec((None, 1, d), lambda i, j: (i // tpb, 0, 0))
        scale_spec = pl.BlockSpec((None, 1, d), lambda i, j: (i // tpb, 0, 1))
    return pl.pallas_call(
        _proj_kernel,
        out_shape=jax.ShapeDtypeStruct((m, n), out_dtype),
        grid=(m // tm, n // tn),
        in_specs=[pl.BlockSpec((tm, d), lambda i, j: (i, 0)),
                  shift_spec, scale_spec,
                  pl.BlockSpec((1, d), lambda i, j: (0, 0)),
                  pl.BlockSpec((d, tn), lambda i, j: (0, j))],
        out_specs=pl.BlockSpec((tm, tn), lambda i, j: (i, j)),
        scratch_shapes=[pltpu.VMEM((tm, d), BF16)],
        compiler_params=_cparams(("arbitrary", "arbitrary")),
        name="proj",
    )(x2d, mod, mod, norm_g.reshape(1, d), w_bf16)


def _outproj_kernel(ma_ref, mb_ref, wa_ref, wb_ref, x_ref, gate_ref, o_ref):
    mixed = _dot(ma_ref[...], wa_ref[...]) + _dot(mb_ref[...], wb_ref[...])
    o_ref[...] = x_ref[...] + gate_ref[...] * mixed


def _outproj(ma, ca, mb, cb, w_bf16, x2d, mod, rows_per_batch, tm, tn):
    m, d = x2d.shape
    half = w_bf16.shape[0] // 2
    gcol = 2 * (d // tn)
    if rows_per_batch is None:
        gate_spec = pl.BlockSpec((tm, tn), lambda i, j: (i, gcol + j))
    else:
        tpb = rows_per_batch // tm
        gate_spec = pl.BlockSpec((None, 1, tn), lambda i, j: (i // tpb, 0, gcol + j))
    return pl.pallas_call(
        _outproj_kernel,
        out_shape=jax.ShapeDtypeStruct((m, d), F32),
        grid=(m // tm, d // tn),
        in_specs=[pl.BlockSpec((tm, half), lambda i, j: (i, ca)),
                  pl.BlockSpec((tm, half), lambda i, j: (i, cb)),
                  pl.BlockSpec((half, tn), lambda i, j: (0, j)),
                  pl.BlockSpec((half, tn), lambda i, j: (1, j)),
                  pl.BlockSpec((tm, tn), lambda i, j: (i, j)),
                  gate_spec],
        out_specs=pl.BlockSpec((tm, tn), lambda i, j: (i, j)),
        compiler_params=_cparams(("arbitrary", "arbitrary")),
        name="outproj",
    )(ma, mb, w_bf16, w_bf16, x2d, mod)


def _gla_kernel(q_ref, k_ref, v_ref, ra_ref, g_ref, wa2_ref, ba_ref, ng_ref,
                o_ref, st_ref, stt_ref, la_ref, b_ref, *, n_sub, dk, dv):
    c = pl.program_id(1)
    cs = GLA_CHUNK

    @pl.when(c == 0)
    def _():
        stt_ref[...] = jnp.zeros_like(stt_ref)

    row = lax.broadcasted_iota(jnp.int32, (cs, cs), 0)
    col = lax.broadcasted_iota(jnp.int32, (cs, cs), 1)
    causal = row >= col
    tri = jnp.where(causal, 1.0, 0.0).astype(BF16)
    qscale = np.float32(dk ** -0.5)

    pre = _dot(ra_ref[...].astype(BF16), wa2_ref[...]) + ba_ref[...]
    la = (jnp.minimum(pre, 0.0) - _softplus_neg_abs(pre)) * np.float32(1.0 / GLA_TAU)
    la_ref[...] = la
    safe = jnp.min(la) >= np.float32(-GLA_SAFE_LOG_DECAY / cs)

    def pairwise_scores(qs, kb, hs):
        b = b_ref[:, hs]
        sublanes = 8

        def key_group(gi, acc):
            s0 = pl.multiple_of(gi * sublanes, sublanes)
            bk = b_ref[pl.ds(s0, sublanes), hs]
            for r in range(sublanes):
                e = jnp.exp(jnp.minimum(b - bk[r:r + 1, :], 0.0))
                acc = jnp.where(col == s0 + r, _dot_nt((qs * e).astype(BF16), kb), acc)
            return acc

        return lax.fori_loop(0, cs // sublanes, key_group, jnp.zeros((cs, cs), F32))

    def make_body(factorised):
        def body(i, carry):
            r0 = pl.multiple_of(i * cs, cs)
            rows = pl.ds(r0, cs)
            b = _split_dot(la_ref[rows, :], tri, left=True)
            bl = b[cs - 1:cs, :]
            k = k_ref[rows, :].astype(F32)
            qs = q_ref[rows, :].astype(F32) * qscale
            qe = (qs * jnp.exp(b)).astype(BF16)
            kd = (k * jnp.exp(bl - b)).astype(BF16)
            ebl = jnp.exp(bl)
            if factorised:
                ke = (k * jnp.exp(-b)).astype(BF16)
            else:
                b_ref[...] = b
            for h in range(HA):
                ks, vs = slice(h * dk, (h + 1) * dk), slice(h * dv, (h + 1) * dv)
                v = v_ref[rows, vs].astype(BF16)
                if factorised:
                    s = _dot_nt(qe[:, ks], ke[:, ks])
                else:
                    s = pairwise_scores(qs[:, ks], k[:, ks].astype(BF16), ks)
                s = jnp.where(causal, s, 0.0)
                stt = stt_ref[h]
                o = _dot(s.astype(BF16), v) + _dot_nt(qe[:, ks], stt.astype(BF16))
                stt_ref[h] = stt * ebl[:, ks] + _dot_tn(v, kd[:, ks])
                og = _rms(o, ng_ref[:, vs]) * _silu(g_ref[rows, vs].astype(F32))
                o_ref[rows, vs] = og.astype(o_ref.dtype)
            return carry
        return body

    @pl.when(safe)
    def _():
        lax.fori_loop(0, n_sub, make_body(True), 0)

    @pl.when(jnp.logical_not(safe))
    def _():
        lax.fori_loop(0, n_sub, make_body(False), 0)

    @pl.when(c == pl.num_programs(1) - 1)
    def _():
        for h in range(HA):
            st_ref[h] = stt_ref[h].T


def _gla_prompt(y3, wa2_pad, b_a, norm_g, tc):
    bsz, t, n = y3.shape
    dk, dv = LANES, 2 * LANES
    wk, wv = HA * dk, HA * dv
    kern = functools.partial(_gla_kernel, n_sub=tc // GLA_CHUNK, dk=dk, dv=dv)
    return pl.pallas_call(
        kern,
        out_shape=(jax.ShapeDtypeStruct((bsz, t, wv), BF16),
                   jax.ShapeDtypeStruct((bsz, HA, dk, dv), F32)),
        grid=(bsz, t // tc),
        in_specs=[pl.BlockSpec((None, tc, wk), lambda b, c: (b, c, 0)),
                  pl.BlockSpec((None, tc, wk), lambda b, c: (b, c, 1)),
                  pl.BlockSpec((None, tc, wv), lambda b, c: (b, c, 1)),
                  pl.BlockSpec((None, tc, LANES), lambda b, c: (b, c, (n - MXU_DIM) // LANES)),
                  pl.BlockSpec((None, tc, wv), lambda b, c: (b, c, 2)),
                  pl.BlockSpec((LANES, wk), lambda b, c: (0, 0)),
                  pl.BlockSpec((1, wk), lambda b, c: (0, 0)),
                  pl.BlockSpec((1, wv), lambda b, c: (0, 0))],
        out_specs=(pl.BlockSpec((None, tc, wv), lambda b, c: (b, c, 0)),
                   pl.BlockSpec((None, HA, dk, dv), lambda b, c: (b, 0, 0, 0))),
        scratch_shapes=[pltpu.VMEM((HA, dv, dk), F32), pltpu.VMEM((tc, wk), F32),
                        pltpu.VMEM((GLA_CHUNK, wk), F32)],
        compiler_params=_cparams(("arbitrary", "arbitrary")),
        name="gla_prompt",
    )(y3, y3, y3, y3, y3, wa2_pad, b_a.reshape(1, -1), norm_g.reshape(1, -1))


def _rot(x, cosf, sinf):
    return x * cosf + pltpu.roll(x, x.shape[-1] // 2, 1) * sinf


def _ret_kernel(q_ref, k_ref, v_ref, g_ref, cos_ref, sin_ref, lg_ref, ng_ref,
                o_ref, st_ref, stt_ref, *, n_sub, dk, dv):
    c = pl.program_id(1)
    cs = RET_CHUNK

    @pl.when(c == 0)
    def _():
        stt_ref[...] = jnp.zeros_like(stt_ref)

    row = lax.broadcasted_iota(jnp.int32, (cs, cs), 0)
    col = lax.broadcasted_iota(jnp.int32, (cs, cs), 1)
    rel = (row - col).astype(F32)
    ridx = lax.broadcasted_iota(jnp.int32, (cs, 1), 0).astype(F32)
    lgs = [lg_ref[:, h * dk:h * dk + 1] for h in range(HB)]
    decay = [jnp.where(row >= col, jnp.exp(rel * lg), 0.0) for lg in lgs]
    inter = [jnp.exp((ridx + 1.0) * lg) for lg in lgs]
    kdec = [jnp.exp((np.float32(cs - 1.0) - ridx) * lg) for lg in lgs]
    sdec = [jnp.exp(np.float32(cs) * lg) for lg in lgs]
    qscale = np.float32(dk ** -0.5)

    def body(i, carry):
        r0 = pl.multiple_of(i * cs, cs)
        rows = pl.ds(r0, cs)
        cosf = cos_ref[rows, :]
        sinf = sin_ref[rows, :]
        for h in range(HB):
            ks, vs = slice(h * dk, (h + 1) * dk), slice(h * dv, (h + 1) * dv)
            qr = (_rot(q_ref[rows, ks].astype(F32), cosf, sinf) * qscale).astype(BF16)
            kr = _rot(k_ref[rows, ks].astype(F32), cosf, sinf)
            v = v_ref[rows, vs].astype(BF16)
            s = _dot_nt(qr, kr.astype(BF16)) * decay[h]
            stt = stt_ref[h]
            o = _dot(s.astype(BF16), v) + _dot_nt(qr, stt.astype(BF16)) * inter[h]
            stt_ref[h] = stt * sdec[h] + _dot_tn(v, (kr * kdec[h]).astype(BF16))
            og = _rms(o, ng_ref[:, vs]) * _silu(g_ref[rows, vs].astype(F32))
            o_ref[rows, vs] = og.astype(o_ref.dtype)
        return carry

    lax.fori_loop(0, n_sub, body, 0)

    @pl.when(c == pl.num_programs(1) - 1)
    def _():
        for h in range(HB):
            st_ref[h] = stt_ref[h].T


def _ret_prompt(y3, cosf, sinf, lg_row, norm_g, col0, tc):
    bsz, t, _ = y3.shape
    dk, dv = LANES, 2 * LANES
    wk, wv = HB * dk, HB * dv
    cq, cv = col0 // wk, (col0 + 2 * wk) // wv
    kern = functools.partial(_ret_kernel, n_sub=tc // RET_CHUNK, dk=dk, dv=dv)
    return pl.pallas_call(
        kern,
        out_shape=(jax.ShapeDtypeStruct((bsz, t, wv), BF16),
                   jax.ShapeDtypeStruct((bsz, HB, dk, dv), F32)),
        grid=(bsz, t // tc),
        in_specs=[pl.BlockSpec((None, tc, wk), lambda b, c: (b, c, cq)),
                  pl.BlockSpec((None, tc, wk), lambda b, c: (b, c, cq + 1)),
                  pl.BlockSpec((None, tc, wv), lambda b, c: (b, c, cv)),
                  pl.BlockSpec((None, tc, wv), lambda b, c: (b, c, cv + 1)),
                  pl.BlockSpec((tc, dk), lambda b, c: (c, 0)),
                  pl.BlockSpec((tc, dk), lambda b, c: (c, 0)),
                  pl.BlockSpec((1, wk), lambda b, c: (0, 0)),
                  pl.BlockSpec((1, wv), lambda b, c: (0, 0))],
        out_specs=(pl.BlockSpec((None, tc, wv), lambda b, c: (b, c, 0)),
                   pl.BlockSpec((None, HB, dk, dv), lambda b, c: (b, 0, 0, 0))),
        scratch_shapes=[pltpu.VMEM((HB, dv, dk), F32)],
        compiler_params=_cparams(("arbitrary", "arbitrary")),
        name="ret_prompt",
    )(y3, y3, y3, y3, cosf, sinf, lg_row, norm_g.reshape(1, -1))


def _row_to_col(x_row):
    n = x_row.shape[-1]
    r = lax.broadcasted_iota(jnp.int32, (n, n), 0)
    c = lax.broadcasted_iota(jnp.int32, (n, n), 1)
    return jnp.sum(jnp.where(r == c, jnp.broadcast_to(x_row, (n, n)), 0.0), axis=1, keepdims=True)


def _even_decode_kernel(y_ref, sg_ref, sr_ref, wa2_ref, ba_ref, gng_ref, rng_ref, cos_ref, sin_ref,
                        lg_ref, o_ref, sgo_ref, sro_ref, *, cols):
    dk, dv = LANES, 2 * LANES
    cqa, cka, cva, cga, cqb, ckb, cvb, cgb, cra = cols
    y = y_ref[...]
    pre = _dot(y[:, cra:cra + LANES].astype(BF16), wa2_ref[...]) + ba_ref[...]
    la = (jnp.minimum(pre, 0.0) - _softplus_neg_abs(pre)) * np.float32(1.0 / GLA_TAU)
    alpha = jnp.exp(la)
    cosf = cos_ref[...]
    sinf = sin_ref[...]
    gam = jnp.exp(lg_ref[...])
    outs = []
    for h in range(HA):
        q = y[:, cqa + h * dk:cqa + (h + 1) * dk] * np.float32(dk ** -0.5)
        k = y[:, cka + h * dk:cka + (h + 1) * dk]
        v = y[:, cva + h * dv:cva + (h + 1) * dv]
        s_new = _row_to_col(alpha[:, h * dk:(h + 1) * dk]) * sg_ref[h] + _row_to_col(k) * v
        sgo_ref[h] = s_new
        o = jnp.sum(_row_to_col(q) * s_new, axis=0, keepdims=True)
        g = y[:, cga + h * dv:cga + (h + 1) * dv]
        outs.append(_rms(o, gng_ref[:, h * dv:(h + 1) * dv]) * _silu(g))
    for h in range(HB):
        q = _rot(y[:, cqb + h * dk:cqb + (h + 1) * dk], cosf, sinf) * np.float32(dk ** -0.5)
        k = _rot(y[:, ckb + h * dk:ckb + (h + 1) * dk], cosf, sinf)
        v = y[:, cvb + h * dv:cvb + (h + 1) * dv]
        s_new = _row_to_col(gam[:, h * dk:(h + 1) * dk]) * sr_ref[h] + _row_to_col(k) * v
        sro_ref[h] = s_new
        o = jnp.sum(_row_to_col(q) * s_new, axis=0, keepdims=True)
        g = y[:, cgb + h * dv:cgb + (h + 1) * dv]
        outs.append(_rms(o, rng_ref[:, h * dv:(h + 1) * dv]) * _silu(g))
    o_ref[...] = jnp.concatenate(outs, axis=-1).astype(o_ref.dtype)


def _even_decode(y_s, state_gla, state_ret, wa2_pad, b_a, gla_ng, ret_ng, cos_row, sin_row, lg_row, cols):
    bsz, n = y_s.shape
    dk, dv = LANES, 2 * LANES
    wide = (HA + HB) * dv
    kern = functools.partial(_even_decode_kernel, cols=cols)
    full = lambda a: pl.BlockSpec(a.shape, lambda b: (0,) * a.ndim)
    b_a2, gng, rng_ = b_a.reshape(1, -1), gla_ng.reshape(1, -1), ret_ng.reshape(1, -1)
    mixed, sg, sr = pl.pallas_call(
        kern,
        out_shape=(jax.ShapeDtypeStruct((bsz, 1, wide), BF16),
                   jax.ShapeDtypeStruct(state_gla.shape, F32),
                   jax.ShapeDtypeStruct(state_ret.shape, F32)),
        grid=(bsz,),
        in_specs=[pl.BlockSpec((None, 1, n), lambda b: (b, 0, 0)),
                  pl.BlockSpec((None, HA, dk, dv), lambda b: (b, 0, 0, 0)),
                  pl.BlockSpec((None, HB, dk, dv), lambda b: (b, 0, 0, 0)),
                  full(wa2_pad), full(b_a2), full(gng), full(rng_), full(cos_row), full(sin_row), full(lg_row)],
        out_specs=(pl.BlockSpec((None, 1, wide), lambda b: (b, 0, 0)),
                   pl.BlockSpec((None, HA, dk, dv), lambda b: (b, 0, 0, 0)),
                   pl.BlockSpec((None, HB, dk, dv), lambda b: (b, 0, 0, 0))),
        compiler_params=_cparams(("arbitrary",)),
        name="even_decode",
    )(y_s.reshape(bsz, 1, n), state_gla, state_ret, wa2_pad, b_a2, gng, rng_, cos_row, sin_row, lg_row)
    return mixed.reshape(bsz, wide), sg, sr


def _qkv_kernel(q_ref, k_ref, v_ref, qg_ref, kg_ref, qn_ref, kn_ref, knb_ref, vo_ref, vb_ref):
    qscale = np.float32(q_ref.shape[-1] ** -0.5) * LOG2E
    qn_ref[...] = (_rms(q_ref[...].astype(F32), qg_ref[...]) * qscale).astype(BF16)
    kn = _rms(k_ref[...].astype(F32), kg_ref[...])
    kn_ref[...] = kn
    knb_ref[...] = kn.astype(BF16)
    v = v_ref[...]
    vo_ref[...] = v.astype(F32)
    vb_ref[...] = v.astype(BF16)


def _qkv_prompt(y2, q_g, k_g, tm):
    m = y2.shape[0]
    dc = LANES
    blk = lambda off: pl.BlockSpec((tm, dc), lambda i, h: (i, off + h))
    gspec = pl.BlockSpec((1, dc), lambda i, h: (0, 0))
    return pl.pallas_call(
        _qkv_kernel,
        out_shape=(jax.ShapeDtypeStruct((m, HC * dc), BF16),
                   jax.ShapeDtypeStruct((m, HC * dc), F32),
                   jax.ShapeDtypeStruct((m, HC * dc), BF16),
                   jax.ShapeDtypeStruct((m, HC * dc), F32),
                   jax.ShapeDtypeStruct((m, HC * dc), BF16)),
        grid=(m // tm, HC),
        in_specs=[blk(0), blk(HC), blk(2 * HC), gspec, gspec],
        out_specs=(blk(0), blk(0), blk(0), blk(0), blk(0)),
        compiler_params=_cparams(("arbitrary", "arbitrary")),
        name="qkv_prompt",
    )(y2, y2, y2, q_g.reshape(1, dc), k_g.reshape(1, dc))


def _sb_kernel(qi_ref, kb_ref, off_ref, q_ref, k_ref, v_ref, g_ref, o_ref, acc_ref, run_ref, *, sub, hp):
    hg = pl.program_id(1)
    p = pl.program_id(2)
    qi = qi_ref[p]
    kb = kb_ref[p]
    tq = q_ref.shape[0]
    dc = q_ref.shape[1] // hp

    @pl.when(kb == qi)
    def _():
        acc_ref[...] = jnp.zeros_like(acc_ref)
        run_ref[...] = jnp.zeros_like(run_ref)

    def block(masked):
        r = lax.broadcasted_iota(jnp.int32, (sub, sub), 0)
        c = lax.broadcasted_iota(jnp.int32, (sub, sub), 1)
        upper = jnp.where(r >= c, 1.0, 0.0).astype(BF16)
        units = [(hh, s_i, s_i * sub if masked else 0) for hh in range(hp) for s_i in reversed(range(tq // sub))]
        zs, masks, splits, incls, ws = [], [], [], [], []
        for hh, s_i, r0 in units:
            hs = slice(hh * dc, (hh + 1) * dc)
            off = off_ref[hg * hp + hh] * LOG2E
            zs.append(_dot_nt(q_ref[r0:, hs], k_ref[s_i * sub:(s_i + 1) * sub, hs]) + off)
        for z in zs:
            rest = _neg_log2_rest(z)
            if masked:
                mask = (lax.broadcasted_iota(jnp.int32, z.shape, 1)
                        < lax.broadcasted_iota(jnp.int32, z.shape, 0))
                rest = jnp.where(mask, rest, 0.0)
                masks.append(mask)
            hi = rest.astype(BF16)
            splits.append(jnp.concatenate([hi, (rest - hi.astype(F32)).astype(BF16)], axis=1))
        upper2 = jnp.concatenate([upper, upper], axis=0)
        for hl in splits:
            incls.append(_dot(hl, upper2))
        for u, (hh, s_i, r0) in enumerate(units):
            run = run_ref[hh, r0:, :]
            w = jnp.exp2(zs[u] - incls[u] - run)
            if masked:
                w = jnp.where(masks[u], w, 0.0)
            ws.append(w.astype(BF16))
            run_ref[hh, r0:, :] = run + incls[u][:, 0:1]
        for u, (hh, s_i, r0) in enumerate(units):
            hs = slice(hh * dc, (hh + 1) * dc)
            acc_ref[r0:, hs] += _dot(ws[u], v_ref[s_i * sub:(s_i + 1) * sub, hs])

    pl.when(kb == qi)(functools.partial(block, True))
    pl.when(kb < qi)(functools.partial(block, False))

    @pl.when(kb == 0)
    def _():
        o_ref[...] = (acc_ref[...] * _silu(g_ref[...].astype(F32))).astype(o_ref.dtype)


def _sb_prompt(qn, kn, vb, y3, sb_offset, cg, tq, sub, hp):
    bsz, t, _ = qn.shape
    dc = LANES
    wide = hp * dc
    nq = t // tq
    pairs = [(qi, kb) for qi in range(nq) for kb in range(qi, -1, -1)]
    qi_tab = jnp.asarray([p[0] for p in pairs], jnp.int32)
    kb_tab = jnp.asarray([p[1] for p in pairs], jnp.int32)
    kern = functools.partial(_sb_kernel, sub=sub, hp=hp)
    q_map = lambda b, h, p, qi, kb, off: (b, qi[p], h)
    kv_map = lambda b, h, p, qi, kb, off: (b, kb[p], h)
    return pl.pallas_call(
        kern,
        out_shape=jax.ShapeDtypeStruct((bsz, t, HC * dc), BF16),
        grid_spec=pltpu.PrefetchScalarGridSpec(
            num_scalar_prefetch=3,
            grid=(bsz, HC // hp, len(pairs)),
            in_specs=[pl.BlockSpec((None, tq, wide), q_map),
                      pl.BlockSpec((None, tq, wide), kv_map),
                      pl.BlockSpec((None, tq, wide), kv_map),
                      pl.BlockSpec((None, tq, wide), lambda b, h, p, qi, kb, off: (b, qi[p], cg // wide + h))],
            out_specs=pl.BlockSpec((None, tq, wide), q_map),
            scratch_shapes=[pltpu.VMEM((tq, wide), F32), pltpu.VMEM((hp, tq, 1), F32)]),
        compiler_params=_cparams(("arbitrary",) * 3),
        name="sb_prompt",
    )(qi_tab, kb_tab, sb_offset, qn, kn, vb, y3)


def _cmlp_kernel(u_ref, v_ref, g_ref, vg_ref, ws_ref, bs_ref, o_ref, *, n_chunk):
    cs = D_CHUNK
    u = _gelu(u_ref[...].astype(F32))
    vn = _rms(_gelu(v_ref[...].astype(F32)), vg_ref[...]).astype(BF16)
    gate = _silu(g_ref[...].astype(F32))
    bs = bs_ref[...]
    r = lax.broadcasted_iota(jnp.int32, (cs, cs), 0)
    c = lax.broadcasted_iota(jnp.int32, (cs, cs), 1)
    for g in range(DG):
        w = jnp.where(r >= c, ws_ref[g], 0.0).astype(BF16)
        lo, hi = g * LANES, (g + 1) * LANES
        for ch in range(n_chunk):
            r0, r1 = ch * cs, (ch + 1) * cs
            mixed = _dot(w, vn[r0:r1, lo:hi]) + bs[:, lo:hi]
            o_ref[r0:r1, lo:hi] = (u[r0:r1, lo:hi] * mixed * gate[r0:r1, lo:hi]).astype(o_ref.dtype)


def _cmlp_prompt(y3, v_norm_g, w_s, bs_full, cols, tm):
    bsz, t, _ = y3.shape
    br = DG * LANES
    cu, cv, cg = cols
    kern = functools.partial(_cmlp_kernel, n_chunk=tm // D_CHUNK)
    return pl.pallas_call(
        kern,
        out_shape=jax.ShapeDtypeStruct((bsz, t, br), BF16),
        grid=(bsz, t // tm),
        in_specs=[pl.BlockSpec((None, tm, br), lambda b, i: (b, i, cu)),
                  pl.BlockSpec((None, tm, br), lambda b, i: (b, i, cv)),
                  pl.BlockSpec((None, tm, br), lambda b, i: (b, i, cg)),
                  pl.BlockSpec((1, br), lambda b, i: (0, 0)),
                  pl.BlockSpec((DG, D_CHUNK, D_CHUNK), lambda b, i: (0, 0, 0)),
                  pl.BlockSpec((D_CHUNK, br), lambda b, i: (0, 0))],
        out_specs=pl.BlockSpec((None, tm, br), lambda b, i: (b, i, 0)),
        compiler_params=_cparams(("arbitrary", "arbitrary")),
        name="cmlp_prompt",
    )(y3, y3, y3, v_norm_g.reshape(1, br), w_s, bs_full)


def _odd_decode_kernel(y_ref, qg_ref, kg_ref, vg_ref, w0_ref, b0_ref, qn_ref, kn_ref, vo_ref, cv_ref, od_ref):
    br = HC * LANES
    y = y_ref[...]
    for h in range(HC):
        lo, hi = h * LANES, (h + 1) * LANES
        qn_ref[:, lo:hi] = _rms(y[:, lo:hi], qg_ref[...])
        kn_ref[:, lo:hi] = _rms(y[:, br + lo:br + hi], kg_ref[...])
    vo_ref[...] = y[:, 2 * br:3 * br]
    u = _gelu(y[:, 4 * br:5 * br])
    vn = _rms(_gelu(y[:, 5 * br:6 * br]), vg_ref[...])
    cv_ref[...] = vn
    od = u * (w0_ref[...] * vn + b0_ref[...]) * _silu(y[:, 6 * br:7 * br])
    od_ref[...] = od.astype(od_ref.dtype)


def _odd_decode(y_s, q_g, k_g, v_norm_g, w0_row, b0_row):
    s = y_s.shape[0]
    br = HC * LANES
    args = (y_s, q_g.reshape(1, -1), k_g.reshape(1, -1), v_norm_g.reshape(1, -1), w0_row, b0_row)
    full = lambda a: pl.BlockSpec(a.shape, lambda i: (0,) * a.ndim)
    osd = lambda dt: jax.ShapeDtypeStruct((s, br), dt)
    ospec = pl.BlockSpec((s, br), lambda i: (0, 0))
    return pl.pallas_call(
        _odd_decode_kernel,
        out_shape=(osd(F32), osd(F32), osd(F32), osd(F32), osd(BF16)),
        grid=(1,),
        in_specs=[full(a) for a in args],
        out_specs=(ospec,) * 5,
        compiler_params=_cparams(("arbitrary",)),
        name="odd_decode",
    )(*args)


def _strided_suffix_sums(x, stride, axis):
    n = x.shape[axis]
    idx = lax.broadcasted_iota(jnp.int32, x.shape, axis)
    inc, tot = x, x
    k = stride
    while k < n:
        inc = inc + jnp.where(idx < n - k, pltpu.roll(inc, n - k, axis), 0.0)
        tot = tot + pltpu.roll(tot, n - k, axis)
        k *= 2
    return inc, tot


def _paged_kernel(pt_ref, q_ref, g_ref, off_ref, *refs, n_pp):
    k_refs, v_refs = refs[:n_pp], refs[n_pp:2 * n_pp]
    o_ref, acc_ref, run_ref = refs[2 * n_pp:]
    j = pl.program_id(1)
    page, hc, dc = k_refs[0].shape
    rows = page * hc
    lane = lax.broadcasted_iota(jnp.int32, (hc, rows), 1)
    own = (lane & (hc - 1)) == lax.broadcasted_iota(jnp.int32, (hc, rows), 0)

    @pl.when(j == 0)
    def _():
        acc_ref[...] = jnp.zeros_like(acc_ref)
        run_ref[...] = jnp.zeros_like(run_ref)

    q = (q_ref[...] * (np.float32(dc ** -0.5) * LOG2E)).astype(BF16)
    off = off_ref[...] * LOG2E
    prow = lax.broadcasted_iota(jnp.int32, (n_pp, rows), 0)
    z = jnp.zeros((n_pp, rows), F32)
    for p in range(n_pp):
        kf = k_refs[p][...].reshape(rows, dc).astype(BF16)
        s = _dot_nt(q, kf)
        z = jnp.where(prow == p, jnp.sum(jnp.where(own, s, 0.0), axis=0, keepdims=True), z)
    z = z + off
    incl, tot = _strided_suffix_sums(_neg_log2_rest(z), hc, 1)
    pages_incl, _ = _strided_suffix_sums(tot, 1, 0)
    run = run_ref[...]
    w = jnp.exp2(z - incl - (pages_incl - tot) - run)
    run_ref[...] = run + pages_incl[0:1, :]
    acc = acc_ref[...]
    for p in range(n_pp):
        wm = jnp.where(own, jnp.broadcast_to(w[p:p + 1, :], (hc, rows)), 0.0).astype(BF16)
        acc = acc + _dot(wm, v_refs[p][...].reshape(rows, dc).astype(BF16))
    acc_ref[...] = acc

    @pl.when(j == pl.num_programs(1) - 1)
    def _():
        o_ref[...] = (acc_ref[...] * _silu(g_ref[...])).astype(o_ref.dtype)


def _paged_attention(qn_s, gate_s, cache_k, cache_v, layer, page_table, sb_offset):
    s, hc, dc = qn_s.shape
    n_pages = page_table.shape[1]
    page = cache_k.shape[2]
    n_pp = PAGES_PER_STEP
    assert hc & (hc - 1) == 0 and n_pages % n_pp == 0
    rows = page * hc

    def kv_spec(p):
        return pl.BlockSpec((None, None, page, hc, dc),
                            lambda b, j, pt: (layer, pt[b, n_pages - (j + 1) * n_pp + p], 0, 0, 0))

    head_blk = pl.BlockSpec((None, hc, dc), lambda b, j, pt: (b, 0, 0))
    return pl.pallas_call(
        functools.partial(_paged_kernel, n_pp=n_pp),
        out_shape=jax.ShapeDtypeStruct((s, hc, dc), BF16),
        grid_spec=pltpu.PrefetchScalarGridSpec(
            num_scalar_prefetch=1,
            grid=(s, n_pages // n_pp),
            in_specs=[head_blk, head_blk, pl.BlockSpec((1, rows), lambda b, j, pt: (0, 0))]
            + [kv_spec(p) for p in range(n_pp)] * 2,
            out_specs=head_blk,
            scratch_shapes=[pltpu.VMEM((hc, dc), F32), pltpu.VMEM((1, rows), F32)]),
        compiler_params=_cparams(("arbitrary", "arbitrary")),
        name="paged_attention",
    )(page_table, qn_s, gate_s, jnp.tile(sb_offset, page).reshape(1, rows),
      *([cache_k] * n_pp), *([cache_v] * n_pp))


def _rope_tables(pos, half):
    inv = ROPE_BASE ** (-jnp.arange(half, dtype=F32) / half)
    ang = pos.astype(F32)[:, None] * inv[None, :]
    cos, sin = jnp.cos(ang), jnp.sin(ang)
    return jnp.concatenate([cos, cos], axis=-1), jnp.concatenate([-sin, sin], axis=-1)


def kernel(x_prompt, x_sample, state_gla, state_ret, cache_k, cache_v, page_table, c_prompt, c_sample,
           e_norm_g, e_w_ada, e_b_ada, e_w_in, e_w_a2, e_b_a, e_gla_norm_g, e_ret_norm_g, e_w_out,
           o_norm_g, o_w_ada, o_b_ada, o_w_in, o_q_norm_g, o_k_norm_g, o_sb_offset, o_v_norm_g,
           o_w_s, o_b_s, o_w_out):
    bsz, t, d = x_prompt.shape
    s = x_sample.shape[0]
    br = d // 2
    n_past = page_table.shape[1] * cache_k.shape[2]
    dk = LANES

    c_rows = jnp.concatenate([c_sample, c_prompt, jnp.zeros((16 - s - bsz, d), F32)], axis=0)

    n_qkv = 2 * HA * dk + HA * 2 * dk
    n_main = e_w_in.shape[2] - GLA_RANK
    col_ret = n_qkv + HA * 2 * dk
    cols_dec = (0, HA * dk, 2 * HA * dk, n_qkv,
                col_ret, col_ret + HB * dk, col_ret + 2 * HB * dk, col_ret + 2 * HB * dk + HB * 2 * dk,
                n_main)

    xp = x_prompt.reshape(bsz * t, d)
    xs = x_sample.reshape(s, d)
    pos_p = jnp.arange(t)
    pos_s = n_past + jnp.arange(x_sample.shape[1])
    cos_p, sin_p = _rope_tables(pos_p, dk // 2)
    cos_s, sin_s = _rope_tables(pos_s, dk // 2)
    log_gamma = jnp.log1p(-jnp.exp2(-5.0 - jnp.arange(HB, dtype=F32)))
    lg_row = jnp.repeat(log_gamma, dk).reshape(1, HB * dk)

    i = 0
    mod = _ada(c_rows, e_w_ada[i], e_b_ada[i])
    mod_s, mod_p = mod[:s], mod[s:s + bsz].reshape(bsz, 1, 3 * d)
    w_in_b = _reorder_cast(e_w_in, i, n_qkv, GLA_RANK, n_main + MXU_DIM, 256)
    wa2_pad = jnp.concatenate([e_w_a2[i], jnp.zeros((LANES - GLA_RANK, HA * dk), F32)], axis=0).astype(BF16)
    w_out_b = e_w_out[i].astype(BF16)

    y = _proj(xp, mod_p, e_norm_g[i], w_in_b, t, 1024, 1280)
    y3 = y.reshape(bsz, t, -1)
    oa, gla_p = _gla_prompt(y3, wa2_pad, e_b_a[i], e_gla_norm_g[i], 512)
    ob, ret_p = _ret_prompt(y3, cos_p, sin_p, lg_row, e_ret_norm_g[i], col_ret, 512)
    xp = _outproj(oa.reshape(bsz * t, br), 0, ob.reshape(bsz * t, br), 0, w_out_b, xp, mod_p, t, 512, d)

    y_s = _proj(xs, mod_s, e_norm_g[i], w_in_b, None, s, 1280)
    mixed_s, gla_s, ret_s = _even_decode(y_s, state_gla[i], state_ret[i], wa2_pad, e_b_a[i],
                                         e_gla_norm_g[i], e_ret_norm_g[i], cos_s, sin_s, lg_row, cols_dec)
    xs = _outproj(mixed_s, 0, mixed_s, 1, w_out_b, xs, mod_s, None, s, 1024)

    mod = _ada(c_rows, o_w_ada[i], o_b_ada[i])
    mod_s, mod_p = mod[:s], mod[s:s + bsz].reshape(bsz, 1, 3 * d)
    w_in_b = o_w_in[i].astype(BF16)
    w_out_b = o_w_out[i].astype(BF16)
    bs_full = jnp.repeat(jnp.transpose(o_b_s[i]), LANES, axis=1)

    y = _proj(xp, mod_p, o_norm_g[i], w_in_b, t, 1024, 1024)
    y3 = y.reshape(bsz, t, -1)
    qn, kn, knb, vo, vb = _qkv_prompt(y, o_q_norm_g[i], o_k_norm_g[i], 1024)
    shp = (bsz, t, br)
    oc = _sb_prompt(qn.reshape(shp), knb.reshape(shp), vb.reshape(shp), y3, o_sb_offset[i], 3 * br, 512, 256, 4)
    od = _cmlp_prompt(y3, o_v_norm_g[i], o_w_s[i], bs_full, (4, 5, 6), 256)
    xp = _outproj(oc.reshape(bsz * t, br), 0, od.reshape(bsz * t, br), 0, w_out_b, xp, mod_p, t, 512, d)

    y_s = _proj(xs, mod_s, o_norm_g[i], w_in_b, None, s, 1024)
    w0_row = jnp.repeat(o_w_s[i][:, 0, 0], LANES).reshape(1, br)
    b0_row = jnp.repeat(o_b_s[i][:, 0], LANES).reshape(1, br)
    qn_s, kn_s, vo_s, cv_s, od_s = _odd_decode(y_s, o_q_norm_g[i], o_k_norm_g[i], o_v_norm_g[i], w0_row, b0_row)
    dc = LANES
    oc_s = _paged_attention(qn_s.reshape(s, HC, dc), y_s[:, 3 * br:4 * br].reshape(s, HC, dc),
                            cache_k, cache_v, i, page_table, o_sb_offset[i]).reshape(s, br)
    xs = _outproj(oc_s, 0, od_s, 0, w_out_b, xs, mod_s, None, s, 1024)

    return (xp.reshape(bsz, t, d), xs.reshape(s, 1, d),
            gla_p[None], gla_s[None], ret_p[None], ret_s[None],
            kn.reshape(1, bsz, t, HC, dc), vo.reshape(1, bsz, t, HC, dc),
            kn_s.reshape(1, s, 1, HC, dc), vo_s.reshape(1, s, 1, HC, dc),
            cv_s.reshape(1, s, 1, br))
```

```python
import functools

import jax
import jax.numpy as jnp
import numpy as np
from jax import lax
from jax.experimental import pallas as pl
from jax.experimental.pallas import tpu as pltpu

F32 = jnp.float32
BF16 = jnp.bfloat16

HA = 4
HB = 4
HC = 8
DG = 8
GLA_RANK = 16
GLA_TAU = 16.0
GLA_CHUNK = 64
GLA_SAFE_LOG_DECAY = 40.0
RET_CHUNK = 128
ROPE_BASE = 10000.0
D_CHUNK = 128
RMS_EPS = 1e-6

LANES = 128
MXU_DIM = 256
VMEM_LIMIT = 56 * 1024 * 1024
LOG2E = np.float32(np.log2(np.e))
PAGES_PER_STEP = 8


def _cparams(sem):
    return pltpu.CompilerParams(dimension_semantics=sem, vmem_limit_bytes=VMEM_LIMIT)


def _dot(a, b):
    return jnp.dot(a, b, preferred_element_type=F32)


def _dot_nt(a, b):
    return lax.dot_general(a, b, (((1,), (1,)), ((), ())), preferred_element_type=F32)


def _dot_tn(a, b):
    return lax.dot_general(a, b, (((0,), (0,)), ((), ())), preferred_element_type=F32)


def _sigmoid(x):
    return 1.0 / (1.0 + jnp.exp(-x))


def _silu(x):
    return x * _sigmoid(x)


def _gelu(x):
    c = np.float32(np.sqrt(2.0 / np.pi))
    return 0.5 * x * (1.0 + jnp.tanh(c * (x + 0.044715 * (x * x * x))))


def _softplus_neg_abs(x):
    return jnp.log(1.0 + jnp.exp(-jnp.abs(x)))


def _neg_log2_rest(z2):
    return jnp.maximum(z2, 0.0) + jnp.log(1.0 + jnp.exp2(-jnp.abs(z2))) * LOG2E


def _split_dot(x, ones_bf16, left=False):
    hi = x.astype(BF16)
    lo = (x - hi.astype(F32)).astype(BF16)
    if left:
        return _dot(ones_bf16, hi) + _dot(ones_bf16, lo)
    return _dot(hi, ones_bf16) + _dot(lo, ones_bf16)


def _rms(x, g):
    ms = jnp.mean(x * x, axis=-1, keepdims=True)
    return x * lax.rsqrt(ms + RMS_EPS) * g


def _ada_kernel(c_ref, w_ref, b_ref, o_ref):
    s = _silu(c_ref[...]).astype(BF16)
    o_ref[...] = _dot(s, w_ref[...].astype(BF16)) + b_ref[...]


def _ada(c_rows, w, b):
    r, d = c_rows.shape
    n = w.shape[1]
    tn = 512
    return pl.pallas_call(
        _ada_kernel,
        out_shape=jax.ShapeDtypeStruct((r, n), F32),
        grid=(n // tn,),
        in_specs=[pl.BlockSpec((r, d), lambda j: (0, 0)),
                  pl.BlockSpec((d, tn), lambda j: (0, j)),
                  pl.BlockSpec((1, tn), lambda j: (0, j))],
        out_specs=pl.BlockSpec((r, tn), lambda j: (0, j)),
        compiler_params=_cparams(("arbitrary",)),
        name="ada",
    )(c_rows, w, b.reshape(1, n))


def _reorder_cast_kernel(w_ref, o_ref, *, n_head, n_rank):
    n = w_ref.shape[0]
    tail = n - n_head - n_rank
    o_ref[:n_head, :] = w_ref[:n_head, :].astype(BF16)
    o_ref[n_head:n_head + tail, :] = w_ref[n_head + n_rank:, :].astype(BF16)
    o_ref[n_head + tail:n, :] = w_ref[n_head:n_head + n_rank, :].astype(BF16)
    o_ref[n:, :] = jnp.zeros((o_ref.shape[0] - n, o_ref.shape[1]), BF16)


def _reorder_cast(wt3, layer, n_head, n_rank, n_out, tc):
    _, n, d = wt3.shape
    return pl.pallas_call(
        functools.partial(_reorder_cast_kernel, n_head=n_head, n_rank=n_rank),
        out_shape=jax.ShapeDtypeStruct((n_out, d), BF16),
        grid=(d // tc,),
        in_specs=[pl.BlockSpec((None, n, tc), lambda c: (layer, 0, c))],
        out_specs=pl.BlockSpec((n_out, tc), lambda c: (0, c)),
        compiler_params=_cparams(("arbitrary",)),
        name="reorder_cast",
    )(wt3)


def _proj_kernel(x_ref, shift_ref, scale_ref, g_ref, w_ref, o_ref, h_ref, *, w_transposed):
    @pl.when(pl.program_id(1) == 0)
    def _():
        h = _rms(x_ref[...], g_ref[...]) * (1.0 + scale_ref[...]) + shift_ref[...]
        h_ref[...] = h.astype(BF16)

    mm = _dot_nt if w_transposed else _dot
    o_ref[...] = mm(h_ref[...], w_ref[...]).astype(o_ref.dtype)


def _proj(x2d, mod, norm_g, w_bf16, rows_per_batch, tm, tn, w_transposed=False):
    m, d = x2d.shape
    out_dtype = F32 if rows_per_batch is None else BF16
    n = w_bf16.shape[0 if w_transposed else 1]
    w_spec = (pl.BlockSpec((tn, d), lambda i, j: (j, 0)) if w_transposed
              else pl.BlockSpec((d, tn), lambda i, j: (0, j)))
    if rows_per_batch is None:
        shift_spec = pl.BlockSpec((tm, d), lambda i, j: (i, 0))
        scale_spec = pl.BlockSpec((tm, d), lambda i, j: (i, 1))
    else:
        tpb = rows_per_batch // tm
        shift_spec = pl.BlockSpec((None, 1, d), lambda i, j: (i // tpb, 0, 0))
        scale_spec = pl.BlockSpec((None, 1, d), lambda i, j: (i // tpb, 0, 1))
    return pl.pallas_call(
        functools.partial(_proj_kernel, w_transposed=w_transposed),
        out_shape=jax.ShapeDtypeStruct((m, n), out_dtype),
        grid=(m // tm, n // tn),
        in_specs=[pl.BlockSpec((tm, d), lambda i, j: (i, 0)),
                  shift_spec, scale_spec,
                  pl.BlockSpec((1, d), lambda i, j: (0, 0)),
                  w_spec],
        out_specs=pl.BlockSpec((tm, tn), lambda i, j: (i, j)),
        scratch_shapes=[pltpu.VMEM((tm, d), BF16)],
        compiler_params=_cparams(("arbitrary", "arbitrary")),
        name="proj",
    )(x2d, mod, mod, norm_g.reshape(1, d), w_bf16)


def _outproj_kernel(ma_ref, mb_ref, wa_ref, wb_ref, x_ref, gate_ref, o_ref):
    mixed = _dot(ma_ref[...], wa_ref[...]) + _dot(mb_ref[...], wb_ref[...])
    o_ref[...] = x_ref[...] + gate_ref[...] * mixed


def _outproj(ma, ca, mb, cb, w_bf16, x2d, mod, rows_per_batch, tm, tn):
    m, d = x2d.shape
    half = w_bf16.shape[0] // 2
    gcol = 2 * (d // tn)
    if rows_per_batch is None:
        gate_spec = pl.BlockSpec((tm, tn), lambda i, j: (i, gcol + j))
    else:
        tpb = rows_per_batch // tm
        gate_spec = pl.BlockSpec((None, 1, tn), lambda i, j: (i // tpb, 0, gcol + j))
    return pl.pallas_call(
        _outproj_kernel,
        out_shape=jax.ShapeDtypeStruct((m, d), F32),
        grid=(m // tm, d // tn),
        in_specs=[pl.BlockSpec((tm, half), lambda i, j: (i, ca)),
                  pl.BlockSpec((tm, half), lambda i, j: (i, cb)),
                  pl.BlockSpec((half, tn), lambda i, j: (0, j)),
                  pl.BlockSpec((half, tn), lambda i, j: (1, j)),
                  pl.BlockSpec((tm, tn), lambda i, j: (i, j)),
                  gate_spec],
        out_specs=pl.BlockSpec((tm, tn), lambda i, j: (i, j)),
        compiler_params=_cparams(("arbitrary", "arbitrary")),
        name="outproj",
    )(ma, mb, w_bf16, w_bf16, x2d, mod)


def _gla_kernel(q_ref, k_ref, v_ref, ra_ref, g_ref, wa2_ref, ba_ref, ng_ref,
                o_ref, st_ref, stt_ref, la_ref, b_ref, *, n_sub, dk, dv):
    c = pl.program_id(1)
    cs = GLA_CHUNK

    @pl.when(c == 0)
    def _():
        stt_ref[...] = jnp.zeros_like(stt_ref)

    row = lax.broadcasted_iota(jnp.int32, (cs, cs), 0)
    col = lax.broadcasted_iota(jnp.int32, (cs, cs), 1)
    causal = row >= col
    tri = jnp.where(causal, 1.0, 0.0).astype(BF16)
    qscale = np.float32(dk ** -0.5)

    pre = _dot(ra_ref[...].astype(BF16), wa2_ref[...]) + ba_ref[...]
    la = (jnp.minimum(pre, 0.0) - _softplus_neg_abs(pre)) * np.float32(1.0 / GLA_TAU)
    la_ref[...] = la
    safe = jnp.min(la) >= np.float32(-GLA_SAFE_LOG_DECAY / cs)

    def pairwise_scores(qs, kb, hs):
        b = b_ref[:, hs]
        sublanes = 8

        def key_group(gi, acc):
            s0 = pl.multiple_of(gi * sublanes, sublanes)
            bk = b_ref[pl.ds(s0, sublanes), hs]
            for r in range(sublanes):
                e = jnp.exp(jnp.minimum(b - bk[r:r + 1, :], 0.0))
                acc = jnp.where(col == s0 + r, _dot_nt((qs * e).astype(BF16), kb), acc)
            return acc

        return lax.fori_loop(0, cs // sublanes, key_group, jnp.zeros((cs, cs), F32))

    def make_body(factorised):
        def body(i, carry):
            r0 = pl.multiple_of(i * cs, cs)
            rows = pl.ds(r0, cs)
            b = _split_dot(la_ref[rows, :], tri, left=True)
            bl = b[cs - 1:cs, :]
            k = k_ref[rows, :].astype(F32)
            qs = q_ref[rows, :].astype(F32) * qscale
            qe = (qs * jnp.exp(b)).astype(BF16)
            kd = (k * jnp.exp(bl - b)).astype(BF16)
            ebl = jnp.exp(bl)
            if factorised:
                ke = (k * jnp.exp(-b)).astype(BF16)
            else:
                b_ref[...] = b
            for h in range(HA):
                ks, vs = slice(h * dk, (h + 1) * dk), slice(h * dv, (h + 1) * dv)
                v = v_ref[rows, vs].astype(BF16)
                if factorised:
                    s = _dot_nt(qe[:, ks], ke[:, ks])
                else:
                    s = pairwise_scores(qs[:, ks], k[:, ks].astype(BF16), ks)
                s = jnp.where(causal, s, 0.0)
                stt = stt_ref[h]
                o = _dot(s.astype(BF16), v) + _dot_nt(qe[:, ks], stt.astype(BF16))
                stt_ref[h] = stt * ebl[:, ks] + _dot_tn(v, kd[:, ks])
                og = _rms(o, ng_ref[:, vs]) * _silu(g_ref[rows, vs].astype(F32))
                o_ref[rows, vs] = og.astype(o_ref.dtype)
            return carry
        return body

    @pl.when(safe)
    def _():
        lax.fori_loop(0, n_sub, make_body(True), 0)

    @pl.when(jnp.logical_not(safe))
    def _():
        lax.fori_loop(0, n_sub, make_body(False), 0)

    @pl.when(c == pl.num_programs(1) - 1)
    def _():
        for h in range(HA):
            st_ref[h] = stt_ref[h].T


def _gla_prompt(y3, wa2_pad, b_a, norm_g, tc):
    bsz, t, n = y3.shape
    dk, dv = LANES, 2 * LANES
    wk, wv = HA * dk, HA * dv
    kern = functools.partial(_gla_kernel, n_sub=tc // GLA_CHUNK, dk=dk, dv=dv)
    return pl.pallas_call(
        kern,
        out_shape=(jax.ShapeDtypeStruct((bsz, t, wv), BF16),
                   jax.ShapeDtypeStruct((bsz, HA, dk, dv), F32)),
        grid=(bsz, t // tc),
        in_specs=[pl.BlockSpec((None, tc, wk), lambda b, c: (b, c, 0)),
                  pl.BlockSpec((None, tc, wk), lambda b, c: (b, c, 1)),
                  pl.BlockSpec((None, tc, wv), lambda b, c: (b, c, 1)),
                  pl.BlockSpec((None, tc, LANES), lambda b, c: (b, c, (n - MXU_DIM) // LANES)),
                  pl.BlockSpec((None, tc, wv), lambda b, c: (b, c, 2)),
                  pl.BlockSpec((LANES, wk), lambda b, c: (0, 0)),
                  pl.BlockSpec((1, wk), lambda b, c: (0, 0)),
                  pl.BlockSpec((1, wv), lambda b, c: (0, 0))],
        out_specs=(pl.BlockSpec((None, tc, wv), lambda b, c: (b, c, 0)),
                   pl.BlockSpec((None, HA, dk, dv), lambda b, c: (b, 0, 0, 0))),
        scratch_shapes=[pltpu.VMEM((HA, dv, dk), F32), pltpu.VMEM((tc, wk), F32),
                        pltpu.VMEM((GLA_CHUNK, wk), F32)],
        compiler_params=_cparams(("arbitrary", "arbitrary")),
        name="gla_prompt",
    )(y3, y3, y3, y3, y3, wa2_pad, b_a.reshape(1, -1), norm_g.reshape(1, -1))


def _rot(x, cosf, sinf):
    return x * cosf + pltpu.roll(x, x.shape[-1] // 2, 1) * sinf


def _ret_kernel(q_ref, k_ref, v_ref, g_ref, cos_ref, sin_ref, lg_ref, ng_ref,
                o_ref, st_ref, stt_ref, *, n_sub, dk, dv):
    c = pl.program_id(1)
    cs = RET_CHUNK

    @pl.when(c == 0)
    def _():
        stt_ref[...] = jnp.zeros_like(stt_ref)

    row = lax.broadcasted_iota(jnp.int32, (cs, cs), 0)
    col = lax.broadcasted_iota(jnp.int32, (cs, cs), 1)
    rel = (row - col).astype(F32)
    ridx = lax.broadcasted_iota(jnp.int32, (cs, 1), 0).astype(F32)
    lgs = [lg_ref[:, h * dk:h * dk + 1] for h in range(HB)]
    decay = [jnp.where(row >= col, jnp.exp(rel * lg), 0.0) for lg in lgs]
    inter = [jnp.exp((ridx + 1.0) * lg) for lg in lgs]
    kdec = [jnp.exp((np.float32(cs - 1.0) - ridx) * lg) for lg in lgs]
    sdec = [jnp.exp(np.float32(cs) * lg) for lg in lgs]
    qscale = np.float32(dk ** -0.5)

    def body(i, carry):
        r0 = pl.multiple_of(i * cs, cs)
        rows = pl.ds(r0, cs)
        cosf = cos_ref[rows, :]
        sinf = sin_ref[rows, :]
        for h in range(HB):
            ks, vs = slice(h * dk, (h + 1) * dk), slice(h * dv, (h + 1) * dv)
            qr = (_rot(q_ref[rows, ks].astype(F32), cosf, sinf) * qscale).astype(BF16)
            kr = _rot(k_ref[rows, ks].astype(F32), cosf, sinf)
            v = v_ref[rows, vs].astype(BF16)
            s = _dot_nt(qr, kr.astype(BF16)) * decay[h]
            stt = stt_ref[h]
            o = _dot(s.astype(BF16), v) + _dot_nt(qr, stt.astype(BF16)) * inter[h]
            stt_ref[h] = stt * sdec[h] + _dot_tn(v, (kr * kdec[h]).astype(BF16))
            og = _rms(o, ng_ref[:, vs]) * _silu(g_ref[rows, vs].astype(F32))
            o_ref[rows, vs] = og.astype(o_ref.dtype)
        return carry

    lax.fori_loop(0, n_sub, body, 0)

    @pl.when(c == pl.num_programs(1) - 1)
    def _():
        for h in range(HB):
            st_ref[h] = stt_ref[h].T


def _ret_prompt(y3, cosf, sinf, lg_row, norm_g, col0, tc):
    bsz, t, _ = y3.shape
    dk, dv = LANES, 2 * LANES
    wk, wv = HB * dk, HB * dv
    cq, cv = col0 // wk, (col0 + 2 * wk) // wv
    kern = functools.partial(_ret_kernel, n_sub=tc // RET_CHUNK, dk=dk, dv=dv)
    return pl.pallas_call(
        kern,
        out_shape=(jax.ShapeDtypeStruct((bsz, t, wv), BF16),
                   jax.ShapeDtypeStruct((bsz, HB, dk, dv), F32)),
        grid=(bsz, t // tc),
        in_specs=[pl.BlockSpec((None, tc, wk), lambda b, c: (b, c, cq)),
                  pl.BlockSpec((None, tc, wk), lambda b, c: (b, c, cq + 1)),
                  pl.BlockSpec((None, tc, wv), lambda b, c: (b, c, cv)),
                  pl.BlockSpec((None, tc, wv), lambda b, c: (b, c, cv + 1)),
                  pl.BlockSpec((tc, dk), lambda b, c: (c, 0)),
                  pl.BlockSpec((tc, dk), lambda b, c: (c, 0)),
                  pl.BlockSpec((1, wk), lambda b, c: (0, 0)),
                  pl.BlockSpec((1, wv), lambda b, c: (0, 0))],
        out_specs=(pl.BlockSpec((None, tc, wv), lambda b, c: (b, c, 0)),
                   pl.BlockSpec((None, HB, dk, dv), lambda b, c: (b, 0, 0, 0))),
        scratch_shapes=[pltpu.VMEM((HB, dv, dk), F32)],
        compiler_params=_cparams(("arbitrary", "arbitrary")),
        name="ret_prompt",
    )(y3, y3, y3, y3, cosf, sinf, lg_row, norm_g.reshape(1, -1))


def _row_to_col(x_row):
    n = x_row.shape[-1]
    r = lax.broadcasted_iota(jnp.int32, (n, n), 0)
    c = lax.broadcasted_iota(jnp.int32, (n, n), 1)
    return jnp.sum(jnp.where(r == c, jnp.broadcast_to(x_row, (n, n)), 0.0), axis=1, keepdims=True)


def _even_decode_kernel(y_ref, sg_ref, sr_ref, wa2_ref, ba_ref, gng_ref, rng_ref, cos_ref, sin_ref,
                        lg_ref, o_ref, sgo_ref, sro_ref, *, cols):
    dk, dv = LANES, 2 * LANES
    cqa, cka, cva, cga, cqb, ckb, cvb, cgb, cra = cols
    y = y_ref[...]
    pre = _dot(y[:, cra:cra + LANES].astype(BF16), wa2_ref[...]) + ba_ref[...]
    la = (jnp.minimum(pre, 0.0) - _softplus_neg_abs(pre)) * np.float32(1.0 / GLA_TAU)
    alpha = jnp.exp(la)
    cosf = cos_ref[...]
    sinf = sin_ref[...]
    gam = jnp.exp(lg_ref[...])
    outs = []
    for h in range(HA):
        q = y[:, cqa + h * dk:cqa + (h + 1) * dk] * np.float32(dk ** -0.5)
        k = y[:, cka + h * dk:cka + (h + 1) * dk]
        v = y[:, cva + h * dv:cva + (h + 1) * dv]
        s_new = _row_to_col(alpha[:, h * dk:(h + 1) * dk]) * sg_ref[h] + _row_to_col(k) * v
        sgo_ref[h] = s_new
        o = jnp.sum(_row_to_col(q) * s_new, axis=0, keepdims=True)
        g = y[:, cga + h * dv:cga + (h + 1) * dv]
        outs.append(_rms(o, gng_ref[:, h * dv:(h + 1) * dv]) * _silu(g))
    for h in range(HB):
        q = _rot(y[:, cqb + h * dk:cqb + (h + 1) * dk], cosf, sinf) * np.float32(dk ** -0.5)
        k = _rot(y[:, ckb + h * dk:ckb + (h + 1) * dk], cosf, sinf)
        v = y[:, cvb + h * dv:cvb + (h + 1) * dv]
        s_new = _row_to_col(gam[:, h * dk:(h + 1) * dk]) * sr_ref[h] + _row_to_col(k) * v
        sro_ref[h] = s_new
        o = jnp.sum(_row_to_col(q) * s_new, axis=0, keepdims=True)
        g = y[:, cgb + h * dv:cgb + (h + 1) * dv]
        outs.append(_rms(o, rng_ref[:, h * dv:(h + 1) * dv]) * _silu(g))
    o_ref[...] = jnp.concatenate(outs, axis=-1).astype(o_ref.dtype)


def _even_decode(y_s, state_gla, state_ret, wa2_pad, b_a, gla_ng, ret_ng, cos_row, sin_row, lg_row, cols):
    bsz, n = y_s.shape
    dk, dv = LANES, 2 * LANES
    wide = (HA + HB) * dv
    kern = functools.partial(_even_decode_kernel, cols=cols)
    full = lambda a: pl.BlockSpec(a.shape, lambda b: (0,) * a.ndim)
    b_a2, gng, rng_ = b_a.reshape(1, -1), gla_ng.reshape(1, -1), ret_ng.reshape(1, -1)
    mixed, sg, sr = pl.pallas_call(
        kern,
        out_shape=(jax.ShapeDtypeStruct((bsz, 1, wide), BF16),
                   jax.ShapeDtypeStruct(state_gla.shape, F32),
                   jax.ShapeDtypeStruct(state_ret.shape, F32)),
        grid=(bsz,),
        in_specs=[pl.BlockSpec((None, 1, n), lambda b: (b, 0, 0)),
                  pl.BlockSpec((None, HA, dk, dv), lambda b: (b, 0, 0, 0)),
                  pl.BlockSpec((None, HB, dk, dv), lambda b: (b, 0, 0, 0)),
                  full(wa2_pad), full(b_a2), full(gng), full(rng_), full(cos_row), full(sin_row), full(lg_row)],
        out_specs=(pl.BlockSpec((None, 1, wide), lambda b: (b, 0, 0)),
                   pl.BlockSpec((None, HA, dk, dv), lambda b: (b, 0, 0, 0)),
                   pl.BlockSpec((None, HB, dk, dv), lambda b: (b, 0, 0, 0))),
        compiler_params=_cparams(("arbitrary",)),
        name="even_decode",
    )(y_s.reshape(bsz, 1, n), state_gla, state_ret, wa2_pad, b_a2, gng, rng_, cos_row, sin_row, lg_row)
    return mixed.reshape(bsz, wide), sg, sr


def _qkv_kernel(q_ref, k_ref, v_ref, qg_ref, kg_ref, qn_ref, kn_ref, knb_ref, vo_ref):
    dc = qg_ref.shape[-1]
    qscale = np.float32(dc ** -0.5) * LOG2E
    for h in range(q_ref.shape[-1] // dc):
        hs = slice(h * dc, (h + 1) * dc)
        qn_ref[:, hs] = (_rms(q_ref[:, hs].astype(F32), qg_ref[...]) * qscale).astype(BF16)
        kn = _rms(k_ref[:, hs].astype(F32), kg_ref[...])
        kn_ref[:, hs] = kn
        knb_ref[:, hs] = kn.astype(BF16)
    vo_ref[...] = v_ref[...].astype(F32)


def _qkv_prompt(y2, q_g, k_g, tm):
    m = y2.shape[0]
    dc = LANES
    wide = HC * dc
    blk = lambda c: pl.BlockSpec((tm, wide), lambda i: (i, c))
    gspec = pl.BlockSpec((1, dc), lambda i: (0, 0))
    return pl.pallas_call(
        _qkv_kernel,
        out_shape=(jax.ShapeDtypeStruct((m, wide), BF16),
                   jax.ShapeDtypeStruct((m, wide), F32),
                   jax.ShapeDtypeStruct((m, wide), BF16),
                   jax.ShapeDtypeStruct((m, wide), F32)),
        grid=(m // tm,),
        in_specs=[blk(0), blk(1), blk(2), gspec, gspec],
        out_specs=(blk(0), blk(0), blk(0), blk(0)),
        compiler_params=_cparams(("arbitrary",)),
        name="qkv_prompt",
    )(y2, y2, y2, q_g.reshape(1, dc), k_g.reshape(1, dc))


def _sb_kernel(qi_ref, kb_ref, q_ref, k_ref, v_ref, g_ref, qone_ref, koff_ref, o_ref, acc_ref, run_ref,
               *, sub, hp):
    p = pl.program_id(2)
    qi = qi_ref[p]
    kb = kb_ref[p]
    tq = q_ref.shape[0]
    dc = q_ref.shape[1] // hp

    @pl.when(kb == qi)
    def _():
        acc_ref[...] = jnp.zeros_like(acc_ref)
        run_ref[...] = jnp.zeros_like(run_ref)

    def block(masked):
        r = lax.broadcasted_iota(jnp.int32, (sub, sub), 0)
        c = lax.broadcasted_iota(jnp.int32, (sub, sub), 1)
        upper = jnp.where(r >= c, 1.0, 0.0).astype(BF16)
        units = [(hh, s_i, s_i * sub if masked else 0) for hh in range(hp) for s_i in reversed(range(tq // sub))]
        zs, masks, splits, incls, ws = [], [], [], [], []
        for hh, s_i, r0 in units:
            hs = slice(hh * dc, (hh + 1) * dc)
            qa = jnp.concatenate([q_ref[r0:, hs], qone_ref[r0:, :]], axis=1)
            ka = jnp.concatenate([k_ref[s_i * sub:(s_i + 1) * sub, hs], koff_ref[hh]], axis=1)
            zs.append(_dot_nt(qa, ka))
        for z in zs:
            rest = _neg_log2_rest(z)
            if masked:
                mask = (lax.broadcasted_iota(jnp.int32, z.shape, 1)
                        < lax.broadcasted_iota(jnp.int32, z.shape, 0))
                rest = jnp.where(mask, rest, 0.0)
                masks.append(mask)
            hi = rest.astype(BF16)
            splits.append(jnp.concatenate([hi, (rest - hi.astype(F32)).astype(BF16)], axis=1))
        upper2 = jnp.concatenate([upper, upper], axis=0)
        for hl in splits:
            incls.append(_dot(hl, upper2))
        for u in range(len(units)):
            w = jnp.exp2(zs[u] - incls[u])
            if masked:
                w = jnp.where(masks[u], w, 0.0)
            ws.append(w.astype(BF16))
        for u, (hh, s_i, r0) in enumerate(units):
            hs = slice(hh * dc, (hh + 1) * dc)
            run = run_ref[hh, r0:, :]
            acc_ref[r0:, hs] += jnp.exp2(-run) * _dot(ws[u], v_ref[s_i * sub:(s_i + 1) * sub, hs])
            run_ref[hh, r0:, :] = run + incls[u][:, 0:1]

    pl.when(kb == qi)(functools.partial(block, True))
    pl.when(kb < qi)(functools.partial(block, False))

    @pl.when(kb == 0)
    def _():
        o_ref[...] = (acc_ref[...] * _silu(g_ref[...].astype(F32))).astype(o_ref.dtype)


def _sb_prompt(qn, kn, y3, sb_offset, cv, cg, tq, sub, hp):
    bsz, t, _ = qn.shape
    dc = LANES
    wide = hp * dc
    nq = t // tq
    pairs = [(qi, kb) for qi in range(nq) for kb in range(qi, -1, -1)]
    qi_tab = jnp.asarray([p[0] for p in pairs], jnp.int32)
    kb_tab = jnp.asarray([p[1] for p in pairs], jnp.int32)
    off2 = sb_offset.astype(F32) * LOG2E
    terms, rem = [], off2
    for _ in range(3):
        terms.append(rem.astype(BF16))
        rem = rem - terms[-1].astype(F32)
    lane = jnp.arange(dc)
    koff = sum(jnp.where(lane == n, tm[:, None, None], 0).astype(BF16) for n, tm in enumerate(terms))
    koff = jnp.broadcast_to(koff, (HC, sub, dc))
    qone = jnp.broadcast_to(jnp.where(lane < len(terms), 1, 0).astype(BF16), (tq, dc))
    kern = functools.partial(_sb_kernel, sub=sub, hp=hp)
    q_map = lambda b, h, p, qi, kb: (b, qi[p], h)
    k_map = lambda b, h, p, qi, kb: (b, kb[p], h)
    return pl.pallas_call(
        kern,
        out_shape=jax.ShapeDtypeStruct((bsz, t, HC * dc), BF16),
        grid_spec=pltpu.PrefetchScalarGridSpec(
            num_scalar_prefetch=2,
            grid=(bsz, HC // hp, len(pairs)),
            in_specs=[pl.BlockSpec((None, tq, wide), q_map),
                      pl.BlockSpec((None, tq, wide), k_map),
                      pl.BlockSpec((None, tq, wide), lambda b, h, p, qi, kb: (b, kb[p], cv // wide + h)),
                      pl.BlockSpec((None, tq, wide), lambda b, h, p, qi, kb: (b, qi[p], cg // wide + h)),
                      pl.BlockSpec((tq, dc), lambda b, h, p, qi, kb: (0, 0)),
                      pl.BlockSpec((hp, sub, dc), lambda b, h, p, qi, kb: (h, 0, 0))],
            out_specs=pl.BlockSpec((None, tq, wide), q_map),
            scratch_shapes=[pltpu.VMEM((tq, wide), F32), pltpu.VMEM((hp, tq, 1), F32)]),
        compiler_params=_cparams(("arbitrary",) * 3),
        name="sb_prompt",
    )(qi_tab, kb_tab, qn, kn, y3, y3, qone, koff)


def _cmlp_kernel(u_ref, v_ref, g_ref, vg_ref, ws_ref, bs_ref, o_ref, *, n_chunk):
    cs = D_CHUNK
    u = _gelu(u_ref[...].astype(F32))
    vn = _rms(_gelu(v_ref[...].astype(F32)), vg_ref[...]).astype(BF16)
    gate = _silu(g_ref[...].astype(F32))
    bs = bs_ref[...]
    r = lax.broadcasted_iota(jnp.int32, (cs, cs), 0)
    c = lax.broadcasted_iota(jnp.int32, (cs, cs), 1)
    for g in range(DG):
        w = jnp.where(r >= c, ws_ref[g], 0.0).astype(BF16)
        lo, hi = g * LANES, (g + 1) * LANES
        for ch in range(n_chunk):
            r0, r1 = ch * cs, (ch + 1) * cs
            mixed = _dot(w, vn[r0:r1, lo:hi]) + bs[:, lo:hi]
            o_ref[r0:r1, lo:hi] = (u[r0:r1, lo:hi] * mixed * gate[r0:r1, lo:hi]).astype(o_ref.dtype)


def _cmlp_prompt(y3, v_norm_g, w_s, bs_full, cols, tm):
    bsz, t, _ = y3.shape
    br = DG * LANES
    cu, cv, cg = cols
    kern = functools.partial(_cmlp_kernel, n_chunk=tm // D_CHUNK)
    return pl.pallas_call(
        kern,
        out_shape=jax.ShapeDtypeStruct((bsz, t, br), BF16),
        grid=(bsz, t // tm),
        in_specs=[pl.BlockSpec((None, tm, br), lambda b, i: (b, i, cu)),
                  pl.BlockSpec((None, tm, br), lambda b, i: (b, i, cv)),
                  pl.BlockSpec((None, tm, br), lambda b, i: (b, i, cg)),
                  pl.BlockSpec((1, br), lambda b, i: (0, 0)),
                  pl.BlockSpec((DG, D_CHUNK, D_CHUNK), lambda b, i: (0, 0, 0)),
                  pl.BlockSpec((D_CHUNK, br), lambda b, i: (0, 0))],
        out_specs=pl.BlockSpec((None, tm, br), lambda b, i: (b, i, 0)),
        compiler_params=_cparams(("arbitrary", "arbitrary")),
        name="cmlp_prompt",
    )(y3, y3, y3, v_norm_g.reshape(1, br), w_s, bs_full)


def _odd_decode_kernel(y_ref, qg_ref, kg_ref, vg_ref, w0_ref, b0_ref, qn_ref, kn_ref, vo_ref, cv_ref, od_ref):
    br = HC * LANES
    y = y_ref[...]
    for h in range(HC):
        lo, hi = h * LANES, (h + 1) * LANES
        qn_ref[:, lo:hi] = _rms(y[:, lo:hi], qg_ref[...])
        kn_ref[:, lo:hi] = _rms(y[:, br + lo:br + hi], kg_ref[...])
    vo_ref[...] = y[:, 2 * br:3 * br]
    u = _gelu(y[:, 4 * br:5 * br])
    vn = _rms(_gelu(y[:, 5 * br:6 * br]), vg_ref[...])
    cv_ref[...] = vn
    od = u * (w0_ref[...] * vn + b0_ref[...]) * _silu(y[:, 6 * br:7 * br])
    od_ref[...] = od.astype(od_ref.dtype)


def _odd_decode(y_s, q_g, k_g, v_norm_g, w0_row, b0_row):
    s = y_s.shape[0]
    br = HC * LANES
    args = (y_s, q_g.reshape(1, -1), k_g.reshape(1, -1), v_norm_g.reshape(1, -1), w0_row, b0_row)
    full = lambda a: pl.BlockSpec(a.shape, lambda i: (0,) * a.ndim)
    osd = lambda dt: jax.ShapeDtypeStruct((s, br), dt)
    ospec = pl.BlockSpec((s, br), lambda i: (0, 0))
    return pl.pallas_call(
        _odd_decode_kernel,
        out_shape=(osd(F32), osd(F32), osd(F32), osd(F32), osd(BF16)),
        grid=(1,),
        in_specs=[full(a) for a in args],
        out_specs=(ospec,) * 5,
        compiler_params=_cparams(("arbitrary",)),
        name="odd_decode",
    )(*args)


def _strided_suffix_sums(x, stride, axis):
    n = x.shape[axis]
    idx = lax.broadcasted_iota(jnp.int32, x.shape, axis)
    inc, tot = x, x
    k = stride
    while k < n:
        inc = inc + jnp.where(idx < n - k, pltpu.roll(inc, n - k, axis), 0.0)
        tot = tot + pltpu.roll(tot, n - k, axis)
        k *= 2
    return inc, tot


def _paged_kernel(pt_ref, q_ref, g_ref, off_ref, *refs, n_pp):
    k_refs, v_refs = refs[:n_pp], refs[n_pp:2 * n_pp]
    o_ref, acc_ref, run_ref = refs[2 * n_pp:]
    j = pl.program_id(1)
    page, hc, dc = k_refs[0].shape
    rows = page * hc
    lane = lax.broadcasted_iota(jnp.int32, (hc, rows), 1)
    own = (lane & (hc - 1)) == lax.broadcasted_iota(jnp.int32, (hc, rows), 0)

    @pl.when(j == 0)
    def _():
        acc_ref[...] = jnp.zeros_like(acc_ref)
        run_ref[...] = jnp.zeros_like(run_ref)

    q = (q_ref[...] * (np.float32(dc ** -0.5) * LOG2E)).astype(BF16)
    off = off_ref[...] * LOG2E
    prow = lax.broadcasted_iota(jnp.int32, (n_pp, rows), 0)
    z = jnp.zeros((n_pp, rows), F32)
    for p in range(n_pp):
        kf = k_refs[p][...].reshape(rows, dc).astype(BF16)
        s = _dot_nt(q, kf)
        z = jnp.where(prow == p, jnp.sum(jnp.where(own, s, 0.0), axis=0, keepdims=True), z)
    z = z + off
    incl, tot = _strided_suffix_sums(_neg_log2_rest(z), hc, 1)
    pages_incl, _ = _strided_suffix_sums(tot, 1, 0)
    run = run_ref[...]
    w = jnp.exp2(z - incl - (pages_incl - tot) - run)
    run_ref[...] = run + pages_incl[0:1, :]
    acc = acc_ref[...]
    for p in range(n_pp):
        wm = jnp.where(own, jnp.broadcast_to(w[p:p + 1, :], (hc, rows)), 0.0).astype(BF16)
        acc = acc + _dot(wm, v_refs[p][...].reshape(rows, dc).astype(BF16))
    acc_ref[...] = acc

    @pl.when(j == pl.num_programs(1) - 1)
    def _():
        o_ref[...] = (acc_ref[...] * _silu(g_ref[...])).astype(o_ref.dtype)


def _paged_attention(qn_s, gate_s, cache_k, cache_v, layer, page_table, sb_offset):
    s, hc, dc = qn_s.shape
    n_pages = page_table.shape[1]
    page = cache_k.shape[2]
    n_pp = PAGES_PER_STEP
    assert hc & (hc - 1) == 0 and n_pages % n_pp == 0
    rows = page * hc

    def kv_spec(p):
        return pl.BlockSpec((None, None, page, hc, dc),
                            lambda b, j, pt: (layer, pt[b, n_pages - (j + 1) * n_pp + p], 0, 0, 0))

    head_blk = pl.BlockSpec((None, hc, dc), lambda b, j, pt: (b, 0, 0))
    return pl.pallas_call(
        functools.partial(_paged_kernel, n_pp=n_pp),
        out_shape=jax.ShapeDtypeStruct((s, hc, dc), BF16),
        grid_spec=pltpu.PrefetchScalarGridSpec(
            num_scalar_prefetch=1,
            grid=(s, n_pages // n_pp),
            in_specs=[head_blk, head_blk, pl.BlockSpec((1, rows), lambda b, j, pt: (0, 0))]
            + [kv_spec(p) for p in range(n_pp)] * 2,
            out_specs=head_blk,
            scratch_shapes=[pltpu.VMEM((hc, dc), F32), pltpu.VMEM((1, rows), F32)]),
        compiler_params=_cparams(("arbitrary", "arbitrary")),
        name="paged_attention",
    )(page_table, qn_s, gate_s, jnp.tile(sb_offset, page).reshape(1, rows),
      *([cache_k] * n_pp), *([cache_v] * n_pp))


def _rope_tables(pos, half):
    inv = ROPE_BASE ** (-jnp.arange(half, dtype=F32) / half)
    ang = pos.astype(F32)[:, None] * inv[None, :]
    cos, sin = jnp.cos(ang), jnp.sin(ang)
    return jnp.concatenate([cos, cos], axis=-1), jnp.concatenate([-sin, sin], axis=-1)


def kernel(x_prompt, x_sample, state_gla, state_ret, cache_k, cache_v, page_table, c_prompt, c_sample,
           e_norm_g, e_w_ada, e_b_ada, e_w_in, e_w_a2, e_b_a, e_gla_norm_g, e_ret_norm_g, e_w_out,
           o_norm_g, o_w_ada, o_b_ada, o_w_in, o_q_norm_g, o_k_norm_g, o_sb_offset, o_v_norm_g,
           o_w_s, o_b_s, o_w_out):
    bsz, t, d = x_prompt.shape
    s = x_sample.shape[0]
    br = d // 2
    n_past = page_table.shape[1] * cache_k.shape[2]
    dk = LANES

    c_rows = jnp.concatenate([c_sample, c_prompt, jnp.zeros((16 - s - bsz, d), F32)], axis=0)

    n_qkv = 2 * HA * dk + HA * 2 * dk
    n_main = e_w_in.shape[2] - GLA_RANK
    col_ret = n_qkv + HA * 2 * dk
    cols_dec = (0, HA * dk, 2 * HA * dk, n_qkv,
                col_ret, col_ret + HB * dk, col_ret + 2 * HB * dk, col_ret + 2 * HB * dk + HB * 2 * dk,
                n_main)

    xp = x_prompt.reshape(bsz * t, d)
    xs = x_sample.reshape(s, d)
    pos_p = jnp.arange(t)
    pos_s = n_past + jnp.arange(x_sample.shape[1])
    cos_p, sin_p = _rope_tables(pos_p, dk // 2)
    cos_s, sin_s = _rope_tables(pos_s, dk // 2)
    log_gamma = jnp.log1p(-jnp.exp2(-5.0 - jnp.arange(HB, dtype=F32)))
    lg_row = jnp.repeat(log_gamma, dk).reshape(1, HB * dk)

    i = 0
    mod = _ada(c_rows, e_w_ada[i], e_b_ada[i])
    mod_s, mod_p = mod[:s], mod[s:s + bsz].reshape(bsz, 1, 3 * d)
    w_in_b = _reorder_cast(jnp.swapaxes(e_w_in, 1, 2), i, n_qkv, GLA_RANK, n_main + MXU_DIM, 256)
    wa2_pad = jnp.concatenate([e_w_a2[i], jnp.zeros((LANES - GLA_RANK, HA * dk), F32)], axis=0).astype(BF16)
    w_out_b = e_w_out[i].astype(BF16)

    y = _proj(xp, mod_p, e_norm_g[i], w_in_b, t, 1024, 1280, w_transposed=True)
    y3 = y.reshape(bsz, t, -1)
    oa, gla_p = _gla_prompt(y3, wa2_pad, e_b_a[i], e_gla_norm_g[i], 512)
    ob, ret_p = _ret_prompt(y3, cos_p, sin_p, lg_row, e_ret_norm_g[i], col_ret, 512)
    xp = _outproj(oa.reshape(bsz * t, br), 0, ob.reshape(bsz * t, br), 0, w_out_b, xp, mod_p, t, 512, d)

    y_s = _proj(xs, mod_s, e_norm_g[i], w_in_b, None, s, 1280, w_transposed=True)
    mixed_s, gla_s, ret_s = _even_decode(y_s, state_gla[i], state_ret[i], wa2_pad, e_b_a[i],
                                         e_gla_norm_g[i], e_ret_norm_g[i], cos_s, sin_s, lg_row, cols_dec)
    xs = _outproj(mixed_s, 0, mixed_s, 1, w_out_b, xs, mod_s, None, s, 1024)

    mod = _ada(c_rows, o_w_ada[i], o_b_ada[i])
    mod_s, mod_p = mod[:s], mod[s:s + bsz].reshape(bsz, 1, 3 * d)
    w_in_b = o_w_in[i].astype(BF16)
    w_out_b = o_w_out[i].astype(BF16)
    bs_full = jnp.repeat(jnp.transpose(o_b_s[i]), LANES, axis=1)

    y = _proj(xp, mod_p, o_norm_g[i], w_in_b, t, 1024, 1024)
    y3 = y.reshape(bsz, t, -1)
    qn, kn, knb, vo = _qkv_prompt(y, o_q_norm_g[i], o_k_norm_g[i], 512)
    shp = (bsz, t, br)
    oc = _sb_prompt(qn.reshape(shp), knb.reshape(shp), y3, o_sb_offset[i], 2 * br, 3 * br, 512, 256, 4)
    od = _cmlp_prompt(y3, o_v_norm_g[i], o_w_s[i], bs_full, (4, 5, 6), 256)
    xp = _outproj(oc.reshape(bsz * t, br), 0, od.reshape(bsz * t, br), 0, w_out_b, xp, mod_p, t, 512, d)

    y_s = _proj(xs, mod_s, o_norm_g[i], w_in_b, None, s, 1024)
    w0_row = jnp.repeat(o_w_s[i][:, 0, 0], LANES).reshape(1, br)
    b0_row = jnp.repeat(o_b_s[i][:, 0], LANES).reshape(1, br)
    qn_s, kn_s, vo_s, cv_s, od_s = _odd_decode(y_s, o_q_norm_g[i], o_k_norm_g[i], o_v_norm_g[i], w0_row, b0_row)
    dc = LANES
    oc_s = _paged_attention(qn_s.reshape(s, HC, dc), y_s[:, 3 * br:4 * br].reshape(s, HC, dc),
                            cache_k, cache_v, i, page_table, o_sb_offset[i]).reshape(s, br)
    xs = _outproj(oc_s, 0, od_s, 0, w_out_b, xs, mod_s, None, s, 1024)

    return (xp.reshape(bsz, t, d), xs.reshape(s, 1, d),
            gla_p[None], gla_s[None], ret_p[None], ret_s[None],
            kn.reshape(1, bsz, t, HC, dc), vo.reshape(1, bsz, t, HC, dc),
            kn_s.reshape(1, s, 1, HC, dc), vo_s.reshape(1, s, 1, HC, dc),
            cv_s.reshape(1, s, 1, br))
```

```python
import functools

import jax
import jax.numpy as jnp
import numpy as np
from jax import lax
from jax.experimental import pallas as pl
from jax.experimental.pallas import tpu as pltpu

F32 = jnp.float32
BF16 = jnp.bfloat16

HA = 4
HB = 4
HC = 8
DG = 8
GLA_RANK = 16
GLA_TAU = 16.0
GLA_CHUNK = 64
GLA_SAFE_LOG_DECAY = 40.0
RET_CHUNK = 128
ROPE_BASE = 10000.0
D_CHUNK = 128
RMS_EPS = 1e-6

LANES = 128
MXU_DIM = 256
VMEM_LIMIT = 56 * 1024 * 1024
LOG2E = np.float32(np.log2(np.e))
PAGES_PER_STEP = 8


def _cparams(sem):
    return pltpu.CompilerParams(dimension_semantics=sem, vmem_limit_bytes=VMEM_LIMIT)


def _dot(a, b):
    return jnp.dot(a, b, preferred_element_type=F32)


def _dot_nt(a, b):
    return lax.dot_general(a, b, (((1,), (1,)), ((), ())), preferred_element_type=F32)


def _dot_tn(a, b):
    return lax.dot_general(a, b, (((0,), (0,)), ((), ())), preferred_element_type=F32)


def _sigmoid(x):
    return 1.0 / (1.0 + jnp.exp(-x))


def _silu(x):
    return x * _sigmoid(x)


def _gelu(x):
    c = np.float32(np.sqrt(2.0 / np.pi))
    return 0.5 * x * (1.0 + jnp.tanh(c * (x + 0.044715 * (x * x * x))))


def _softplus_neg_abs(x):
    return jnp.log(1.0 + jnp.exp(-jnp.abs(x)))


def _neg_log2_rest(z2):
    return jnp.maximum(z2, 0.0) + jnp.log(1.0 + jnp.exp2(-jnp.abs(z2))) * LOG2E


def _split_dot(x, ones_bf16, left=False):
    hi = x.astype(BF16)
    lo = (x - hi.astype(F32)).astype(BF16)
    if left:
        return _dot(ones_bf16, hi) + _dot(ones_bf16, lo)
    return _dot(hi, ones_bf16) + _dot(lo, ones_bf16)


def _rms(x, g):
    ms = jnp.mean(x * x, axis=-1, keepdims=True)
    return x * lax.rsqrt(ms + RMS_EPS) * g


def _ada_kernel(c_ref, w_ref, b_ref, o_ref):
    s = _silu(c_ref[...]).astype(BF16)
    o_ref[...] = _dot(s, w_ref[...].astype(BF16)) + b_ref[...]


def _ada(c_rows, w, b):
    r, d = c_rows.shape
    n = w.shape[1]
    tn = 512
    return pl.pallas_call(
        _ada_kernel,
        out_shape=jax.ShapeDtypeStruct((r, n), F32),
        grid=(n // tn,),
        in_specs=[pl.BlockSpec((r, d), lambda j: (0, 0)),
                  pl.BlockSpec((d, tn), lambda j: (0, j)),
                  pl.BlockSpec((1, tn), lambda j: (0, j))],
        out_specs=pl.BlockSpec((r, tn), lambda j: (0, j)),
        compiler_params=_cparams(("arbitrary",)),
        name="ada",
    )(c_rows, w, b.reshape(1, n))


def _reorder_cast_kernel(w_ref, o_ref, *, n_head, n_rank):
    n = w_ref.shape[0]
    tail = n - n_head - n_rank
    o_ref[:n_head, :] = w_ref[:n_head, :].astype(BF16)
    o_ref[n_head:n_head + tail, :] = w_ref[n_head + n_rank:, :].astype(BF16)
    o_ref[n_head + tail:n, :] = w_ref[n_head:n_head + n_rank, :].astype(BF16)
    o_ref[n:, :] = jnp.zeros((o_ref.shape[0] - n, o_ref.shape[1]), BF16)


def _reorder_cast(wt3, layer, n_head, n_rank, n_out, tc):
    _, n, d = wt3.shape
    return pl.pallas_call(
        functools.partial(_reorder_cast_kernel, n_head=n_head, n_rank=n_rank),
        out_shape=jax.ShapeDtypeStruct((n_out, d), BF16),
        grid=(d // tc,),
        in_specs=[pl.BlockSpec((None, n, tc), lambda c: (layer, 0, c))],
        out_specs=pl.BlockSpec((n_out, tc), lambda c: (0, c)),
        compiler_params=_cparams(("arbitrary",)),
        name="reorder_cast",
    )(wt3)


def _proj_kernel(x_ref, shift_ref, scale_ref, g_ref, w_ref, o_ref, h_ref, *, w_transposed):
    @pl.when(pl.program_id(1) == 0)
    def _():
        h = _rms(x_ref[...], g_ref[...]) * (1.0 + scale_ref[...]) + shift_ref[...]
        h_ref[...] = h.astype(BF16)

    mm = _dot_nt if w_transposed else _dot
    o_ref[...] = mm(h_ref[...], w_ref[...]).astype(o_ref.dtype)


def _proj(x2d, mod, norm_g, w_bf16, rows_per_batch, tm, tn, w_transposed=False):
    m, d = x2d.shape
    out_dtype = F32 if rows_per_batch is None else BF16
    n = w_bf16.shape[0 if w_transposed else 1]
    w_spec = (pl.BlockSpec((tn, d), lambda i, j: (j, 0)) if w_transposed
              else pl.BlockSpec((d, tn), lambda i, j: (0, j)))
    if rows_per_batch is None:
        shift_spec = pl.BlockSpec((tm, d), lambda i, j: (i, 0))
        scale_spec = pl.BlockSpec((tm, d), lambda i, j: (i, 1))
    else:
        tpb = rows_per_batch // tm
        shift_spec = pl.BlockSpec((None, 1, d), lambda i, j: (i // tpb, 0, 0))
        scale_spec = pl.BlockSpec((None, 1, d), lambda i, j: (i // tpb, 0, 1))
    return pl.pallas_call(
        functools.partial(_proj_kernel, w_transposed=w_transposed),
        out_shape=jax.ShapeDtypeStruct((m, n), out_dtype),
        grid=(m // tm, n // tn),
        in_specs=[pl.BlockSpec((tm, d), lambda i, j: (i, 0)),
                  shift_spec, scale_spec,
                  pl.BlockSpec((1, d), lambda i, j: (0, 0)),
                  w_spec],
        out_specs=pl.BlockSpec((tm, tn), lambda i, j: (i, j)),
        scratch_shapes=[pltpu.VMEM((tm, d), BF16)],
        compiler_params=_cparams(("arbitrary", "arbitrary")),
        name="proj",
    )(x2d, mod, mod, norm_g.reshape(1, d), w_bf16)


def _outproj_kernel(ma_ref, mb_ref, wa_ref, wb_ref, x_ref, gate_ref, o_ref):
    mixed = _dot(ma_ref[...], wa_ref[...]) + _dot(mb_ref[...], wb_ref[...])
    o_ref[...] = x_ref[...] + gate_ref[...] * mixed


def _outproj(ma, ca, mb, cb, w_bf16, x2d, mod, rows_per_batch, tm, tn):
    m, d = x2d.shape
    half = w_bf16.shape[0] // 2
    gcol = 2 * (d // tn)
    if rows_per_batch is None:
        gate_spec = pl.BlockSpec((tm, tn), lambda i, j: (i, gcol + j))
    else:
        tpb = rows_per_batch // tm
        gate_spec = pl.BlockSpec((None, 1, tn), lambda i, j: (i // tpb, 0, gcol + j))
    return pl.pallas_call(
        _outproj_kernel,
        out_shape=jax.ShapeDtypeStruct((m, d), F32),
        grid=(m // tm, d // tn),
        in_specs=[pl.BlockSpec((tm, half), lambda i, j: (i, ca)),
                  pl.BlockSpec((tm, half), lambda i, j: (i, cb)),
                  pl.BlockSpec((half, tn), lambda i, j: (0, j)),
                  pl.BlockSpec((half, tn), lambda i, j: (1, j)),
                  pl.BlockSpec((tm, tn), lambda i, j: (i, j)),
                  gate_spec],
        out_specs=pl.BlockSpec((tm, tn), lambda i, j: (i, j)),
        compiler_params=_cparams(("arbitrary", "arbitrary")),
        name="outproj",
    )(ma, mb, w_bf16, w_bf16, x2d, mod)


def _gla_kernel(q_ref, k_ref, v_ref, ra_ref, g_ref, wa2_ref, ba_ref, ng_ref,
                o_ref, st_ref, stt_ref, la_ref, b_ref, *, n_sub, dk, dv):
    c = pl.program_id(1)
    cs = GLA_CHUNK

    @pl.when(c == 0)
    def _():
        stt_ref[...] = jnp.zeros_like(stt_ref)

    row = lax.broadcasted_iota(jnp.int32, (cs, cs), 0)
    col = lax.broadcasted_iota(jnp.int32, (cs, cs), 1)
    causal = row >= col
    tri = jnp.where(causal, 1.0, 0.0).astype(BF16)
    qscale = np.float32(dk ** -0.5)

    pre = _dot(ra_ref[...].astype(BF16), wa2_ref[...]) + ba_ref[...]
    la = (jnp.minimum(pre, 0.0) - _softplus_neg_abs(pre)) * np.float32(1.0 / GLA_TAU)
    la_ref[...] = la
    safe = jnp.min(la) >= np.float32(-GLA_SAFE_LOG_DECAY / cs)

    def pairwise_scores(qs, kb, hs):
        b = b_ref[:, hs]
        sublanes = 8

        def key_group(gi, acc):
            s0 = pl.multiple_of(gi * sublanes, sublanes)
            bk = b_ref[pl.ds(s0, sublanes), hs]
            for r in range(sublanes):
                e = jnp.exp(jnp.minimum(b - bk[r:r + 1, :], 0.0))
                acc = jnp.where(col == s0 + r, _dot_nt((qs * e).astype(BF16), kb), acc)
            return acc

        return lax.fori_loop(0, cs // sublanes, key_group, jnp.zeros((cs, cs), F32))

    def make_body(factorised):
        def body(i, carry):
            r0 = pl.multiple_of(i * cs, cs)
            rows = pl.ds(r0, cs)
            b = _split_dot(la_ref[rows, :], tri, left=True)
            bl = b[cs - 1:cs, :]
            k = k_ref[rows, :].astype(F32)
            qs = q_ref[rows, :].astype(F32) * qscale
            qe = (qs * jnp.exp(b)).astype(BF16)
            kd = (k * jnp.exp(bl - b)).astype(BF16)
            ebl = jnp.exp(bl)
            if factorised:
                ke = (k * jnp.exp(-b)).astype(BF16)
            else:
                b_ref[...] = b
            for h in range(HA):
                ks, vs = slice(h * dk, (h + 1) * dk), slice(h * dv, (h + 1) * dv)
                v = v_ref[rows, vs].astype(BF16)
                if factorised:
                    s = _dot_nt(qe[:, ks], ke[:, ks])
                else:
                    s = pairwise_scores(qs[:, ks], k[:, ks].astype(BF16), ks)
                s = jnp.where(causal, s, 0.0)
                stt = stt_ref[h]
                o = _dot(s.astype(BF16), v) + _dot_nt(qe[:, ks], stt.astype(BF16))
                stt_ref[h] = stt * ebl[:, ks] + _dot_tn(v, kd[:, ks])
                og = _rms(o, ng_ref[:, vs]) * _silu(g_ref[rows, vs].astype(F32))
                o_ref[rows, vs] = og.astype(o_ref.dtype)
            return carry
        return body

    @pl.when(safe)
    def _():
        lax.fori_loop(0, n_sub, make_body(True), 0, unroll=True)

    @pl.when(jnp.logical_not(safe))
    def _():
        lax.fori_loop(0, n_sub, make_body(False), 0)

    @pl.when(c == pl.num_programs(1) - 1)
    def _():
        for h in range(HA):
            st_ref[h] = stt_ref[h].T


def _gla_prompt(y3, wa2_pad, b_a, norm_g, tc):
    bsz, t, n = y3.shape
    dk, dv = LANES, 2 * LANES
    wk, wv = HA * dk, HA * dv
    kern = functools.partial(_gla_kernel, n_sub=tc // GLA_CHUNK, dk=dk, dv=dv)
    return pl.pallas_call(
        kern,
        out_shape=(jax.ShapeDtypeStruct((bsz, t, wv), BF16),
                   jax.ShapeDtypeStruct((bsz, HA, dk, dv), F32)),
        grid=(bsz, t // tc),
        in_specs=[pl.BlockSpec((None, tc, wk), lambda b, c: (b, c, 0)),
                  pl.BlockSpec((None, tc, wk), lambda b, c: (b, c, 1)),
                  pl.BlockSpec((None, tc, wv), lambda b, c: (b, c, 1)),
                  pl.BlockSpec((None, tc, LANES), lambda b, c: (b, c, (n - MXU_DIM) // LANES)),
                  pl.BlockSpec((None, tc, wv), lambda b, c: (b, c, 2)),
                  pl.BlockSpec((LANES, wk), lambda b, c: (0, 0)),
                  pl.BlockSpec((1, wk), lambda b, c: (0, 0)),
                  pl.BlockSpec((1, wv), lambda b, c: (0, 0))],
        out_specs=(pl.BlockSpec((None, tc, wv), lambda b, c: (b, c, 0)),
                   pl.BlockSpec((None, HA, dk, dv), lambda b, c: (b, 0, 0, 0))),
        scratch_shapes=[pltpu.VMEM((HA, dv, dk), F32), pltpu.VMEM((tc, wk), F32),
                        pltpu.VMEM((GLA_CHUNK, wk), F32)],
        compiler_params=_cparams(("arbitrary", "arbitrary")),
        name="gla_prompt",
    )(y3, y3, y3, y3, y3, wa2_pad, b_a.reshape(1, -1), norm_g.reshape(1, -1))


def _rot(x, cosf, sinf):
    return x * cosf + pltpu.roll(x, x.shape[-1] // 2, 1) * sinf


def _ret_kernel(q_ref, k_ref, v_ref, g_ref, cos_ref, sin_ref, lg_ref, ng_ref,
                o_ref, st_ref, stt_ref, *, n_sub, dk, dv):
    c = pl.program_id(1)
    cs = RET_CHUNK

    @pl.when(c == 0)
    def _():
        stt_ref[...] = jnp.zeros_like(stt_ref)

    row = lax.broadcasted_iota(jnp.int32, (cs, cs), 0)
    col = lax.broadcasted_iota(jnp.int32, (cs, cs), 1)
    rel = (row - col).astype(F32)
    ridx = lax.broadcasted_iota(jnp.int32, (cs, 1), 0).astype(F32)
    lgs = [lg_ref[:, h * dk:h * dk + 1] for h in range(HB)]
    decay = [jnp.where(row >= col, jnp.exp(rel * lg), 0.0) for lg in lgs]
    inter = [jnp.exp((ridx + 1.0) * lg) for lg in lgs]
    kdec = [jnp.exp((np.float32(cs - 1.0) - ridx) * lg) for lg in lgs]
    sdec = [jnp.exp(np.float32(cs) * lg) for lg in lgs]
    qscale = np.float32(dk ** -0.5)

    def body(i, carry):
        r0 = pl.multiple_of(i * cs, cs)
        rows = pl.ds(r0, cs)
        cosf = cos_ref[rows, :]
        sinf = sin_ref[rows, :]
        for h in range(HB):
            ks, vs = slice(h * dk, (h + 1) * dk), slice(h * dv, (h + 1) * dv)
            qr = (_rot(q_ref[rows, ks].astype(F32), cosf, sinf) * qscale).astype(BF16)
            kr = _rot(k_ref[rows, ks].astype(F32), cosf, sinf)
            v = v_ref[rows, vs].astype(BF16)
            s = _dot_nt(qr, kr.astype(BF16)) * decay[h]
            stt = stt_ref[h]
            o = _dot(s.astype(BF16), v) + _dot_nt(qr, stt.astype(BF16)) * inter[h]
            stt_ref[h] = stt * sdec[h] + _dot_tn(v, (kr * kdec[h]).astype(BF16))
            og = _rms(o, ng_ref[:, vs]) * _silu(g_ref[rows, vs].astype(F32))
            o_ref[rows, vs] = og.astype(o_ref.dtype)
        return carry

    lax.fori_loop(0, n_sub, body, 0, unroll=True)

    @pl.when(c == pl.num_programs(1) - 1)
    def _():
        for h in range(HB):
            st_ref[h] = stt_ref[h].T


def _ret_prompt(y3, cosf, sinf, lg_row, norm_g, col0, tc):
    bsz, t, _ = y3.shape
    dk, dv = LANES, 2 * LANES
    wk, wv = HB * dk, HB * dv
    cq, cv = col0 // wk, (col0 + 2 * wk) // wv
    kern = functools.partial(_ret_kernel, n_sub=tc // RET_CHUNK, dk=dk, dv=dv)
    return pl.pallas_call(
        kern,
        out_shape=(jax.ShapeDtypeStruct((bsz, t, wv), BF16),
                   jax.ShapeDtypeStruct((bsz, HB, dk, dv), F32)),
        grid=(bsz, t // tc),
        in_specs=[pl.BlockSpec((None, tc, wk), lambda b, c: (b, c, cq)),
                  pl.BlockSpec((None, tc, wk), lambda b, c: (b, c, cq + 1)),
                  pl.BlockSpec((None, tc, wv), lambda b, c: (b, c, cv)),
                  pl.BlockSpec((None, tc, wv), lambda b, c: (b, c, cv + 1)),
                  pl.BlockSpec((tc, dk), lambda b, c: (c, 0)),
                  pl.BlockSpec((tc, dk), lambda b, c: (c, 0)),
                  pl.BlockSpec((1, wk), lambda b, c: (0, 0)),
                  pl.BlockSpec((1, wv), lambda b, c: (0, 0))],
        out_specs=(pl.BlockSpec((None, tc, wv), lambda b, c: (b, c, 0)),
                   pl.BlockSpec((None, HB, dk, dv), lambda b, c: (b, 0, 0, 0))),
        scratch_shapes=[pltpu.VMEM((HB, dv, dk), F32)],
        compiler_params=_cparams(("arbitrary", "arbitrary")),
        name="ret_prompt",
    )(y3, y3, y3, y3, cosf, sinf, lg_row, norm_g.reshape(1, -1))


def _row_to_col(x_row):
    n = x_row.shape[-1]
    r = lax.broadcasted_iota(jnp.int32, (n, n), 0)
    c = lax.broadcasted_iota(jnp.int32, (n, n), 1)
    return jnp.sum(jnp.where(r == c, jnp.broadcast_to(x_row, (n, n)), 0.0), axis=1, keepdims=True)


def _even_decode_kernel(y_ref, sg_ref, sr_ref, wa2_ref, ba_ref, gng_ref, rng_ref, cos_ref, sin_ref,
                        lg_ref, o_ref, sgo_ref, sro_ref, *, cols):
    dk, dv = LANES, 2 * LANES
    cqa, cka, cva, cga, cqb, ckb, cvb, cgb, cra = cols
    y = y_ref[...]
    pre = _dot(y[:, cra:cra + LANES].astype(BF16), wa2_ref[...]) + ba_ref[...]
    la = (jnp.minimum(pre, 0.0) - _softplus_neg_abs(pre)) * np.float32(1.0 / GLA_TAU)
    alpha = jnp.exp(la)
    cosf = cos_ref[...]
    sinf = sin_ref[...]
    gam = jnp.exp(lg_ref[...])
    outs = []
    for h in range(HA):
        q = y[:, cqa + h * dk:cqa + (h + 1) * dk] * np.float32(dk ** -0.5)
        k = y[:, cka + h * dk:cka + (h + 1) * dk]
        v = y[:, cva + h * dv:cva + (h + 1) * dv]
        s_new = _row_to_col(alpha[:, h * dk:(h + 1) * dk]) * sg_ref[h] + _row_to_col(k) * v
        sgo_ref[h] = s_new
        o = jnp.sum(_row_to_col(q) * s_new, axis=0, keepdims=True)
        g = y[:, cga + h * dv:cga + (h + 1) * dv]
        outs.append(_rms(o, gng_ref[:, h * dv:(h + 1) * dv]) * _silu(g))
    for h in range(HB):
        q = _rot(y[:, cqb + h * dk:cqb + (h + 1) * dk], cosf, sinf) * np.float32(dk ** -0.5)
        k = _rot(y[:, ckb + h * dk:ckb + (h + 1) * dk], cosf, sinf)
        v = y[:, cvb + h * dv:cvb + (h + 1) * dv]
        s_new = _row_to_col(gam[:, h * dk:(h + 1) * dk]) * sr_ref[h] + _row_to_col(k) * v
        sro_ref[h] = s_new
        o = jnp.sum(_row_to_col(q) * s_new, axis=0, keepdims=True)
        g = y[:, cgb + h * dv:cgb + (h + 1) * dv]
        outs.append(_rms(o, rng_ref[:, h * dv:(h + 1) * dv]) * _silu(g))
    o_ref[...] = jnp.concatenate(outs, axis=-1).astype(o_ref.dtype)


def _even_decode(y_s, state_gla, state_ret, wa2_pad, b_a, gla_ng, ret_ng, cos_row, sin_row, lg_row, cols):
    bsz, n = y_s.shape
    dk, dv = LANES, 2 * LANES
    wide = (HA + HB) * dv
    kern = functools.partial(_even_decode_kernel, cols=cols)
    full = lambda a: pl.BlockSpec(a.shape, lambda b: (0,) * a.ndim)
    b_a2, gng, rng_ = b_a.reshape(1, -1), gla_ng.reshape(1, -1), ret_ng.reshape(1, -1)
    mixed, sg, sr = pl.pallas_call(
        kern,
        out_shape=(jax.ShapeDtypeStruct((bsz, 1, wide), BF16),
                   jax.ShapeDtypeStruct(state_gla.shape, F32),
                   jax.ShapeDtypeStruct(state_ret.shape, F32)),
        grid=(bsz,),
        in_specs=[pl.BlockSpec((None, 1, n), lambda b: (b, 0, 0)),
                  pl.BlockSpec((None, HA, dk, dv), lambda b: (b, 0, 0, 0)),
                  pl.BlockSpec((None, HB, dk, dv), lambda b: (b, 0, 0, 0)),
                  full(wa2_pad), full(b_a2), full(gng), full(rng_), full(cos_row), full(sin_row), full(lg_row)],
        out_specs=(pl.BlockSpec((None, 1, wide), lambda b: (b, 0, 0)),
                   pl.BlockSpec((None, HA, dk, dv), lambda b: (b, 0, 0, 0)),
                   pl.BlockSpec((None, HB, dk, dv), lambda b: (b, 0, 0, 0))),
        compiler_params=_cparams(("arbitrary",)),
        name="even_decode",
    )(y_s.reshape(bsz, 1, n), state_gla, state_ret, wa2_pad, b_a2, gng, rng_, cos_row, sin_row, lg_row)
    return mixed.reshape(bsz, wide), sg, sr


def _qkv_kernel(q_ref, k_ref, v_ref, qg_ref, kg_ref, qn_ref, kn_ref, knb_ref, vo_ref):
    dc = qg_ref.shape[-1]
    qscale = np.float32(dc ** -0.5) * LOG2E
    for h in range(q_ref.shape[-1] // dc):
        hs = slice(h * dc, (h + 1) * dc)
        qn_ref[:, hs] = (_rms(q_ref[:, hs].astype(F32), qg_ref[...]) * qscale).astype(BF16)
        kn = _rms(k_ref[:, hs].astype(F32), kg_ref[...])
        kn_ref[:, hs] = kn
        knb_ref[:, hs] = kn.astype(BF16)
    vo_ref[...] = v_ref[...].astype(F32)


def _qkv_prompt(y2, q_g, k_g, tm):
    m = y2.shape[0]
    dc = LANES
    wide = HC * dc
    blk = lambda c: pl.BlockSpec((tm, wide), lambda i: (i, c))
    gspec = pl.BlockSpec((1, dc), lambda i: (0, 0))
    return pl.pallas_call(
        _qkv_kernel,
        out_shape=(jax.ShapeDtypeStruct((m, wide), BF16),
                   jax.ShapeDtypeStruct((m, wide), F32),
                   jax.ShapeDtypeStruct((m, wide), BF16),
                   jax.ShapeDtypeStruct((m, wide), F32)),
        grid=(m // tm,),
        in_specs=[blk(0), blk(1), blk(2), gspec, gspec],
        out_specs=(blk(0), blk(0), blk(0), blk(0)),
        compiler_params=_cparams(("arbitrary",)),
        name="qkv_prompt",
    )(y2, y2, y2, q_g.reshape(1, dc), k_g.reshape(1, dc))


def _sb_kernel(qi_ref, kb_ref, q_ref, k_ref, v_ref, g_ref, qone_ref, koff_ref, o_ref, acc_ref, run_ref,
               *, sub, hp):
    p = pl.program_id(2)
    qi = qi_ref[p]
    kb = kb_ref[p]
    tq = q_ref.shape[0]
    dc = q_ref.shape[1] // hp

    @pl.when(kb == qi)
    def _():
        acc_ref[...] = jnp.zeros_like(acc_ref)
        run_ref[...] = jnp.zeros_like(run_ref)

    def block(masked):
        r = lax.broadcasted_iota(jnp.int32, (sub, sub), 0)
        c = lax.broadcasted_iota(jnp.int32, (sub, sub), 1)
        after = jnp.where(r > c, 1.0, 0.0).astype(BF16)
        units = [(hh, s_i, s_i * sub if masked else 0) for hh in range(hp) for s_i in reversed(range(tq // sub))]
        zs, masks, log_betas, rests, firsts, laters, ws = [], [], [], [], [], [], []
        for hh, s_i, r0 in units:
            hs = slice(hh * dc, (hh + 1) * dc)
            qa = jnp.concatenate([q_ref[r0:, hs], qone_ref[r0:, :]], axis=1)
            ka = jnp.concatenate([k_ref[s_i * sub:(s_i + 1) * sub, hs], koff_ref[hh]], axis=1)
            zs.append(_dot_nt(qa, ka))
        for z in zs:
            rest = _neg_log2_rest(z)
            log_betas.append(z - rest)
            if masked:
                mask = (lax.broadcasted_iota(jnp.int32, z.shape, 1)
                        < lax.broadcasted_iota(jnp.int32, z.shape, 0))
                rest = jnp.where(mask, rest, 0.0)
                masks.append(mask)
            firsts.append(rest[:, 0:1])
            rests.append(rest.astype(BF16))
        for rb in rests:
            laters.append(_dot(rb, after))
        for u in range(len(units)):
            w = jnp.exp2(log_betas[u] - laters[u])
            if masked:
                w = jnp.where(masks[u], w, 0.0)
            ws.append(w.astype(BF16))
        for u, (hh, s_i, r0) in enumerate(units):
            hs = slice(hh * dc, (hh + 1) * dc)
            run = run_ref[hh, r0:, :]
            acc_ref[r0:, hs] += jnp.exp2(-run) * _dot(ws[u], v_ref[s_i * sub:(s_i + 1) * sub, hs])
            run_ref[hh, r0:, :] = run + (laters[u][:, 0:1] + firsts[u])

    pl.when(kb == qi)(functools.partial(block, True))
    pl.when(kb < qi)(functools.partial(block, False))

    @pl.when(kb == 0)
    def _():
        o_ref[...] = (acc_ref[...] * _silu(g_ref[...].astype(F32))).astype(o_ref.dtype)


def _sb_prompt(qn, kn, y3, sb_offset, cv, cg, tq, sub, hp):
    bsz, t, _ = qn.shape
    dc = LANES
    wide = hp * dc
    nq = t // tq
    pairs = [(qi, kb) for qi in range(nq) for kb in range(qi, -1, -1)]
    qi_tab = jnp.asarray([p[0] for p in pairs], jnp.int32)
    kb_tab = jnp.asarray([p[1] for p in pairs], jnp.int32)
    off2 = sb_offset.astype(F32) * LOG2E
    terms, rem = [], off2
    for _ in range(3):
        terms.append(rem.astype(BF16))
        rem = rem - terms[-1].astype(F32)
    lane = jnp.arange(dc)
    koff = sum(jnp.where(lane == n, tm[:, None, None], 0).astype(BF16) for n, tm in enumerate(terms))
    koff = jnp.broadcast_to(koff, (HC, sub, dc))
    qone = jnp.broadcast_to(jnp.where(lane < len(terms), 1, 0).astype(BF16), (tq, dc))
    kern = functools.partial(_sb_kernel, sub=sub, hp=hp)
    q_map = lambda b, h, p, qi, kb: (b, qi[p], h)
    k_map = lambda b, h, p, qi, kb: (b, kb[p], h)
    return pl.pallas_call(
        kern,
        out_shape=jax.ShapeDtypeStruct((bsz, t, HC * dc), BF16),
        grid_spec=pltpu.PrefetchScalarGridSpec(
            num_scalar_prefetch=2,
            grid=(bsz, HC // hp, len(pairs)),
            in_specs=[pl.BlockSpec((None, tq, wide), q_map),
                      pl.BlockSpec((None, tq, wide), k_map),
                      pl.BlockSpec((None, tq, wide), lambda b, h, p, qi, kb: (b, kb[p], cv // wide + h)),
                      pl.BlockSpec((None, tq, wide), lambda b, h, p, qi, kb: (b, qi[p], cg // wide + h)),
                      pl.BlockSpec((tq, dc), lambda b, h, p, qi, kb: (0, 0)),
                      pl.BlockSpec((hp, sub, dc), lambda b, h, p, qi, kb: (h, 0, 0))],
            out_specs=pl.BlockSpec((None, tq, wide), q_map),
            scratch_shapes=[pltpu.VMEM((tq, wide), F32), pltpu.VMEM((hp, tq, 1), F32)]),
        compiler_params=_cparams(("arbitrary",) * 3),
        name="sb_prompt",
    )(qi_tab, kb_tab, qn, kn, y3, y3, qone, koff)


def _cmlp_kernel(u_ref, v_ref, g_ref, vg_ref, ws_ref, bs_ref, o_ref, *, n_chunk):
    cs = D_CHUNK
    u = _gelu(u_ref[...].astype(F32))
    vn = _rms(_gelu(v_ref[...].astype(F32)), vg_ref[...]).astype(BF16)
    gate = _silu(g_ref[...].astype(F32))
    bs = bs_ref[...]
    r = lax.broadcasted_iota(jnp.int32, (cs, cs), 0)
    c = lax.broadcasted_iota(jnp.int32, (cs, cs), 1)
    for g in range(DG):
        w = jnp.where(r >= c, ws_ref[g], 0.0).astype(BF16)
        lo, hi = g * LANES, (g + 1) * LANES
        for ch in range(n_chunk):
            r0, r1 = ch * cs, (ch + 1) * cs
            mixed = _dot(w, vn[r0:r1, lo:hi]) + bs[:, lo:hi]
            o_ref[r0:r1, lo:hi] = (u[r0:r1, lo:hi] * mixed * gate[r0:r1, lo:hi]).astype(o_ref.dtype)


def _cmlp_prompt(y3, v_norm_g, w_s, bs_full, cols, tm):
    bsz, t, _ = y3.shape
    br = DG * LANES
    cu, cv, cg = cols
    kern = functools.partial(_cmlp_kernel, n_chunk=tm // D_CHUNK)
    return pl.pallas_call(
        kern,
        out_shape=jax.ShapeDtypeStruct((bsz, t, br), BF16),
        grid=(bsz, t // tm),
        in_specs=[pl.BlockSpec((None, tm, br), lambda b, i: (b, i, cu)),
                  pl.BlockSpec((None, tm, br), lambda b, i: (b, i, cv)),
                  pl.BlockSpec((None, tm, br), lambda b, i: (b, i, cg)),
                  pl.BlockSpec((1, br), lambda b, i: (0, 0)),
                  pl.BlockSpec((DG, D_CHUNK, D_CHUNK), lambda b, i: (0, 0, 0)),
                  pl.BlockSpec((D_CHUNK, br), lambda b, i: (0, 0))],
        out_specs=pl.BlockSpec((None, tm, br), lambda b, i: (b, i, 0)),
        compiler_params=_cparams(("arbitrary", "arbitrary")),
        name="cmlp_prompt",
    )(y3, y3, y3, v_norm_g.reshape(1, br), w_s, bs_full)


def _odd_decode_kernel(y_ref, qg_ref, kg_ref, vg_ref, w0_ref, b0_ref, qn_ref, kn_ref, vo_ref, cv_ref, od_ref):
    br = HC * LANES
    y = y_ref[...]
    for h in range(HC):
        lo, hi = h * LANES, (h + 1) * LANES
        qn_ref[:, lo:hi] = _rms(y[:, lo:hi], qg_ref[...])
        kn_ref[:, lo:hi] = _rms(y[:, br + lo:br + hi], kg_ref[...])
    vo_ref[...] = y[:, 2 * br:3 * br]
    u = _gelu(y[:, 4 * br:5 * br])
    vn = _rms(_gelu(y[:, 5 * br:6 * br]), vg_ref[...])
    cv_ref[...] = vn
    od = u * (w0_ref[...] * vn + b0_ref[...]) * _silu(y[:, 6 * br:7 * br])
    od_ref[...] = od.astype(od_ref.dtype)


def _odd_decode(y_s, q_g, k_g, v_norm_g, w0_row, b0_row):
    s = y_s.shape[0]
    br = HC * LANES
    args = (y_s, q_g.reshape(1, -1), k_g.reshape(1, -1), v_norm_g.reshape(1, -1), w0_row, b0_row)
    full = lambda a: pl.BlockSpec(a.shape, lambda i: (0,) * a.ndim)
    osd = lambda dt: jax.ShapeDtypeStruct((s, br), dt)
    ospec = pl.BlockSpec((s, br), lambda i: (0, 0))
    return pl.pallas_call(
        _odd_decode_kernel,
        out_shape=(osd(F32), osd(F32), osd(F32), osd(F32), osd(BF16)),
        grid=(1,),
        in_specs=[full(a) for a in args],
        out_specs=(ospec,) * 5,
        compiler_params=_cparams(("arbitrary",)),
        name="odd_decode",
    )(*args)


def _strided_suffix_sums(x, stride, axis):
    n = x.shape[axis]
    idx = lax.broadcasted_iota(jnp.int32, x.shape, axis)
    inc, tot = x, x
    k = stride
    while k < n:
        inc = inc + jnp.where(idx < n - k, pltpu.roll(inc, n - k, axis), 0.0)
        tot = tot + pltpu.roll(tot, n - k, axis)
        k *= 2
    return inc, tot


def _paged_kernel(pt_ref, q_ref, g_ref, off_ref, *refs, n_pp):
    k_refs, v_refs = refs[:n_pp], refs[n_pp:2 * n_pp]
    o_ref, acc_ref, run_ref = refs[2 * n_pp:]
    j = pl.program_id(1)
    page, hc, dc = k_refs[0].shape
    rows = page * hc
    lane = lax.broadcasted_iota(jnp.int32, (hc, rows), 1)
    own = (lane & (hc - 1)) == lax.broadcasted_iota(jnp.int32, (hc, rows), 0)

    @pl.when(j == 0)
    def _():
        acc_ref[...] = jnp.zeros_like(acc_ref)
        run_ref[...] = jnp.zeros_like(run_ref)

    q = (q_ref[...] * (np.float32(dc ** -0.5) * LOG2E)).astype(BF16)
    off = off_ref[...] * LOG2E
    prow = lax.broadcasted_iota(jnp.int32, (n_pp, rows), 0)
    z = jnp.zeros((n_pp, rows), F32)
    for p in range(n_pp):
        kf = k_refs[p][...].reshape(rows, dc).astype(BF16)
        s = _dot_nt(q, kf)
        z = jnp.where(prow == p, jnp.sum(jnp.where(own, s, 0.0), axis=0, keepdims=True), z)
    z = z + off
    incl, tot = _strided_suffix_sums(_neg_log2_rest(z), hc, 1)
    pages_incl, _ = _strided_suffix_sums(tot, 1, 0)
    run = run_ref[...]
    w = jnp.exp2(z - incl - (pages_incl - tot) - run)
    run_ref[...] = run + pages_incl[0:1, :]
    acc = acc_ref[...]
    for p in range(n_pp):
        wm = jnp.where(own, jnp.broadcast_to(w[p:p + 1, :], (hc, rows)), 0.0).astype(BF16)
        acc = acc + _dot(wm, v_refs[p][...].reshape(rows, dc).astype(BF16))
    acc_ref[...] = acc

    @pl.when(j == pl.num_programs(1) - 1)
    def _():
        o_ref[...] = (acc_ref[...] * _silu(g_ref[...])).astype(o_ref.dtype)


def _paged_attention(qn_s, gate_s, cache_k, cache_v, layer, page_table, sb_offset):
    s, hc, dc = qn_s.shape
    n_pages = page_table.shape[1]
    page = cache_k.shape[2]
    n_pp = PAGES_PER_STEP
    assert hc & (hc - 1) == 0 and n_pages % n_pp == 0
    rows = page * hc

    def kv_spec(p):
        return pl.BlockSpec((None, None, page, hc, dc),
                            lambda b, j, pt: (layer, pt[b, n_pages - (j + 1) * n_pp + p], 0, 0, 0))

    head_blk = pl.BlockSpec((None, hc, dc), lambda b, j, pt: (b, 0, 0))
    return pl.pallas_call(
        functools.partial(_paged_kernel, n_pp=n_pp),
        out_shape=jax.ShapeDtypeStruct((s, hc, dc), BF16),
        grid_spec=pltpu.PrefetchScalarGridSpec(
            num_scalar_prefetch=1,
            grid=(s, n_pages // n_pp),
            in_specs=[head_blk, head_blk, pl.BlockSpec((1, rows), lambda b, j, pt: (0, 0))]
            + [kv_spec(p) for p in range(n_pp)] * 2,
            out_specs=head_blk,
            scratch_shapes=[pltpu.VMEM((hc, dc), F32), pltpu.VMEM((1, rows), F32)]),
        compiler_params=_cparams(("arbitrary", "arbitrary")),
        name="paged_attention",
    )(page_table, qn_s, gate_s, jnp.tile(sb_offset, page).reshape(1, rows),
      *([cache_k] * n_pp), *([cache_v] * n_pp))


def _rope_tables(pos, half):
    inv = ROPE_BASE ** (-jnp.arange(half, dtype=F32) / half)
    ang = pos.astype(F32)[:, None] * inv[None, :]
    cos, sin = jnp.cos(ang), jnp.sin(ang)
    return jnp.concatenate([cos, cos], axis=-1), jnp.concatenate([-sin, sin], axis=-1)


def kernel(x_prompt, x_sample, state_gla, state_ret, cache_k, cache_v, page_table, c_prompt, c_sample,
           e_norm_g, e_w_ada, e_b_ada, e_w_in, e_w_a2, e_b_a, e_gla_norm_g, e_ret_norm_g, e_w_out,
           o_norm_g, o_w_ada, o_b_ada, o_w_in, o_q_norm_g, o_k_norm_g, o_sb_offset, o_v_norm_g,
           o_w_s, o_b_s, o_w_out):
    bsz, t, d = x_prompt.shape
    s = x_sample.shape[0]
    br = d // 2
    n_past = page_table.shape[1] * cache_k.shape[2]
    dk = LANES

    c_rows = jnp.concatenate([c_sample, c_prompt, jnp.zeros((16 - s - bsz, d), F32)], axis=0)

    n_qkv = 2 * HA * dk + HA * 2 * dk
    n_main = e_w_in.shape[2] - GLA_RANK
    col_ret = n_qkv + HA * 2 * dk
    cols_dec = (0, HA * dk, 2 * HA * dk, n_qkv,
                col_ret, col_ret + HB * dk, col_ret + 2 * HB * dk, col_ret + 2 * HB * dk + HB * 2 * dk,
                n_main)

    xp = x_prompt.reshape(bsz * t, d)
    xs = x_sample.reshape(s, d)
    pos_p = jnp.arange(t)
    pos_s = n_past + jnp.arange(x_sample.shape[1])
    cos_p, sin_p = _rope_tables(pos_p, dk // 2)
    cos_s, sin_s = _rope_tables(pos_s, dk // 2)
    log_gamma = jnp.log1p(-jnp.exp2(-5.0 - jnp.arange(HB, dtype=F32)))
    lg_row = jnp.repeat(log_gamma, dk).reshape(1, HB * dk)

    i = 0
    mod = _ada(c_rows, e_w_ada[i], e_b_ada[i])
    mod_s, mod_p = mod[:s], mod[s:s + bsz].reshape(bsz, 1, 3 * d)
    w_in_b = _reorder_cast(jnp.swapaxes(e_w_in, 1, 2), i, n_qkv, GLA_RANK, n_main + MXU_DIM, 256)
    wa2_pad = jnp.concatenate([e_w_a2[i], jnp.zeros((LANES - GLA_RANK, HA * dk), F32)], axis=0).astype(BF16)
    w_out_b = e_w_out[i].astype(BF16)

    y = _proj(xp, mod_p, e_norm_g[i], w_in_b, t, 1024, 1280, w_transposed=True)
    y3 = y.reshape(bsz, t, -1)
    oa, gla_p = _gla_prompt(y3, wa2_pad, e_b_a[i], e_gla_norm_g[i], 512)
    ob, ret_p = _ret_prompt(y3, cos_p, sin_p, lg_row, e_ret_norm_g[i], col_ret, 512)
    xp = _outproj(oa.reshape(bsz * t, br), 0, ob.reshape(bsz * t, br), 0, w_out_b, xp, mod_p, t, 512, d)

    y_s = _proj(xs, mod_s, e_norm_g[i], w_in_b, None, s, 1280, w_transposed=True)
    mixed_s, gla_s, ret_s = _even_decode(y_s, state_gla[i], state_ret[i], wa2_pad, e_b_a[i],
                                         e_gla_norm_g[i], e_ret_norm_g[i], cos_s, sin_s, lg_row, cols_dec)
    xs = _outproj(mixed_s, 0, mixed_s, 1, w_out_b, xs, mod_s, None, s, 1024)

    mod = _ada(c_rows, o_w_ada[i], o_b_ada[i])
    mod_s, mod_p = mod[:s], mod[s:s + bsz].reshape(bsz, 1, 3 * d)
    w_in_b = o_w_in[i].astype(BF16)
    w_out_b = o_w_out[i].astype(BF16)
    bs_full = jnp.repeat(jnp.transpose(o_b_s[i]), LANES, axis=1)

    y = _proj(xp, mod_p, o_norm_g[i], w_in_b, t, 1024, 1024)
    y3 = y.reshape(bsz, t, -1)
    qn, kn, knb, vo = _qkv_prompt(y, o_q_norm_g[i], o_k_norm_g[i], 512)
    shp = (bsz, t, br)
    oc = _sb_prompt(qn.reshape(shp), knb.reshape(shp), y3, o_sb_offset[i], 2 * br, 3 * br, 512, 256, 4)
    od = _cmlp_prompt(y3, o_v_norm_g[i], o_w_s[i], bs_full, (4, 5, 6), 256)
    xp = _outproj(oc.reshape(bsz * t, br), 0, od.reshape(bsz * t, br), 0, w_out_b, xp, mod_p, t, 512, d)

    y_s = _proj(xs, mod_s, o_norm_g[i], w_in_b, None, s, 1024)
    w0_row = jnp.repeat(o_w_s[i][:, 0, 0], LANES).reshape(1, br)
    b0_row = jnp.repeat(o_b_s[i][:, 0], LANES).reshape(1, br)
    qn_s, kn_s, vo_s, cv_s, od_s = _odd_decode(y_s, o_q_norm_g[i], o_k_norm_g[i], o_v_norm_g[i], w0_row, b0_row)
    dc = LANES
    oc_s = _paged_attention(qn_s.reshape(s, HC, dc), y_s[:, 3 * br:4 * br].reshape(s, HC, dc),
                            cache_k, cache_v, i, page_table, o_sb_offset[i]).reshape(s, br)
    xs = _outproj(oc_s, 0, od_s, 0, w_out_b, xs, mod_s, None, s, 1024)

    return (xp.reshape(bsz, t, d), xs.reshape(s, 1, d),
            gla_p[None], gla_s[None], ret_p[None], ret_s[None],
            kn.reshape(1, bsz, t, HC, dc), vo.reshape(1, bsz, t, HC, dc),
            kn_s.reshape(1, s, 1, HC, dc), vo_s.reshape(1, s, 1, HC, dc),
            cv_s.reshape(1, s, 1, br))
```

```python
import functools

import jax
import jax.numpy as jnp
import numpy as np
from jax import lax
from jax.experimental import pallas as pl
from jax.experimental.pallas import tpu as pltpu

F32 = jnp.float32
BF16 = jnp.bfloat16

HA = 4
HB = 4
HC = 8
DG = 8
GLA_RANK = 16
GLA_TAU = 16.0
GLA_CHUNK = 64
GLA_SAFE_LOG_DECAY = 40.0
RET_CHUNK = 128
ROPE_BASE = 10000.0
D_CHUNK = 128
RMS_EPS = 1e-6

LANES = 128
MXU_DIM = 256
VMEM_LIMIT = 56 * 1024 * 1024
LOG2E = np.float32(np.log2(np.e))
PAGES_PER_STEP = 16


def _cparams(sem):
    return pltpu.CompilerParams(dimension_semantics=sem, vmem_limit_bytes=VMEM_LIMIT)


def _dot(a, b):
    return jnp.dot(a, b, preferred_element_type=F32)


def _dot_nt(a, b):
    return lax.dot_general(a, b, (((1,), (1,)), ((), ())), preferred_element_type=F32)


def _dot_tn(a, b):
    return lax.dot_general(a, b, (((0,), (0,)), ((), ())), preferred_element_type=F32)


def _sigmoid(x):
    return 1.0 / (1.0 + jnp.exp(-x))


def _silu(x):
    return x * _sigmoid(x)


def _gelu(x):
    c = np.float32(np.sqrt(2.0 / np.pi))
    return 0.5 * x * (1.0 + jnp.tanh(c * (x + 0.044715 * (x * x * x))))


def _softplus_neg_abs(x):
    return jnp.log(1.0 + jnp.exp(-jnp.abs(x)))


def _neg_log2_rest(z2):
    return jnp.maximum(z2, 0.0) + jnp.log(1.0 + jnp.exp2(-jnp.abs(z2))) * LOG2E


def _split_dot(x, ones_bf16, left=False):
    hi = x.astype(BF16)
    lo = (x - hi.astype(F32)).astype(BF16)
    if left:
        return _dot(ones_bf16, hi) + _dot(ones_bf16, lo)
    return _dot(hi, ones_bf16) + _dot(lo, ones_bf16)


def _rms(x, g):
    ms = jnp.mean(x * x, axis=-1, keepdims=True)
    return x * lax.rsqrt(ms + RMS_EPS) * g


def _ada_kernel(c_ref, w_ref, b_ref, o_ref):
    s = _silu(c_ref[...]).astype(BF16)
    o_ref[...] = _dot(s, w_ref[...].astype(BF16)) + b_ref[...]


def _ada(c_rows, w, b):
    r, d = c_rows.shape
    n = w.shape[1]
    tn = 512
    return pl.pallas_call(
        _ada_kernel,
        out_shape=jax.ShapeDtypeStruct((r, n), F32),
        grid=(n // tn,),
        in_specs=[pl.BlockSpec((r, d), lambda j: (0, 0)),
                  pl.BlockSpec((d, tn), lambda j: (0, j)),
                  pl.BlockSpec((1, tn), lambda j: (0, j))],
        out_specs=pl.BlockSpec((r, tn), lambda j: (0, j)),
        compiler_params=_cparams(("arbitrary",)),
        name="ada",
    )(c_rows, w, b.reshape(1, n))


def _reorder_cast_kernel(w_ref, o_ref, *, n_head, n_rank):
    n = w_ref.shape[0]
    tail = n - n_head - n_rank
    o_ref[:n_head, :] = w_ref[:n_head, :].astype(BF16)
    o_ref[n_head:n_head + tail, :] = w_ref[n_head + n_rank:, :].astype(BF16)
    o_ref[n_head + tail:n, :] = w_ref[n_head:n_head + n_rank, :].astype(BF16)
    o_ref[n:, :] = jnp.zeros((o_ref.shape[0] - n, o_ref.shape[1]), BF16)


def _reorder_cast(wt3, layer, n_head, n_rank, n_out, tc):
    _, n, d = wt3.shape
    return pl.pallas_call(
        functools.partial(_reorder_cast_kernel, n_head=n_head, n_rank=n_rank),
        out_shape=jax.ShapeDtypeStruct((n_out, d), BF16),
        grid=(d // tc,),
        in_specs=[pl.BlockSpec((None, n, tc), lambda c: (layer, 0, c))],
        out_specs=pl.BlockSpec((n_out, tc), lambda c: (0, c)),
        compiler_params=_cparams(("arbitrary",)),
        name="reorder_cast",
    )(wt3)


def _proj_kernel(x_ref, shift_ref, scale_ref, g_ref, w_ref, o_ref, h_ref, *, w_transposed, n_chunks):
    mm = _dot_nt if w_transposed else _dot
    j = pl.program_id(1)

    @pl.when(j == 0)
    def _():
        rc = x_ref.shape[0] // n_chunks
        per_row = shift_ref.shape[0] > 1
        for c in range(n_chunks):
            rs = slice(c * rc, (c + 1) * rc)
            scale = scale_ref[rs, :] if per_row else scale_ref[...]
            shift = shift_ref[rs, :] if per_row else shift_ref[...]
            h = (_rms(x_ref[rs, :], g_ref[...]) * (1.0 + scale) + shift).astype(BF16)
            h_ref[rs, :] = h
            o_ref[rs, :] = mm(h, w_ref[...]).astype(o_ref.dtype)

    @pl.when(j > 0)
    def _():
        o_ref[...] = mm(h_ref[...], w_ref[...]).astype(o_ref.dtype)


def _proj(x2d, mod, norm_g, w_bf16, rows_per_batch, tm, tn, w_transposed=False):
    m, d = x2d.shape
    out_dtype = F32 if rows_per_batch is None else BF16
    n = w_bf16.shape[0 if w_transposed else 1]
    w_spec = (pl.BlockSpec((tn, d), lambda i, j: (j, 0)) if w_transposed
              else pl.BlockSpec((d, tn), lambda i, j: (0, j)))
    if rows_per_batch is None:
        shift_spec = pl.BlockSpec((tm, d), lambda i, j: (i, 0))
        scale_spec = pl.BlockSpec((tm, d), lambda i, j: (i, 1))
    else:
        tpb = rows_per_batch // tm
        shift_spec = pl.BlockSpec((None, 1, d), lambda i, j: (i // tpb, 0, 0))
        scale_spec = pl.BlockSpec((None, 1, d), lambda i, j: (i // tpb, 0, 1))
    return pl.pallas_call(
        functools.partial(_proj_kernel, w_transposed=w_transposed, n_chunks=max(1, tm // MXU_DIM)),
        out_shape=jax.ShapeDtypeStruct((m, n), out_dtype),
        grid=(m // tm, n // tn),
        in_specs=[pl.BlockSpec((tm, d), lambda i, j: (i, 0)),
                  shift_spec, scale_spec,
                  pl.BlockSpec((1, d), lambda i, j: (0, 0)),
                  w_spec],
        out_specs=pl.BlockSpec((tm, tn), lambda i, j: (i, j)),
        scratch_shapes=[pltpu.VMEM((tm, d), BF16)],
        compiler_params=_cparams(("arbitrary", "arbitrary")),
        name="proj",
    )(x2d, mod, mod, norm_g.reshape(1, d), w_bf16)


def _outproj_kernel(ma_ref, mb_ref, wa_ref, wb_ref, x_ref, gate_ref, o_ref):
    mixed = _dot(ma_ref[...], wa_ref[...]) + _dot(mb_ref[...], wb_ref[...])
    o_ref[...] = x_ref[...] + gate_ref[...] * mixed


def _outproj(ma, ca, mb, cb, w_bf16, x2d, mod, rows_per_batch, tm, tn):
    m, d = x2d.shape
    half = w_bf16.shape[0] // 2
    gcol = 2 * (d // tn)
    if rows_per_batch is None:
        gate_spec = pl.BlockSpec((tm, tn), lambda i, j: (i, gcol + j))
    else:
        tpb = rows_per_batch // tm
        gate_spec = pl.BlockSpec((None, 1, tn), lambda i, j: (i // tpb, 0, gcol + j))
    return pl.pallas_call(
        _outproj_kernel,
        out_shape=jax.ShapeDtypeStruct((m, d), F32),
        grid=(m // tm, d // tn),
        in_specs=[pl.BlockSpec((tm, half), lambda i, j: (i, ca)),
                  pl.BlockSpec((tm, half), lambda i, j: (i, cb)),
                  pl.BlockSpec((half, tn), lambda i, j: (0, j)),
                  pl.BlockSpec((half, tn), lambda i, j: (1, j)),
                  pl.BlockSpec((tm, tn), lambda i, j: (i, j)),
                  gate_spec],
        out_specs=pl.BlockSpec((tm, tn), lambda i, j: (i, j)),
        compiler_params=_cparams(("arbitrary", "arbitrary")),
        name="outproj",
    )(ma, mb, w_bf16, w_bf16, x2d, mod)


def _gla_kernel(q_ref, k_ref, v_ref, ra_ref, g_ref, wa2_ref, ba_ref, ng_ref,
                o_ref, st_ref, stt_ref, la_ref, b_ref, *, n_sub, dk, dv):
    c = pl.program_id(1)
    cs = GLA_CHUNK

    @pl.when(c == 0)
    def _():
        stt_ref[...] = jnp.zeros_like(stt_ref)

    row = lax.broadcasted_iota(jnp.int32, (cs, cs), 0)
    col = lax.broadcasted_iota(jnp.int32, (cs, cs), 1)
    causal = row >= col
    tri = jnp.where(causal, 1.0, 0.0).astype(BF16)
    qscale = np.float32(dk ** -0.5)

    pre = _dot(ra_ref[...].astype(BF16), wa2_ref[...]) + ba_ref[...]
    la = (jnp.minimum(pre, 0.0) - _softplus_neg_abs(pre)) * np.float32(1.0 / GLA_TAU)
    la_ref[...] = la
    safe = jnp.min(la) >= np.float32(-GLA_SAFE_LOG_DECAY / cs)

    def pairwise_scores(qs, kb, hs):
        b = b_ref[:, hs]
        sublanes = 8

        def key_group(gi, acc):
            s0 = pl.multiple_of(gi * sublanes, sublanes)
            bk = b_ref[pl.ds(s0, sublanes), hs]
            for r in range(sublanes):
                e = jnp.exp(jnp.minimum(b - bk[r:r + 1, :], 0.0))
                acc = jnp.where(col == s0 + r, _dot_nt((qs * e).astype(BF16), kb), acc)
            return acc

        return lax.fori_loop(0, cs // sublanes, key_group, jnp.zeros((cs, cs), F32))

    def make_body(factorised):
        def body(i, carry):
            r0 = pl.multiple_of(i * cs, cs)
            rows = pl.ds(r0, cs)
            b = _split_dot(la_ref[rows, :], tri, left=True)
            bl = b[cs - 1:cs, :]
            k = k_ref[rows, :].astype(F32)
            qs = q_ref[rows, :].astype(F32) * qscale
            qe = (qs * jnp.exp(b)).astype(BF16)
            kd = (k * jnp.exp(bl - b)).astype(BF16)
            ebl = jnp.exp(bl)
            if factorised:
                ke = (k * jnp.exp(-b)).astype(BF16)
            else:
                b_ref[...] = b
            for h in range(HA):
                ks, vs = slice(h * dk, (h + 1) * dk), slice(h * dv, (h + 1) * dv)
                v = v_ref[rows, vs].astype(BF16)
                if factorised:
                    s = _dot_nt(qe[:, ks], ke[:, ks])
                else:
                    s = pairwise_scores(qs[:, ks], k[:, ks].astype(BF16), ks)
                s = jnp.where(causal, s, 0.0)
                stt = stt_ref[h]
                o = _dot(s.astype(BF16), v) + _dot_nt(qe[:, ks], stt.astype(BF16))
                stt_ref[h] = stt * ebl[:, ks] + _dot_tn(v, kd[:, ks])
                og = _rms(o, ng_ref[:, vs]) * _silu(g_ref[rows, vs].astype(F32))
                o_ref[rows, vs] = og.astype(o_ref.dtype)
            return carry
        return body

    @pl.when(safe)
    def _():
        lax.fori_loop(0, n_sub, make_body(True), 0, unroll=True)

    @pl.when(jnp.logical_not(safe))
    def _():
        lax.fori_loop(0, n_sub, make_body(False), 0)

    @pl.when(c == pl.num_programs(1) - 1)
    def _():
        for h in range(HA):
            st_ref[h] = stt_ref[h].T


def _gla_prompt(y3, wa2_pad, b_a, norm_g, tc):
    bsz, t, n = y3.shape
    dk, dv = LANES, 2 * LANES
    wk, wv = HA * dk, HA * dv
    kern = functools.partial(_gla_kernel, n_sub=tc // GLA_CHUNK, dk=dk, dv=dv)
    return pl.pallas_call(
        kern,
        out_shape=(jax.ShapeDtypeStruct((bsz, t, wv), BF16),
                   jax.ShapeDtypeStruct((bsz, HA, dk, dv), F32)),
        grid=(bsz, t // tc),
        in_specs=[pl.BlockSpec((None, tc, wk), lambda b, c: (b, c, 0)),
                  pl.BlockSpec((None, tc, wk), lambda b, c: (b, c, 1)),
                  pl.BlockSpec((None, tc, wv), lambda b, c: (b, c, 1)),
                  pl.BlockSpec((None, tc, LANES), lambda b, c: (b, c, (n - MXU_DIM) // LANES)),
                  pl.BlockSpec((None, tc, wv), lambda b, c: (b, c, 2)),
                  pl.BlockSpec((LANES, wk), lambda b, c: (0, 0)),
                  pl.BlockSpec((1, wk), lambda b, c: (0, 0)),
                  pl.BlockSpec((1, wv), lambda b, c: (0, 0))],
        out_specs=(pl.BlockSpec((None, tc, wv), lambda b, c: (b, c, 0)),
                   pl.BlockSpec((None, HA, dk, dv), lambda b, c: (b, 0, 0, 0))),
        scratch_shapes=[pltpu.VMEM((HA, dv, dk), F32), pltpu.VMEM((tc, wk), F32),
                        pltpu.VMEM((GLA_CHUNK, wk), F32)],
        compiler_params=_cparams(("arbitrary", "arbitrary")),
        name="gla_prompt",
    )(y3, y3, y3, y3, y3, wa2_pad, b_a.reshape(1, -1), norm_g.reshape(1, -1))


def _rot(x, cosf, sinf):
    return x * cosf + pltpu.roll(x, x.shape[-1] // 2, 1) * sinf


def _ret_kernel(q_ref, k_ref, v_ref, g_ref, cos_ref, sin_ref, lg_ref, ng_ref,
                o_ref, st_ref, stt_ref, *, n_sub, dk, dv):
    c = pl.program_id(1)
    cs = RET_CHUNK

    @pl.when(c == 0)
    def _():
        stt_ref[...] = jnp.zeros_like(stt_ref)

    row = lax.broadcasted_iota(jnp.int32, (cs, cs), 0)
    col = lax.broadcasted_iota(jnp.int32, (cs, cs), 1)
    rel = (row - col).astype(F32)
    ridx = lax.broadcasted_iota(jnp.int32, (cs, 1), 0).astype(F32)
    lgs = [lg_ref[:, h * dk:h * dk + 1] for h in range(HB)]
    decay = [jnp.where(row >= col, jnp.exp(rel * lg), 0.0) for lg in lgs]
    inter = [jnp.exp((ridx + 1.0) * lg) for lg in lgs]
    kdec = [jnp.exp((np.float32(cs - 1.0) - ridx) * lg) for lg in lgs]
    sdec = [jnp.exp(np.float32(cs) * lg) for lg in lgs]
    qscale = np.float32(dk ** -0.5)

    def body(i, carry):
        r0 = pl.multiple_of(i * cs, cs)
        rows = pl.ds(r0, cs)
        cosf = cos_ref[rows, :]
        sinf = sin_ref[rows, :]
        for h in range(HB):
            ks, vs = slice(h * dk, (h + 1) * dk), slice(h * dv, (h + 1) * dv)
            qr = (_rot(q_ref[rows, ks].astype(F32), cosf, sinf) * qscale).astype(BF16)
            kr = _rot(k_ref[rows, ks].astype(F32), cosf, sinf)
            v = v_ref[rows, vs].astype(BF16)
            s = _dot_nt(qr, kr.astype(BF16)) * decay[h]
            stt = stt_ref[h]
            o = _dot(s.astype(BF16), v) + _dot_nt(qr, stt.astype(BF16)) * inter[h]
            stt_ref[h] = stt * sdec[h] + _dot_tn(v, (kr * kdec[h]).astype(BF16))
            og = _rms(o, ng_ref[:, vs]) * _silu(g_ref[rows, vs].astype(F32))
            o_ref[rows, vs] = og.astype(o_ref.dtype)
        return carry

    lax.fori_loop(0, n_sub, body, 0, unroll=True)

    @pl.when(c == pl.num_programs(1) - 1)
    def _():
        for h in range(HB):
            st_ref[h] = stt_ref[h].T


def _ret_prompt(y3, cosf, sinf, lg_row, norm_g, col0, tc):
    bsz, t, _ = y3.shape
    dk, dv = LANES, 2 * LANES
    wk, wv = HB * dk, HB * dv
    cq, cv = col0 // wk, (col0 + 2 * wk) // wv
    kern = functools.partial(_ret_kernel, n_sub=tc // RET_CHUNK, dk=dk, dv=dv)
    return pl.pallas_call(
        kern,
        out_shape=(jax.ShapeDtypeStruct((bsz, t, wv), BF16),
                   jax.ShapeDtypeStruct((bsz, HB, dk, dv), F32)),
        grid=(bsz, t // tc),
        in_specs=[pl.BlockSpec((None, tc, wk), lambda b, c: (b, c, cq)),
                  pl.BlockSpec((None, tc, wk), lambda b, c: (b, c, cq + 1)),
                  pl.BlockSpec((None, tc, wv), lambda b, c: (b, c, cv)),
                  pl.BlockSpec((None, tc, wv), lambda b, c: (b, c, cv + 1)),
                  pl.BlockSpec((tc, dk), lambda b, c: (c, 0)),
                  pl.BlockSpec((tc, dk), lambda b, c: (c, 0)),
                  pl.BlockSpec((1, wk), lambda b, c: (0, 0)),
                  pl.BlockSpec((1, wv), lambda b, c: (0, 0))],
        out_specs=(pl.BlockSpec((None, tc, wv), lambda b, c: (b, c, 0)),
                   pl.BlockSpec((None, HB, dk, dv), lambda b, c: (b, 0, 0, 0))),
        scratch_shapes=[pltpu.VMEM((HB, dv, dk), F32)],
        compiler_params=_cparams(("arbitrary", "arbitrary")),
        name="ret_prompt",
    )(y3, y3, y3, y3, cosf, sinf, lg_row, norm_g.reshape(1, -1))


def _row_to_col(x_row):
    n = x_row.shape[-1]
    r = lax.broadcasted_iota(jnp.int32, (n, n), 0)
    c = lax.broadcasted_iota(jnp.int32, (n, n), 1)
    return jnp.sum(jnp.where(r == c, jnp.broadcast_to(x_row, (n, n)), 0.0), axis=1, keepdims=True)


def _even_decode_kernel(y_ref, sg_ref, sr_ref, wa2_ref, ba_ref, gng_ref, rng_ref, cos_ref, sin_ref,
                        lg_ref, o_ref, sgo_ref, sro_ref, *, cols):
    dk, dv = LANES, 2 * LANES
    cqa, cka, cva, cga, cqb, ckb, cvb, cgb, cra = cols
    y = y_ref[...]
    pre = _dot(y[:, cra:cra + LANES].astype(BF16), wa2_ref[...]) + ba_ref[...]
    la = (jnp.minimum(pre, 0.0) - _softplus_neg_abs(pre)) * np.float32(1.0 / GLA_TAU)
    alpha = jnp.exp(la)
    cosf = cos_ref[...]
    sinf = sin_ref[...]
    gam = jnp.exp(lg_ref[...])
    outs = []
    for h in range(HA):
        q = y[:, cqa + h * dk:cqa + (h + 1) * dk] * np.float32(dk ** -0.5)
        k = y[:, cka + h * dk:cka + (h + 1) * dk]
        v = y[:, cva + h * dv:cva + (h + 1) * dv]
        s_new = _row_to_col(alpha[:, h * dk:(h + 1) * dk]) * sg_ref[h] + _row_to_col(k) * v
        sgo_ref[h] = s_new
        o = jnp.sum(_row_to_col(q) * s_new, axis=0, keepdims=True)
        g = y[:, cga + h * dv:cga + (h + 1) * dv]
        outs.append(_rms(o, gng_ref[:, h * dv:(h + 1) * dv]) * _silu(g))
    for h in range(HB):
        q = _rot(y[:, cqb + h * dk:cqb + (h + 1) * dk], cosf, sinf) * np.float32(dk ** -0.5)
        k = _rot(y[:, ckb + h * dk:ckb + (h + 1) * dk], cosf, sinf)
        v = y[:, cvb + h * dv:cvb + (h + 1) * dv]
        s_new = _row_to_col(gam[:, h * dk:(h + 1) * dk]) * sr_ref[h] + _row_to_col(k) * v
        sro_ref[h] = s_new
        o = jnp.sum(_row_to_col(q) * s_new, axis=0, keepdims=True)
        g = y[:, cgb + h * dv:cgb + (h + 1) * dv]
        outs.append(_rms(o, rng_ref[:, h * dv:(h + 1) * dv]) * _silu(g))
    o_ref[...] = jnp.concatenate(outs, axis=-1).astype(o_ref.dtype)


def _even_decode(y_s, state_gla, state_ret, wa2_pad, b_a, gla_ng, ret_ng, cos_row, sin_row, lg_row, cols):
    bsz, n = y_s.shape
    dk, dv = LANES, 2 * LANES
    wide = (HA + HB) * dv
    kern = functools.partial(_even_decode_kernel, cols=cols)
    full = lambda a: pl.BlockSpec(a.shape, lambda b: (0,) * a.ndim)
    b_a2, gng, rng_ = b_a.reshape(1, -1), gla_ng.reshape(1, -1), ret_ng.reshape(1, -1)
    mixed, sg, sr = pl.pallas_call(
        kern,
        out_shape=(jax.ShapeDtypeStruct((bsz, 1, wide), BF16),
                   jax.ShapeDtypeStruct(state_gla.shape, F32),
                   jax.ShapeDtypeStruct(state_ret.shape, F32)),
        grid=(bsz,),
        in_specs=[pl.BlockSpec((None, 1, n), lambda b: (b, 0, 0)),
                  pl.BlockSpec((None, HA, dk, dv), lambda b: (b, 0, 0, 0)),
                  pl.BlockSpec((None, HB, dk, dv), lambda b: (b, 0, 0, 0)),
                  full(wa2_pad), full(b_a2), full(gng), full(rng_), full(cos_row), full(sin_row), full(lg_row)],
        out_specs=(pl.BlockSpec((None, 1, wide), lambda b: (b, 0, 0)),
                   pl.BlockSpec((None, HA, dk, dv), lambda b: (b, 0, 0, 0)),
                   pl.BlockSpec((None, HB, dk, dv), lambda b: (b, 0, 0, 0))),
        compiler_params=_cparams(("arbitrary",)),
        name="even_decode",
    )(y_s.reshape(bsz, 1, n), state_gla, state_ret, wa2_pad, b_a2, gng, rng_, cos_row, sin_row, lg_row)
    return mixed.reshape(bsz, wide), sg, sr


def _qkv_kernel(q_ref, k_ref, v_ref, qg_ref, kg_ref, qn_ref, kn_ref, knb_ref, vo_ref):
    dc = qg_ref.shape[-1]
    qscale = np.float32(dc ** -0.5) * LOG2E
    for h in range(q_ref.shape[-1] // dc):
        hs = slice(h * dc, (h + 1) * dc)
        qn_ref[:, hs] = (_rms(q_ref[:, hs].astype(F32), qg_ref[...]) * qscale).astype(BF16)
        kn = _rms(k_ref[:, hs].astype(F32), kg_ref[...])
        kn_ref[:, hs] = kn
        knb_ref[:, hs] = kn.astype(BF16)
    vo_ref[...] = v_ref[...].astype(F32)


def _qkv_prompt(y2, q_g, k_g, tm):
    m = y2.shape[0]
    dc = LANES
    wide = HC * dc
    blk = lambda c: pl.BlockSpec((tm, wide), lambda i: (i, c))
    gspec = pl.BlockSpec((1, dc), lambda i: (0, 0))
    return pl.pallas_call(
        _qkv_kernel,
        out_shape=(jax.ShapeDtypeStruct((m, wide), BF16),
                   jax.ShapeDtypeStruct((m, wide), F32),
                   jax.ShapeDtypeStruct((m, wide), BF16),
                   jax.ShapeDtypeStruct((m, wide), F32)),
        grid=(m // tm,),
        in_specs=[blk(0), blk(1), blk(2), gspec, gspec],
        out_specs=(blk(0), blk(0), blk(0), blk(0)),
        compiler_params=_cparams(("arbitrary",)),
        name="qkv_prompt",
    )(y2, y2, y2, q_g.reshape(1, dc), k_g.reshape(1, dc))


def _sb_kernel(qi_ref, kb_ref, q_ref, k_ref, v_ref, g_ref, qone_ref, koff_ref, o_ref, acc_ref, run_ref,
               *, sub, hp):
    p = pl.program_id(2)
    qi = qi_ref[p]
    kb = kb_ref[p]
    tq = q_ref.shape[0]
    dc = q_ref.shape[1] // hp

    @pl.when(kb == qi)
    def _():
        acc_ref[...] = jnp.zeros_like(acc_ref)
        run_ref[...] = jnp.zeros_like(run_ref)

    def block(masked):
        r = lax.broadcasted_iota(jnp.int32, (sub, sub), 0)
        c = lax.broadcasted_iota(jnp.int32, (sub, sub), 1)
        after = jnp.where(r > c, 1.0, 0.0).astype(BF16)
        units = [(hh, s_i, s_i * sub if masked else 0) for hh in range(hp) for s_i in reversed(range(tq // sub))]
        zs, masks, log_betas, rests, firsts, laters, ws = [], [], [], [], [], [], []
        for hh, s_i, r0 in units:
            hs = slice(hh * dc, (hh + 1) * dc)
            qa = jnp.concatenate([q_ref[r0:, hs], qone_ref[r0:, :]], axis=1)
            ka = jnp.concatenate([k_ref[s_i * sub:(s_i + 1) * sub, hs], koff_ref[hh]], axis=1)
            zs.append(_dot_nt(qa, ka))
        for z in zs:
            rest = _neg_log2_rest(z)
            log_betas.append(z - rest)
            if masked:
                mask = (lax.broadcasted_iota(jnp.int32, z.shape, 1)
                        < lax.broadcasted_iota(jnp.int32, z.shape, 0))
                rest = jnp.where(mask, rest, 0.0)
                masks.append(mask)
            firsts.append(rest[:, 0:1])
            rests.append(rest.astype(BF16))
        for rb in rests:
            laters.append(_dot(rb, after))
        for u in range(len(units)):
            w = jnp.exp2(log_betas[u] - laters[u])
            if masked:
                w = jnp.where(masks[u], w, 0.0)
            ws.append(w.astype(BF16))
        for u, (hh, s_i, r0) in enumerate(units):
            hs = slice(hh * dc, (hh + 1) * dc)
            run = run_ref[hh, r0:, :]
            acc_ref[r0:, hs] += jnp.exp2(-run) * _dot(ws[u], v_ref[s_i * sub:(s_i + 1) * sub, hs])
            run_ref[hh, r0:, :] = run + (laters[u][:, 0:1] + firsts[u])

    pl.when(kb == qi)(functools.partial(block, True))
    pl.when(kb < qi)(functools.partial(block, False))

    @pl.when(kb == 0)
    def _():
        o_ref[...] = (acc_ref[...] * _silu(g_ref[...].astype(F32))).astype(o_ref.dtype)


def _sb_prompt(qn, kn, y3, sb_offset, cv, cg, tq, sub, hp):
    bsz, t, _ = qn.shape
    dc = LANES
    wide = hp * dc
    nq = t // tq
    pairs = [(qi, kb) for qi in range(nq) for kb in range(qi, -1, -1)]
    qi_tab = jnp.asarray([p[0] for p in pairs], jnp.int32)
    kb_tab = jnp.asarray([p[1] for p in pairs], jnp.int32)
    off2 = sb_offset.astype(F32) * LOG2E
    terms, rem = [], off2
    for _ in range(3):
        terms.append(rem.astype(BF16))
        rem = rem - terms[-1].astype(F32)
    lane = jnp.arange(dc)
    koff = sum(jnp.where(lane == n, tm[:, None, None], 0).astype(BF16) for n, tm in enumerate(terms))
    koff = jnp.broadcast_to(koff, (HC, sub, dc))
    qone = jnp.broadcast_to(jnp.where(lane < len(terms), 1, 0).astype(BF16), (tq, dc))
    kern = functools.partial(_sb_kernel, sub=sub, hp=hp)
    q_map = lambda b, h, p, qi, kb: (b, qi[p], h)
    k_map = lambda b, h, p, qi, kb: (b, kb[p], h)
    return pl.pallas_call(
        kern,
        out_shape=jax.ShapeDtypeStruct((bsz, t, HC * dc), BF16),
        grid_spec=pltpu.PrefetchScalarGridSpec(
            num_scalar_prefetch=2,
            grid=(bsz, HC // hp, len(pairs)),
            in_specs=[pl.BlockSpec((None, tq, wide), q_map),
                      pl.BlockSpec((None, tq, wide), k_map),
                      pl.BlockSpec((None, tq, wide), lambda b, h, p, qi, kb: (b, kb[p], cv // wide + h)),
                      pl.BlockSpec((None, tq, wide), lambda b, h, p, qi, kb: (b, qi[p], cg // wide + h)),
                      pl.BlockSpec((tq, dc), lambda b, h, p, qi, kb: (0, 0)),
                      pl.BlockSpec((hp, sub, dc), lambda b, h, p, qi, kb: (h, 0, 0))],
            out_specs=pl.BlockSpec((None, tq, wide), q_map),
            scratch_shapes=[pltpu.VMEM((tq, wide), F32), pltpu.VMEM((hp, tq, 1), F32)]),
        compiler_params=_cparams(("arbitrary",) * 3),
        name="sb_prompt",
    )(qi_tab, kb_tab, qn, kn, y3, y3, qone, koff)


def _cmlp_kernel(u_ref, v_ref, g_ref, vg_ref, ws_ref, bs_ref, o_ref, *, n_chunk):
    cs = D_CHUNK
    u = _gelu(u_ref[...].astype(F32))
    vn = _rms(_gelu(v_ref[...].astype(F32)), vg_ref[...]).astype(BF16)
    gate = _silu(g_ref[...].astype(F32))
    bs = bs_ref[...]
    r = lax.broadcasted_iota(jnp.int32, (cs, cs), 0)
    c = lax.broadcasted_iota(jnp.int32, (cs, cs), 1)
    for g in range(DG):
        w = jnp.where(r >= c, ws_ref[g], 0.0).astype(BF16)
        lo, hi = g * LANES, (g + 1) * LANES
        for ch in range(n_chunk):
            r0, r1 = ch * cs, (ch + 1) * cs
            mixed = _dot(w, vn[r0:r1, lo:hi]) + bs[:, lo:hi]
            o_ref[r0:r1, lo:hi] = (u[r0:r1, lo:hi] * mixed * gate[r0:r1, lo:hi]).astype(o_ref.dtype)


def _cmlp_prompt(y3, v_norm_g, w_s, bs_full, cols, tm):
    bsz, t, _ = y3.shape
    br = DG * LANES
    cu, cv, cg = cols
    kern = functools.partial(_cmlp_kernel, n_chunk=tm // D_CHUNK)
    return pl.pallas_call(
        kern,
        out_shape=jax.ShapeDtypeStruct((bsz, t, br), BF16),
        grid=(bsz, t // tm),
        in_specs=[pl.BlockSpec((None, tm, br), lambda b, i: (b, i, cu)),
                  pl.BlockSpec((None, tm, br), lambda b, i: (b, i, cv)),
                  pl.BlockSpec((None, tm, br), lambda b, i: (b, i, cg)),
                  pl.BlockSpec((1, br), lambda b, i: (0, 0)),
                  pl.BlockSpec((DG, D_CHUNK, D_CHUNK), lambda b, i: (0, 0, 0)),
                  pl.BlockSpec((D_CHUNK, br), lambda b, i: (0, 0))],
        out_specs=pl.BlockSpec((None, tm, br), lambda b, i: (b, i, 0)),
        compiler_params=_cparams(("arbitrary", "arbitrary")),
        name="cmlp_prompt",
    )(y3, y3, y3, v_norm_g.reshape(1, br), w_s, bs_full)


def _odd_decode_kernel(y_ref, qg_ref, kg_ref, vg_ref, w0_ref, b0_ref, qn_ref, kn_ref, vo_ref, cv_ref, od_ref):
    br = HC * LANES
    y = y_ref[...]
    for h in range(HC):
        lo, hi = h * LANES, (h + 1) * LANES
        qn_ref[:, lo:hi] = _rms(y[:, lo:hi], qg_ref[...])
        kn_ref[:, lo:hi] = _rms(y[:, br + lo:br + hi], kg_ref[...])
    vo_ref[...] = y[:, 2 * br:3 * br]
    u = _gelu(y[:, 4 * br:5 * br])
    vn = _rms(_gelu(y[:, 5 * br:6 * br]), vg_ref[...])
    cv_ref[...] = vn
    od = u * (w0_ref[...] * vn + b0_ref[...]) * _silu(y[:, 6 * br:7 * br])
    od_ref[...] = od.astype(od_ref.dtype)


def _odd_decode(y_s, q_g, k_g, v_norm_g, w0_row, b0_row):
    s = y_s.shape[0]
    br = HC * LANES
    args = (y_s, q_g.reshape(1, -1), k_g.reshape(1, -1), v_norm_g.reshape(1, -1), w0_row, b0_row)
    full = lambda a: pl.BlockSpec(a.shape, lambda i: (0,) * a.ndim)
    osd = lambda dt: jax.ShapeDtypeStruct((s, br), dt)
    ospec = pl.BlockSpec((s, br), lambda i: (0, 0))
    return pl.pallas_call(
        _odd_decode_kernel,
        out_shape=(osd(F32), osd(F32), osd(F32), osd(F32), osd(BF16)),
        grid=(1,),
        in_specs=[full(a) for a in args],
        out_specs=(ospec,) * 5,
        compiler_params=_cparams(("arbitrary",)),
        name="odd_decode",
    )(*args)


def _strided_suffix_sums(x, stride, axis):
    n = x.shape[axis]
    idx = lax.broadcasted_iota(jnp.int32, x.shape, axis)
    inc, tot = x, x
    k = stride
    while k < n:
        inc = inc + jnp.where(idx < n - k, pltpu.roll(inc, n - k, axis), 0.0)
        tot = tot + pltpu.roll(tot, n - k, axis)
        k *= 2
    return inc, tot


def _paged_kernel(pt_ref, q_ref, g_ref, off_ref, *refs, n_pp):
    k_refs, v_refs = refs[:n_pp], refs[n_pp:2 * n_pp]
    o_ref, acc_ref, run_ref = refs[2 * n_pp:]
    j = pl.program_id(1)
    page, hc, dc = k_refs[0].shape
    rows = page * hc
    lane = lax.broadcasted_iota(jnp.int32, (hc, rows), 1)
    own = (lane & (hc - 1)) == lax.broadcasted_iota(jnp.int32, (hc, rows), 0)

    @pl.when(j == 0)
    def _():
        acc_ref[...] = jnp.zeros_like(acc_ref)
        run_ref[...] = jnp.zeros_like(run_ref)

    q = (q_ref[...] * (np.float32(dc ** -0.5) * LOG2E)).astype(BF16)
    off = off_ref[...] * LOG2E
    prow = lax.broadcasted_iota(jnp.int32, (n_pp, rows), 0)
    z = jnp.zeros((n_pp, rows), F32)
    for p in range(n_pp):
        kf = k_refs[p][...].reshape(rows, dc).astype(BF16)
        s = _dot_nt(q, kf)
        z = jnp.where(prow == p, jnp.sum(jnp.where(own, s, 0.0), axis=0, keepdims=True), z)
    z = z + off
    incl, tot = _strided_suffix_sums(_neg_log2_rest(z), hc, 1)
    pages_incl, _ = _strided_suffix_sums(tot, 1, 0)
    run = run_ref[...]
    w = jnp.exp2(z - incl - (pages_incl - tot) - run)
    run_ref[...] = run + pages_incl[0:1, :]
    acc = acc_ref[...]
    for p in range(n_pp):
        wm = jnp.where(own, jnp.broadcast_to(w[p:p + 1, :], (hc, rows)), 0.0).astype(BF16)
        acc = acc + _dot(wm, v_refs[p][...].reshape(rows, dc).astype(BF16))
    acc_ref[...] = acc

    @pl.when(j == pl.num_programs(1) - 1)
    def _():
        o_ref[...] = (acc_ref[...] * _silu(g_ref[...])).astype(o_ref.dtype)


def _paged_attention(qn_s, gate_s, cache_k, cache_v, layer, page_table, sb_offset):
    s, hc, dc = qn_s.shape
    n_pages = page_table.shape[1]
    page = cache_k.shape[2]
    n_pp = PAGES_PER_STEP
    assert hc & (hc - 1) == 0 and n_pages % n_pp == 0
    rows = page * hc

    def kv_spec(p):
        return pl.BlockSpec((None, None, page, hc, dc),
                            lambda b, j, pt: (layer, pt[b, n_pages - (j + 1) * n_pp + p], 0, 0, 0))

    head_blk = pl.BlockSpec((None, hc, dc), lambda b, j, pt: (b, 0, 0))
    return pl.pallas_call(
        functools.partial(_paged_kernel, n_pp=n_pp),
        out_shape=jax.ShapeDtypeStruct((s, hc, dc), BF16),
        grid_spec=pltpu.PrefetchScalarGridSpec(
            num_scalar_prefetch=1,
            grid=(s, n_pages // n_pp),
            in_specs=[head_blk, head_blk, pl.BlockSpec((1, rows), lambda b, j, pt: (0, 0))]
            + [kv_spec(p) for p in range(n_pp)] * 2,
            out_specs=head_blk,
            scratch_shapes=[pltpu.VMEM((hc, dc), F32), pltpu.VMEM((1, rows), F32)]),
        compiler_params=_cparams(("arbitrary", "arbitrary")),
        name="paged_attention",
    )(page_table, qn_s, gate_s, jnp.tile(sb_offset, page).reshape(1, rows),
      *([cache_k] * n_pp), *([cache_v] * n_pp))


def _rope_tables(pos, half):
    inv = ROPE_BASE ** (-jnp.arange(half, dtype=F32) / half)
    ang = pos.astype(F32)[:, None] * inv[None, :]
    cos, sin = jnp.cos(ang), jnp.sin(ang)
    return jnp.concatenate([cos, cos], axis=-1), jnp.concatenate([-sin, sin], axis=-1)


def kernel(x_prompt, x_sample, state_gla, state_ret, cache_k, cache_v, page_table, c_prompt, c_sample,
           e_norm_g, e_w_ada, e_b_ada, e_w_in, e_w_a2, e_b_a, e_gla_norm_g, e_ret_norm_g, e_w_out,
           o_norm_g, o_w_ada, o_b_ada, o_w_in, o_q_norm_g, o_k_norm_g, o_sb_offset, o_v_norm_g,
           o_w_s, o_b_s, o_w_out):
    bsz, t, d = x_prompt.shape
    s = x_sample.shape[0]
    br = d // 2
    n_past = page_table.shape[1] * cache_k.shape[2]
    dk = LANES

    c_rows = jnp.concatenate([c_sample, c_prompt, jnp.zeros((16 - s - bsz, d), F32)], axis=0)

    n_qkv = 2 * HA * dk + HA * 2 * dk
    n_main = e_w_in.shape[2] - GLA_RANK
    col_ret = n_qkv + HA * 2 * dk
    cols_dec = (0, HA * dk, 2 * HA * dk, n_qkv,
                col_ret, col_ret + HB * dk, col_ret + 2 * HB * dk, col_ret + 2 * HB * dk + HB * 2 * dk,
                n_main)

    xp = x_prompt.reshape(bsz * t, d)
    xs = x_sample.reshape(s, d)
    pos_p = jnp.arange(t)
    pos_s = n_past + jnp.arange(x_sample.shape[1])
    cos_p, sin_p = _rope_tables(pos_p, dk // 2)
    cos_s, sin_s = _rope_tables(pos_s, dk // 2)
    log_gamma = jnp.log1p(-jnp.exp2(-5.0 - jnp.arange(HB, dtype=F32)))
    lg_row = jnp.repeat(log_gamma, dk).reshape(1, HB * dk)

    i = 0
    mod = _ada(c_rows, e_w_ada[i], e_b_ada[i])
    mod_s, mod_p = mod[:s], mod[s:s + bsz].reshape(bsz, 1, 3 * d)
    w_in_b = _reorder_cast(jnp.swapaxes(e_w_in, 1, 2), i, n_qkv, GLA_RANK, n_main + MXU_DIM, 256)
    wa2_pad = jnp.concatenate([e_w_a2[i], jnp.zeros((LANES - GLA_RANK, HA * dk), F32)], axis=0).astype(BF16)
    w_out_b = e_w_out[i].astype(BF16)

    y = _proj(xp, mod_p, e_norm_g[i], w_in_b, t, 1024, 1280, w_transposed=True)
    y3 = y.reshape(bsz, t, -1)
    oa, gla_p = _gla_prompt(y3, wa2_pad, e_b_a[i], e_gla_norm_g[i], 512)
    ob, ret_p = _ret_prompt(y3, cos_p, sin_p, lg_row, e_ret_norm_g[i], col_ret, 512)
    xp = _outproj(oa.reshape(bsz * t, br), 0, ob.reshape(bsz * t, br), 0, w_out_b, xp, mod_p, t, 512, d)

    y_s = _proj(xs, mod_s, e_norm_g[i], w_in_b, None, s, 1280, w_transposed=True)
    mixed_s, gla_s, ret_s = _even_decode(y_s, state_gla[i], state_ret[i], wa2_pad, e_b_a[i],
                                         e_gla_norm_g[i], e_ret_norm_g[i], cos_s, sin_s, lg_row, cols_dec)
    xs = _outproj(mixed_s, 0, mixed_s, 1, w_out_b, xs, mod_s, None, s, 1024)

    mod = _ada(c_rows, o_w_ada[i], o_b_ada[i])
    mod_s, mod_p = mod[:s], mod[s:s + bsz].reshape(bsz, 1, 3 * d)
    w_in_b = o_w_in[i].astype(BF16)
    w_out_b = o_w_out[i].astype(BF16)
    bs_full = jnp.repeat(jnp.transpose(o_b_s[i]), LANES, axis=1)

    y = _proj(xp, mod_p, o_norm_g[i], w_in_b, t, 1024, 1024)
    y3 = y.reshape(bsz, t, -1)
    qn, kn, knb, vo = _qkv_prompt(y, o_q_norm_g[i], o_k_norm_g[i], 512)
    shp = (bsz, t, br)
    oc = _sb_prompt(qn.reshape(shp), knb.reshape(shp), y3, o_sb_offset[i], 2 * br, 3 * br, 512, 256, 4)
    od = _cmlp_prompt(y3, o_v_norm_g[i], o_w_s[i], bs_full, (4, 5, 6), 256)
    xp = _outproj(oc.reshape(bsz * t, br), 0, od.reshape(bsz * t, br), 0, w_out_b, xp, mod_p, t, 512, d)

    y_s = _proj(xs, mod_s, o_norm_g[i], w_in_b, None, s, 1024)
    w0_row = jnp.repeat(o_w_s[i][:, 0, 0], LANES).reshape(1, br)
    b0_row = jnp.repeat(o_b_s[i][:, 0], LANES).reshape(1, br)
    qn_s, kn_s, vo_s, cv_s, od_s = _odd_decode(y_s, o_q_norm_g[i], o_k_norm_g[i], o_v_norm_g[i], w0_row, b0_row)
    dc = LANES
    oc_s = _paged_attention(qn_s.reshape(s, HC, dc), y_s[:, 3 * br:4 * br].reshape(s, HC, dc),
                            cache_k, cache_v, i, page_table, o_sb_offset[i]).reshape(s, br)
    xs = _outproj(oc_s, 0, od_s, 0, w_out_b, xs, mod_s, None, s, 1024)

    return (xp.reshape(bsz, t, d), xs.reshape(s, 1, d),
            gla_p[None], gla_s[None], ret_p[None], ret_s[None],
            kn.reshape(1, bsz, t, HC, dc), vo.reshape(1, bsz, t, HC, dc),
            kn_s.reshape(1, s, 1, HC, dc), vo_s.reshape(1, s, 1, HC, dc),
            cv_s.reshape(1, s, 1, br))
```

```python
import functools

import jax
import jax.numpy as jnp
import numpy as np
from jax import lax
from jax.experimental import pallas as pl
from jax.experimental.pallas import tpu as pltpu

F32 = jnp.float32
BF16 = jnp.bfloat16

HA = 4
HB = 4
HC = 8
DG = 8
GLA_RANK = 16
GLA_TAU = 16.0
GLA_CHUNK = 64
GLA_SAFE_LOG_DECAY = 40.0
RET_CHUNK = 128
ROPE_BASE = 10000.0
D_CHUNK = 128
RMS_EPS = 1e-6

LANES = 128
MXU_DIM = 256
VMEM_LIMIT = 56 * 1024 * 1024
LOG2E = np.float32(np.log2(np.e))
PAGES_PER_STEP = 8


def _cparams(sem):
    return pltpu.CompilerParams(dimension_semantics=sem, vmem_limit_bytes=VMEM_LIMIT)


def _dot(a, b):
    return jnp.dot(a, b, preferred_element_type=F32)


def _dot_nt(a, b):
    return lax.dot_general(a, b, (((1,), (1,)), ((), ())), preferred_element_type=F32)


def _dot_tn(a, b):
    return lax.dot_general(a, b, (((0,), (0,)), ((), ())), preferred_element_type=F32)


def _sigmoid(x):
    return 1.0 / (1.0 + jnp.exp(-x))


def _silu(x):
    return x * _sigmoid(x)


def _gelu(x):
    c = np.float32(np.sqrt(2.0 / np.pi))
    return 0.5 * x * (1.0 + jnp.tanh(c * (x + 0.044715 * (x * x * x))))


def _softplus_neg_abs(x):
    return jnp.log(1.0 + jnp.exp(-jnp.abs(x)))


def _neg_log2_rest(z2):
    return jnp.maximum(z2, 0.0) + jnp.log(1.0 + jnp.exp2(-jnp.abs(z2))) * LOG2E


def _split_dot(x, ones_bf16, left=False):
    hi = x.astype(BF16)
    lo = (x - hi.astype(F32)).astype(BF16)
    if left:
        return _dot(ones_bf16, hi) + _dot(ones_bf16, lo)
    return _dot(hi, ones_bf16) + _dot(lo, ones_bf16)


def _rms(x, g):
    ms = jnp.mean(x * x, axis=-1, keepdims=True)
    return x * lax.rsqrt(ms + RMS_EPS) * g


def _ada_kernel(c_ref, w_ref, b_ref, o_ref):
    s = _silu(c_ref[...]).astype(BF16)
    o_ref[...] = _dot(s, w_ref[...].astype(BF16)) + b_ref[...]


def _ada(c_rows, w, b):
    r, d = c_rows.shape
    n = w.shape[1]
    tn = 512
    return pl.pallas_call(
        _ada_kernel,
        out_shape=jax.ShapeDtypeStruct((r, n), F32),
        grid=(n // tn,),
        in_specs=[pl.BlockSpec((r, d), lambda j: (0, 0)),
                  pl.BlockSpec((d, tn), lambda j: (0, j)),
                  pl.BlockSpec((1, tn), lambda j: (0, j))],
        out_specs=pl.BlockSpec((r, tn), lambda j: (0, j)),
        compiler_params=_cparams(("arbitrary",)),
        name="ada",
    )(c_rows, w, b.reshape(1, n))


def _reorder_cast_kernel(w_ref, o_ref, *, n_head, n_rank):
    n = w_ref.shape[0]
    tail = n - n_head - n_rank
    o_ref[:n_head, :] = w_ref[:n_head, :].astype(BF16)
    o_ref[n_head:n_head + tail, :] = w_ref[n_head + n_rank:, :].astype(BF16)
    o_ref[n_head + tail:n, :] = w_ref[n_head:n_head + n_rank, :].astype(BF16)
    o_ref[n:, :] = jnp.zeros((o_ref.shape[0] - n, o_ref.shape[1]), BF16)


def _reorder_cast(wt3, layer, n_head, n_rank, n_out, tc):
    _, n, d = wt3.shape
    return pl.pallas_call(
        functools.partial(_reorder_cast_kernel, n_head=n_head, n_rank=n_rank),
        out_shape=jax.ShapeDtypeStruct((n_out, d), BF16),
        grid=(d // tc,),
        in_specs=[pl.BlockSpec((None, n, tc), lambda c: (layer, 0, c))],
        out_specs=pl.BlockSpec((n_out, tc), lambda c: (0, c)),
        compiler_params=_cparams(("arbitrary",)),
        name="reorder_cast",
    )(wt3)


def _proj_kernel(x_ref, shift_ref, scale_ref, g_ref, w_ref, o_ref, h_ref, *, w_transposed, n_chunks):
    mm = _dot_nt if w_transposed else _dot
    j = pl.program_id(1)

    @pl.when(j == 0)
    def _():
        rc = x_ref.shape[0] // n_chunks
        per_row = shift_ref.shape[0] > 1
        for c in range(n_chunks):
            rs = slice(c * rc, (c + 1) * rc)
            scale = scale_ref[rs, :] if per_row else scale_ref[...]
            shift = shift_ref[rs, :] if per_row else shift_ref[...]
            h = (_rms(x_ref[rs, :], g_ref[...]) * (1.0 + scale) + shift).astype(BF16)
            h_ref[rs, :] = h
            o_ref[rs, :] = mm(h, w_ref[...]).astype(o_ref.dtype)

    @pl.when(j > 0)
    def _():
        o_ref[...] = mm(h_ref[...], w_ref[...]).astype(o_ref.dtype)


def _proj(x2d, mod, norm_g, w_bf16, rows_per_batch, tm, tn, w_transposed=False):
    m, d = x2d.shape
    out_dtype = F32 if rows_per_batch is None else BF16
    n = w_bf16.shape[0 if w_transposed else 1]
    w_spec = (pl.BlockSpec((tn, d), lambda i, j: (j, 0)) if w_transposed
              else pl.BlockSpec((d, tn), lambda i, j: (0, j)))
    if rows_per_batch is None:
        shift_spec = pl.BlockSpec((tm, d), lambda i, j: (i, 0))
        scale_spec = pl.BlockSpec((tm, d), lambda i, j: (i, 1))
    else:
        tpb = rows_per_batch // tm
        shift_spec = pl.BlockSpec((None, 1, d), lambda i, j: (i // tpb, 0, 0))
        scale_spec = pl.BlockSpec((None, 1, d), lambda i, j: (i // tpb, 0, 1))
    return pl.pallas_call(
        functools.partial(_proj_kernel, w_transposed=w_transposed, n_chunks=max(1, tm // MXU_DIM)),
        out_shape=jax.ShapeDtypeStruct((m, n), out_dtype),
        grid=(m // tm, n // tn),
        in_specs=[pl.BlockSpec((tm, d), lambda i, j: (i, 0)),
                  shift_spec, scale_spec,
                  pl.BlockSpec((1, d), lambda i, j: (0, 0)),
                  w_spec],
        out_specs=pl.BlockSpec((tm, tn), lambda i, j: (i, j)),
        scratch_shapes=[pltpu.VMEM((tm, d), BF16)],
        compiler_params=_cparams(("arbitrary", "arbitrary")),
        name="proj",
    )(x2d, mod, mod, norm_g.reshape(1, d), w_bf16)


def _outproj_kernel(ma_ref, mb_ref, wa_ref, wb_ref, x_ref, gate_ref, o_ref):
    mixed = _dot(ma_ref[...], wa_ref[...]) + _dot(mb_ref[...], wb_ref[...])
    o_ref[...] = x_ref[...] + gate_ref[...] * mixed


def _outproj(ma, ca, mb, cb, w_bf16, x2d, mod, rows_per_batch, tm, tn):
    m, d = x2d.shape
    half = w_bf16.shape[0] // 2
    gcol = 2 * (d // tn)
    if rows_per_batch is None:
        gate_spec = pl.BlockSpec((tm, tn), lambda i, j: (i, gcol + j))
    else:
        tpb = rows_per_batch // tm
        gate_spec = pl.BlockSpec((None, 1, tn), lambda i, j: (i // tpb, 0, gcol + j))
    return pl.pallas_call(
        _outproj_kernel,
        out_shape=jax.ShapeDtypeStruct((m, d), F32),
        grid=(m // tm, d // tn),
        in_specs=[pl.BlockSpec((tm, half), lambda i, j: (i, ca)),
                  pl.BlockSpec((tm, half), lambda i, j: (i, cb)),
                  pl.BlockSpec((half, tn), lambda i, j: (0, j)),
                  pl.BlockSpec((half, tn), lambda i, j: (1, j)),
                  pl.BlockSpec((tm, tn), lambda i, j: (i, j)),
                  gate_spec],
        out_specs=pl.BlockSpec((tm, tn), lambda i, j: (i, j)),
        compiler_params=_cparams(("arbitrary", "arbitrary")),
        name="outproj",
    )(ma, mb, w_bf16, w_bf16, x2d, mod)


def _gla_kernel(q_ref, k_ref, v_ref, ra_ref, g_ref, wa2_ref, ba_ref, ng_ref,
                o_ref, st_ref, stt_ref, la_ref, b_ref, *, n_sub, dk, dv):
    c = pl.program_id(1)
    cs = GLA_CHUNK

    @pl.when(c == 0)
    def _():
        stt_ref[...] = jnp.zeros_like(stt_ref)

    row = lax.broadcasted_iota(jnp.int32, (cs, cs), 0)
    col = lax.broadcasted_iota(jnp.int32, (cs, cs), 1)
    causal = row >= col
    tri = jnp.where(causal, 1.0, 0.0).astype(BF16)
    qscale = np.float32(dk ** -0.5)

    pre = _dot(ra_ref[...].astype(BF16), wa2_ref[...]) + ba_ref[...]
    la = (jnp.minimum(pre, 0.0) - _softplus_neg_abs(pre)) * np.float32(1.0 / GLA_TAU)
    la_ref[...] = la
    safe = jnp.min(la) >= np.float32(-GLA_SAFE_LOG_DECAY / cs)

    def pairwise_scores(qs, kb, hs):
        b = b_ref[:, hs]
        sublanes = 8

        def key_group(gi, acc):
            s0 = pl.multiple_of(gi * sublanes, sublanes)
            bk = b_ref[pl.ds(s0, sublanes), hs]
            for r in range(sublanes):
                e = jnp.exp(jnp.minimum(b - bk[r:r + 1, :], 0.0))
                acc = jnp.where(col == s0 + r, _dot_nt((qs * e).astype(BF16), kb), acc)
            return acc

        return lax.fori_loop(0, cs // sublanes, key_group, jnp.zeros((cs, cs), F32))

    def make_body(factorised):
        def body(i, carry):
            r0 = pl.multiple_of(i * cs, cs)
            rows = pl.ds(r0, cs)
            b = _split_dot(la_ref[rows, :], tri, left=True)
            bl = b[cs - 1:cs, :]
            k = k_ref[rows, :].astype(F32)
            qs = q_ref[rows, :].astype(F32) * qscale
            qe = (qs * jnp.exp(b)).astype(BF16)
            kd = (k * jnp.exp(bl - b)).astype(BF16)
            ebl = jnp.exp(bl)
            if factorised:
                ke = (k * jnp.exp(-b)).astype(BF16)
            else:
                b_ref[...] = b
            for h in range(HA):
                ks, vs = slice(h * dk, (h + 1) * dk), slice(h * dv, (h + 1) * dv)
                v = v_ref[rows, vs].astype(BF16)
                if factorised:
                    s = _dot_nt(qe[:, ks], ke[:, ks])
                else:
                    s = pairwise_scores(qs[:, ks], k[:, ks].astype(BF16), ks)
                s = jnp.where(causal, s, 0.0)
                stt = stt_ref[h]
                o = _dot(s.astype(BF16), v) + _dot_nt(qe[:, ks], stt.astype(BF16))
                stt_ref[h] = stt * ebl[:, ks] + _dot_tn(v, kd[:, ks])
                og = _rms(o, ng_ref[:, vs]) * _silu(g_ref[rows, vs].astype(F32))
                o_ref[rows, vs] = og.astype(o_ref.dtype)
            return carry
        return body

    @pl.when(safe)
    def _():
        lax.fori_loop(0, n_sub, make_body(True), 0, unroll=True)

    @pl.when(jnp.logical_not(safe))
    def _():
        lax.fori_loop(0, n_sub, make_body(False), 0)

    @pl.when(c == pl.num_programs(1) - 1)
    def _():
        for h in range(HA):
            st_ref[h] = stt_ref[h].T


def _gla_prompt(y3, wa2_pad, b_a, norm_g, tc):
    bsz, t, n = y3.shape
    dk, dv = LANES, 2 * LANES
    wk, wv = HA * dk, HA * dv
    kern = functools.partial(_gla_kernel, n_sub=tc // GLA_CHUNK, dk=dk, dv=dv)
    return pl.pallas_call(
        kern,
        out_shape=(jax.ShapeDtypeStruct((bsz, t, wv), BF16),
                   jax.ShapeDtypeStruct((bsz, HA, dk, dv), F32)),
        grid=(bsz, t // tc),
        in_specs=[pl.BlockSpec((None, tc, wk), lambda b, c: (b, c, 0)),
                  pl.BlockSpec((None, tc, wk), lambda b, c: (b, c, 1)),
                  pl.BlockSpec((None, tc, wv), lambda b, c: (b, c, 1)),
                  pl.BlockSpec((None, tc, LANES), lambda b, c: (b, c, (n - MXU_DIM) // LANES)),
                  pl.BlockSpec((None, tc, wv), lambda b, c: (b, c, 2)),
                  pl.BlockSpec((LANES, wk), lambda b, c: (0, 0)),
                  pl.BlockSpec((1, wk), lambda b, c: (0, 0)),
                  pl.BlockSpec((1, wv), lambda b, c: (0, 0))],
        out_specs=(pl.BlockSpec((None, tc, wv), lambda b, c: (b, c, 0)),
                   pl.BlockSpec((None, HA, dk, dv), lambda b, c: (b, 0, 0, 0))),
        scratch_shapes=[pltpu.VMEM((HA, dv, dk), F32), pltpu.VMEM((tc, wk), F32),
                        pltpu.VMEM((GLA_CHUNK, wk), F32)],
        compiler_params=_cparams(("arbitrary", "arbitrary")),
        name="gla_prompt",
    )(y3, y3, y3, y3, y3, wa2_pad, b_a.reshape(1, -1), norm_g.reshape(1, -1))


def _rot(x, cosf, sinf):
    return x * cosf + pltpu.roll(x, x.shape[-1] // 2, 1) * sinf


def _ret_kernel(q_ref, k_ref, v_ref, g_ref, cos_ref, sin_ref, lg_ref, ng_ref,
                o_ref, st_ref, stt_ref, *, n_sub, dk, dv):
    c = pl.program_id(1)
    cs = RET_CHUNK

    @pl.when(c == 0)
    def _():
        stt_ref[...] = jnp.zeros_like(stt_ref)

    row = lax.broadcasted_iota(jnp.int32, (cs, cs), 0)
    col = lax.broadcasted_iota(jnp.int32, (cs, cs), 1)
    rel = (row - col).astype(F32)
    ridx = lax.broadcasted_iota(jnp.int32, (cs, 1), 0).astype(F32)
    lgs = [lg_ref[:, h * dk:h * dk + 1] for h in range(HB)]
    decay = [jnp.where(row >= col, jnp.exp(rel * lg), 0.0) for lg in lgs]
    inter = [jnp.exp((ridx + 1.0) * lg) for lg in lgs]
    kdec = [jnp.exp((np.float32(cs - 1.0) - ridx) * lg) for lg in lgs]
    sdec = [jnp.exp(np.float32(cs) * lg) for lg in lgs]
    qscale = np.float32(dk ** -0.5)

    def body(i, carry):
        r0 = pl.multiple_of(i * cs, cs)
        rows = pl.ds(r0, cs)
        cosf = cos_ref[rows, :]
        sinf = sin_ref[rows, :]
        for h in range(HB):
            ks, vs = slice(h * dk, (h + 1) * dk), slice(h * dv, (h + 1) * dv)
            qr = (_rot(q_ref[rows, ks].astype(F32), cosf, sinf) * qscale).astype(BF16)
            kr = _rot(k_ref[rows, ks].astype(F32), cosf, sinf)
            v = v_ref[rows, vs].astype(BF16)
            s = _dot_nt(qr, kr.astype(BF16)) * decay[h]
            stt = stt_ref[h]
            o = _dot(s.astype(BF16), v) + _dot_nt(qr, stt.astype(BF16)) * inter[h]
            stt_ref[h] = stt * sdec[h] + _dot_tn(v, (kr * kdec[h]).astype(BF16))
            og = _rms(o, ng_ref[:, vs]) * _silu(g_ref[rows, vs].astype(F32))
            o_ref[rows, vs] = og.astype(o_ref.dtype)
        return carry

    lax.fori_loop(0, n_sub, body, 0, unroll=True)

    @pl.when(c == pl.num_programs(1) - 1)
    def _():
        for h in range(HB):
            st_ref[h] = stt_ref[h].T


def _ret_prompt(y3, cosf, sinf, lg_row, norm_g, col0, tc):
    bsz, t, _ = y3.shape
    dk, dv = LANES, 2 * LANES
    wk, wv = HB * dk, HB * dv
    cq, cv = col0 // wk, (col0 + 2 * wk) // wv
    kern = functools.partial(_ret_kernel, n_sub=tc // RET_CHUNK, dk=dk, dv=dv)
    return pl.pallas_call(
        kern,
        out_shape=(jax.ShapeDtypeStruct((bsz, t, wv), BF16),
                   jax.ShapeDtypeStruct((bsz, HB, dk, dv), F32)),
        grid=(bsz, t // tc),
        in_specs=[pl.BlockSpec((None, tc, wk), lambda b, c: (b, c, cq)),
                  pl.BlockSpec((None, tc, wk), lambda b, c: (b, c, cq + 1)),
                  pl.BlockSpec((None, tc, wv), lambda b, c: (b, c, cv)),
                  pl.BlockSpec((None, tc, wv), lambda b, c: (b, c, cv + 1)),
                  pl.BlockSpec((tc, dk), lambda b, c: (c, 0)),
                  pl.BlockSpec((tc, dk), lambda b, c: (c, 0)),
                  pl.BlockSpec((1, wk), lambda b, c: (0, 0)),
                  pl.BlockSpec((1, wv), lambda b, c: (0, 0))],
        out_specs=(pl.BlockSpec((None, tc, wv), lambda b, c: (b, c, 0)),
                   pl.BlockSpec((None, HB, dk, dv), lambda b, c: (b, 0, 0, 0))),
        scratch_shapes=[pltpu.VMEM((HB, dv, dk), F32)],
        compiler_params=_cparams(("arbitrary", "arbitrary")),
        name="ret_prompt",
    )(y3, y3, y3, y3, cosf, sinf, lg_row, norm_g.reshape(1, -1))


def _row_to_col(x_row):
    n = x_row.shape[-1]
    r = lax.broadcasted_iota(jnp.int32, (n, n), 0)
    c = lax.broadcasted_iota(jnp.int32, (n, n), 1)
    return jnp.sum(jnp.where(r == c, jnp.broadcast_to(x_row, (n, n)), 0.0), axis=1, keepdims=True)


def _even_decode_kernel(y_ref, sg_ref, sr_ref, wa2_ref, ba_ref, gng_ref, rng_ref, cos_ref, sin_ref,
                        lg_ref, o_ref, sgo_ref, sro_ref, *, cols):
    dk, dv = LANES, 2 * LANES
    cqa, cka, cva, cga, cqb, ckb, cvb, cgb, cra = cols
    y = y_ref[...]
    pre = _dot(y[:, cra:cra + LANES].astype(BF16), wa2_ref[...]) + ba_ref[...]
    la = (jnp.minimum(pre, 0.0) - _softplus_neg_abs(pre)) * np.float32(1.0 / GLA_TAU)
    alpha = jnp.exp(la)
    cosf = cos_ref[...]
    sinf = sin_ref[...]
    gam = jnp.exp(lg_ref[...])
    outs = []
    for h in range(HA):
        q = y[:, cqa + h * dk:cqa + (h + 1) * dk] * np.float32(dk ** -0.5)
        k = y[:, cka + h * dk:cka + (h + 1) * dk]
        v = y[:, cva + h * dv:cva + (h + 1) * dv]
        s_new = _row_to_col(alpha[:, h * dk:(h + 1) * dk]) * sg_ref[h] + _row_to_col(k) * v
        sgo_ref[h] = s_new
        o = jnp.sum(_row_to_col(q) * s_new, axis=0, keepdims=True)
        g = y[:, cga + h * dv:cga + (h + 1) * dv]
        outs.append(_rms(o, gng_ref[:, h * dv:(h + 1) * dv]) * _silu(g))
    for h in range(HB):
        q = _rot(y[:, cqb + h * dk:cqb + (h + 1) * dk], cosf, sinf) * np.float32(dk ** -0.5)
        k = _rot(y[:, ckb + h * dk:ckb + (h + 1) * dk], cosf, sinf)
        v = y[:, cvb + h * dv:cvb + (h + 1) * dv]
        s_new = _row_to_col(gam[:, h * dk:(h + 1) * dk]) * sr_ref[h] + _row_to_col(k) * v
        sro_ref[h] = s_new
        o = jnp.sum(_row_to_col(q) * s_new, axis=0, keepdims=True)
        g = y[:, cgb + h * dv:cgb + (h + 1) * dv]
        outs.append(_rms(o, rng_ref[:, h * dv:(h + 1) * dv]) * _silu(g))
    o_ref[...] = jnp.concatenate(outs, axis=-1).astype(o_ref.dtype)


def _even_decode(y_s, state_gla, state_ret, wa2_pad, b_a, gla_ng, ret_ng, cos_row, sin_row, lg_row, cols):
    bsz, n = y_s.shape
    dk, dv = LANES, 2 * LANES
    wide = (HA + HB) * dv
    kern = functools.partial(_even_decode_kernel, cols=cols)
    full = lambda a: pl.BlockSpec(a.shape, lambda b: (0,) * a.ndim)
    b_a2, gng, rng_ = b_a.reshape(1, -1), gla_ng.reshape(1, -1), ret_ng.reshape(1, -1)
    mixed, sg, sr = pl.pallas_call(
        kern,
        out_shape=(jax.ShapeDtypeStruct((bsz, 1, wide), BF16),
                   jax.ShapeDtypeStruct(state_gla.shape, F32),
                   jax.ShapeDtypeStruct(state_ret.shape, F32)),
        grid=(bsz,),
        in_specs=[pl.BlockSpec((None, 1, n), lambda b: (b, 0, 0)),
                  pl.BlockSpec((None, HA, dk, dv), lambda b: (b, 0, 0, 0)),
                  pl.BlockSpec((None, HB, dk, dv), lambda b: (b, 0, 0, 0)),
                  full(wa2_pad), full(b_a2), full(gng), full(rng_), full(cos_row), full(sin_row), full(lg_row)],
        out_specs=(pl.BlockSpec((None, 1, wide), lambda b: (b, 0, 0)),
                   pl.BlockSpec((None, HA, dk, dv), lambda b: (b, 0, 0, 0)),
                   pl.BlockSpec((None, HB, dk, dv), lambda b: (b, 0, 0, 0))),
        compiler_params=_cparams(("arbitrary",)),
        name="even_decode",
    )(y_s.reshape(bsz, 1, n), state_gla, state_ret, wa2_pad, b_a2, gng, rng_, cos_row, sin_row, lg_row)
    return mixed.reshape(bsz, wide), sg, sr


def _qkv_kernel(q_ref, k_ref, v_ref, qg_ref, kg_ref, qn_ref, kn_ref, knb_ref, vo_ref):
    dc = qg_ref.shape[-1]
    qscale = np.float32(dc ** -0.5) * LOG2E
    for h in range(q_ref.shape[-1] // dc):
        hs = slice(h * dc, (h + 1) * dc)
        qn_ref[:, hs] = (_rms(q_ref[:, hs].astype(F32), qg_ref[...]) * qscale).astype(BF16)
        kn = _rms(k_ref[:, hs].astype(F32), kg_ref[...])
        kn_ref[:, hs] = kn
        knb_ref[:, hs] = kn.astype(BF16)
    vo_ref[...] = v_ref[...].astype(F32)


def _qkv_prompt(y2, q_g, k_g, tm):
    m = y2.shape[0]
    dc = LANES
    wide = HC * dc
    blk = lambda c: pl.BlockSpec((tm, wide), lambda i: (i, c))
    gspec = pl.BlockSpec((1, dc), lambda i: (0, 0))
    return pl.pallas_call(
        _qkv_kernel,
        out_shape=(jax.ShapeDtypeStruct((m, wide), BF16),
                   jax.ShapeDtypeStruct((m, wide), F32),
                   jax.ShapeDtypeStruct((m, wide), BF16),
                   jax.ShapeDtypeStruct((m, wide), F32)),
        grid=(m // tm,),
        in_specs=[blk(0), blk(1), blk(2), gspec, gspec],
        out_specs=(blk(0), blk(0), blk(0), blk(0)),
        compiler_params=_cparams(("arbitrary",)),
        name="qkv_prompt",
    )(y2, y2, y2, q_g.reshape(1, dc), k_g.reshape(1, dc))


def _sb_step(qi, kb, q_ref, k_ref, v_ref, g_ref, qone_ref, koff_ref, o_ref, acc_ref, run_ref, *, sub, hp):
    tq = q_ref.shape[0]
    dc = q_ref.shape[1] // hp

    @pl.when(kb == qi)
    def _():
        acc_ref[...] = jnp.zeros_like(acc_ref)
        run_ref[...] = jnp.zeros_like(run_ref)

    def block(masked):
        r = lax.broadcasted_iota(jnp.int32, (sub, sub), 0)
        c = lax.broadcasted_iota(jnp.int32, (sub, sub), 1)
        after = jnp.where(r > c, 1.0, 0.0).astype(BF16)
        units = [(hh, s_i, s_i * sub if masked else 0) for hh in range(hp) for s_i in reversed(range(tq // sub))]
        zs, masks, log_betas, rests, firsts, laters, ws = [], [], [], [], [], [], []
        for hh, s_i, r0 in units:
            hs = slice(hh * dc, (hh + 1) * dc)
            qa = jnp.concatenate([q_ref[r0:, hs], qone_ref[r0:, :]], axis=1)
            ka = jnp.concatenate([k_ref[s_i * sub:(s_i + 1) * sub, hs], koff_ref[hh]], axis=1)
            zs.append(_dot_nt(qa, ka))
        for z in zs:
            rest = _neg_log2_rest(z)
            log_betas.append(z - rest)
            if masked:
                mask = (lax.broadcasted_iota(jnp.int32, z.shape, 1)
                        < lax.broadcasted_iota(jnp.int32, z.shape, 0))
                rest = jnp.where(mask, rest, 0.0)
                masks.append(mask)
            firsts.append(rest[:, 0:1])
            rests.append(rest.astype(BF16))
        for rb in rests:
            laters.append(_dot(rb, after))
        for u in range(len(units)):
            w = jnp.exp2(log_betas[u] - laters[u])
            if masked:
                w = jnp.where(masks[u], w, 0.0)
            ws.append(w.astype(BF16))
        for u, (hh, s_i, r0) in enumerate(units):
            hs = slice(hh * dc, (hh + 1) * dc)
            run = run_ref[hh, r0:, :]
            acc_ref[r0:, hs] += jnp.exp2(-run) * _dot(ws[u], v_ref[s_i * sub:(s_i + 1) * sub, hs])
            run_ref[hh, r0:, :] = run + (laters[u][:, 0:1] + firsts[u])

    pl.when(kb == qi)(functools.partial(block, True))
    pl.when(kb < qi)(functools.partial(block, False))

    @pl.when(kb == 0)
    def _():
        o_ref[...] = (acc_ref[...] * _silu(g_ref[...].astype(F32))).astype(o_ref.dtype)


def _cmlp_kernel(u_ref, v_ref, g_ref, vg_ref, ws_ref, bs_ref, o_ref, *, n_chunk):
    cs = D_CHUNK
    u = _gelu(u_ref[...].astype(F32))
    vn = _rms(_gelu(v_ref[...].astype(F32)), vg_ref[...]).astype(BF16)
    gate = _silu(g_ref[...].astype(F32))
    bs = bs_ref[...]
    r = lax.broadcasted_iota(jnp.int32, (cs, cs), 0)
    c = lax.broadcasted_iota(jnp.int32, (cs, cs), 1)
    for g in range(DG):
        w = jnp.where(r >= c, ws_ref[g], 0.0).astype(BF16)
        lo, hi = g * LANES, (g + 1) * LANES
        for ch in range(n_chunk):
            r0, r1 = ch * cs, (ch + 1) * cs
            mixed = _dot(w, vn[r0:r1, lo:hi]) + bs[:, lo:hi]
            o_ref[r0:r1, lo:hi] = (u[r0:r1, lo:hi] * mixed * gate[r0:r1, lo:hi]).astype(o_ref.dtype)


def _cmlp_prompt(y3, v_norm_g, w_s, bs_full, cols, tm):
    bsz, t, _ = y3.shape
    br = DG * LANES
    cu, cv, cg = cols
    kern = functools.partial(_cmlp_kernel, n_chunk=tm // D_CHUNK)
    return pl.pallas_call(
        kern,
        out_shape=jax.ShapeDtypeStruct((bsz, t, br), BF16),
        grid=(bsz, t // tm),
        in_specs=[pl.BlockSpec((None, tm, br), lambda b, i: (b, i, cu)),
                  pl.BlockSpec((None, tm, br), lambda b, i: (b, i, cv)),
                  pl.BlockSpec((None, tm, br), lambda b, i: (b, i, cg)),
                  pl.BlockSpec((1, br), lambda b, i: (0, 0)),
                  pl.BlockSpec((DG, D_CHUNK, D_CHUNK), lambda b, i: (0, 0, 0)),
                  pl.BlockSpec((D_CHUNK, br), lambda b, i: (0, 0))],
        out_specs=pl.BlockSpec((None, tm, br), lambda b, i: (b, i, 0)),
        compiler_params=_cparams(("arbitrary", "arbitrary")),
        name="cmlp_prompt",
    )(y3, y3, y3, v_norm_g.reshape(1, br), w_s, bs_full)


def _odd_decode_kernel(y_ref, qg_ref, kg_ref, vg_ref, w0_ref, b0_ref, qn_ref, kn_ref, vo_ref, cv_ref, od_ref):
    br = HC * LANES
    y = y_ref[...]
    for h in range(HC):
        lo, hi = h * LANES, (h + 1) * LANES
        qn_ref[:, lo:hi] = _rms(y[:, lo:hi], qg_ref[...])
        kn_ref[:, lo:hi] = _rms(y[:, br + lo:br + hi], kg_ref[...])
    vo_ref[...] = y[:, 2 * br:3 * br]
    u = _gelu(y[:, 4 * br:5 * br])
    vn = _rms(_gelu(y[:, 5 * br:6 * br]), vg_ref[...])
    cv_ref[...] = vn
    od = u * (w0_ref[...] * vn + b0_ref[...]) * _silu(y[:, 6 * br:7 * br])
    od_ref[...] = od.astype(od_ref.dtype)


def _odd_decode(y_s, q_g, k_g, v_norm_g, w0_row, b0_row):
    s = y_s.shape[0]
    br = HC * LANES
    args = (y_s, q_g.reshape(1, -1), k_g.reshape(1, -1), v_norm_g.reshape(1, -1), w0_row, b0_row)
    full = lambda a: pl.BlockSpec(a.shape, lambda i: (0,) * a.ndim)
    osd = lambda dt: jax.ShapeDtypeStruct((s, br), dt)
    ospec = pl.BlockSpec((s, br), lambda i: (0, 0))
    return pl.pallas_call(
        _odd_decode_kernel,
        out_shape=(osd(F32), osd(F32), osd(F32), osd(F32), osd(BF16)),
        grid=(1,),
        in_specs=[full(a) for a in args],
        out_specs=(ospec,) * 5,
        compiler_params=_cparams(("arbitrary",)),
        name="odd_decode",
    )(*args)


def _strided_suffix_sums(x, stride, axis):
    n = x.shape[axis]
    idx = lax.broadcasted_iota(jnp.int32, x.shape, axis)
    inc, tot = x, x
    k = stride
    while k < n:
        inc = inc + jnp.where(idx < n - k, pltpu.roll(inc, n - k, axis), 0.0)
        tot = tot + pltpu.roll(tot, n - k, axis)
        k *= 2
    return inc, tot


def _paged_step(j, last_j, q_ref, g_ref, off_ref, k_refs, v_refs, o_ref, acc_ref, run_ref):
    n_pp = len(k_refs)
    page, hc, dc = k_refs[0].shape
    rows = page * hc
    lane = lax.broadcasted_iota(jnp.int32, (hc, rows), 1)
    own = (lane & (hc - 1)) == lax.broadcasted_iota(jnp.int32, (hc, rows), 0)

    @pl.when(j == 0)
    def _():
        acc_ref[...] = jnp.zeros_like(acc_ref)
        run_ref[...] = jnp.zeros_like(run_ref)

    q = (q_ref[...] * (np.float32(dc ** -0.5) * LOG2E)).astype(BF16)
    off = off_ref[...] * LOG2E
    prow = lax.broadcasted_iota(jnp.int32, (n_pp, rows), 0)
    z = jnp.zeros((n_pp, rows), F32)
    for p in range(n_pp):
        kf = k_refs[p][...].reshape(rows, dc).astype(BF16)
        s = _dot_nt(q, kf)
        z = jnp.where(prow == p, jnp.sum(jnp.where(own, s, 0.0), axis=0, keepdims=True), z)
    z = z + off
    incl, tot = _strided_suffix_sums(_neg_log2_rest(z), hc, 1)
    pages_incl, _ = _strided_suffix_sums(tot, 1, 0)
    run = run_ref[...]
    w = jnp.exp2(z - incl - (pages_incl - tot) - run)
    run_ref[...] = run + pages_incl[0:1, :]
    acc = acc_ref[...]
    for p in range(n_pp):
        wm = jnp.where(own, jnp.broadcast_to(w[p:p + 1, :], (hc, rows)), 0.0).astype(BF16)
        acc = acc + _dot(wm, v_refs[p][...].reshape(rows, dc).astype(BF16))
    acc_ref[...] = acc

    @pl.when(j == last_j)
    def _():
        o_ref[...] = (acc_ref[...] * _silu(g_ref[...])).astype(o_ref.dtype)


def _attention_kernel(qi_ref, kb_ref, pt_ref, q_ref, k_ref, v_ref, g_ref, qone_ref, koff_ref,
                      pq_ref, pg_ref, poff_ref, *refs, sub, hp, n_pp, n_pchunks, n_psteps):
    k_refs, v_refs = refs[:n_pp], refs[n_pp:2 * n_pp]
    o_ref, po_ref, acc_ref, run_ref, pacc_ref, prun_ref = refs[2 * n_pp:]
    p = pl.program_id(2)
    _sb_step(qi_ref[p], kb_ref[p], q_ref, k_ref, v_ref, g_ref, qone_ref, koff_ref, o_ref, acc_ref, run_ref,
             sub=sub, hp=hp)
    step = (pl.program_id(0) * pl.num_programs(1) + pl.program_id(1)) * pl.num_programs(2) + p

    @pl.when(step < n_psteps)
    def _():
        _paged_step(step % n_pchunks, n_pchunks - 1, pq_ref, pg_ref, poff_ref, k_refs, v_refs,
                    po_ref, pacc_ref, prun_ref)


def _attention(qn, kn, y3, sb_offset, cv, cg, tq, sub, hp, qn_s, gate_s, cache_k, cache_v, layer, page_table):
    bsz, t, _ = qn.shape
    dc = LANES
    wide = hp * dc
    nq = t // tq
    n_hg = HC // hp
    pairs = [(qi, kb) for qi in range(nq) for kb in range(qi, -1, -1)]
    n_pairs = len(pairs)
    qi_tab = jnp.asarray([p[0] for p in pairs], jnp.int32)
    kb_tab = jnp.asarray([p[1] for p in pairs], jnp.int32)
    off2 = sb_offset.astype(F32) * LOG2E
    terms, rem = [], off2
    for _ in range(3):
        terms.append(rem.astype(BF16))
        rem = rem - terms[-1].astype(F32)
    lane = jnp.arange(dc)
    koff = sum(jnp.where(lane == n, tm[:, None, None], 0).astype(BF16) for n, tm in enumerate(terms))
    koff = jnp.broadcast_to(koff, (HC, sub, dc))
    qone = jnp.broadcast_to(jnp.where(lane < len(terms), 1, 0).astype(BF16), (tq, dc))

    s, hc, _ = qn_s.shape
    n_pages = page_table.shape[1]
    page = cache_k.shape[2]
    n_pp = PAGES_PER_STEP
    n_pchunks = n_pages // n_pp
    n_psteps = s * n_pchunks
    assert hc & (hc - 1) == 0 and n_pages % n_pp == 0 and n_psteps <= bsz * n_hg * n_pairs
    rows = page * hc

    def pstep(b, h, p):
        return jnp.minimum((b * n_hg + h) * n_pairs + p, n_psteps - 1)

    def kv_spec(pp):
        def index(b, h, p, qi, kb, pt):
            st = pstep(b, h, p)
            return (layer, pt[st // n_pchunks, n_pages - (st % n_pchunks + 1) * n_pp + pp], 0, 0, 0)
        return pl.BlockSpec((None, None, page, hc, dc), index)

    seq_blk = pl.BlockSpec((None, hc, dc), lambda b, h, p, qi, kb, pt: (pstep(b, h, p) // n_pchunks, 0, 0))
    q_map = lambda b, h, p, qi, kb, pt: (b, qi[p], h)
    k_map = lambda b, h, p, qi, kb, pt: (b, kb[p], h)
    kern = functools.partial(_attention_kernel, sub=sub, hp=hp, n_pp=n_pp, n_pchunks=n_pchunks,
                             n_psteps=n_psteps)
    return pl.pallas_call(
        kern,
        out_shape=(jax.ShapeDtypeStruct((bsz, t, HC * dc), BF16),
                   jax.ShapeDtypeStruct((s, hc, dc), BF16)),
        grid_spec=pltpu.PrefetchScalarGridSpec(
            num_scalar_prefetch=3,
            grid=(bsz, n_hg, n_pairs),
            in_specs=[pl.BlockSpec((None, tq, wide), q_map),
                      pl.BlockSpec((None, tq, wide), k_map),
                      pl.BlockSpec((None, tq, wide), lambda b, h, p, qi, kb, pt: (b, kb[p], cv // wide + h)),
                      pl.BlockSpec((None, tq, wide), lambda b, h, p, qi, kb, pt: (b, qi[p], cg // wide + h)),
                      pl.BlockSpec((tq, dc), lambda b, h, p, qi, kb, pt: (0, 0)),
                      pl.BlockSpec((hp, sub, dc), lambda b, h, p, qi, kb, pt: (h, 0, 0)),
                      seq_blk, seq_blk,
                      pl.BlockSpec((1, rows), lambda b, h, p, qi, kb, pt: (0, 0))]
            + [kv_spec(pp) for pp in range(n_pp)] * 2,
            out_specs=(pl.BlockSpec((None, tq, wide), q_map), seq_blk),
            scratch_shapes=[pltpu.VMEM((tq, wide), F32), pltpu.VMEM((hp, tq, 1), F32),
                            pltpu.VMEM((hc, dc), F32), pltpu.VMEM((1, rows), F32)]),
        compiler_params=_cparams(("arbitrary",) * 3),
        name="attention",
    )(qi_tab, kb_tab, page_table, qn, kn, y3, y3, qone, koff,
      qn_s, gate_s, jnp.tile(sb_offset, page).reshape(1, rows),
      *([cache_k] * n_pp), *([cache_v] * n_pp))


def _rope_tables(pos, half):
    inv = ROPE_BASE ** (-jnp.arange(half, dtype=F32) / half)
    ang = pos.astype(F32)[:, None] * inv[None, :]
    cos, sin = jnp.cos(ang), jnp.sin(ang)
    return jnp.concatenate([cos, cos], axis=-1), jnp.concatenate([-sin, sin], axis=-1)


def kernel(x_prompt, x_sample, state_gla, state_ret, cache_k, cache_v, page_table, c_prompt, c_sample,
           e_norm_g, e_w_ada, e_b_ada, e_w_in, e_w_a2, e_b_a, e_gla_norm_g, e_ret_norm_g, e_w_out,
           o_norm_g, o_w_ada, o_b_ada, o_w_in, o_q_norm_g, o_k_norm_g, o_sb_offset, o_v_norm_g,
           o_w_s, o_b_s, o_w_out):
    bsz, t, d = x_prompt.shape
    s = x_sample.shape[0]
    br = d // 2
    n_past = page_table.shape[1] * cache_k.shape[2]
    dk = LANES

    c_rows = jnp.concatenate([c_sample, c_prompt, jnp.zeros((16 - s - bsz, d), F32)], axis=0)

    n_qkv = 2 * HA * dk + HA * 2 * dk
    n_main = e_w_in.shape[2] - GLA_RANK
    col_ret = n_qkv + HA * 2 * dk
    cols_dec = (0, HA * dk, 2 * HA * dk, n_qkv,
                col_ret, col_ret + HB * dk, col_ret + 2 * HB * dk, col_ret + 2 * HB * dk + HB * 2 * dk,
                n_main)

    xp = x_prompt.reshape(bsz * t, d)
    xs = x_sample.reshape(s, d)
    pos_p = jnp.arange(t)
    pos_s = n_past + jnp.arange(x_sample.shape[1])
    cos_p, sin_p = _rope_tables(pos_p, dk // 2)
    cos_s, sin_s = _rope_tables(pos_s, dk // 2)
    log_gamma = jnp.log1p(-jnp.exp2(-5.0 - jnp.arange(HB, dtype=F32)))
    lg_row = jnp.repeat(log_gamma, dk).reshape(1, HB * dk)

    i = 0
    mod = _ada(c_rows, e_w_ada[i], e_b_ada[i])
    mod_s, mod_p = mod[:s], mod[s:s + bsz].reshape(bsz, 1, 3 * d)
    w_in_b = _reorder_cast(jnp.swapaxes(e_w_in, 1, 2), i, n_qkv, GLA_RANK, n_main + MXU_DIM, 256)
    wa2_pad = jnp.concatenate([e_w_a2[i], jnp.zeros((LANES - GLA_RANK, HA * dk), F32)], axis=0).astype(BF16)
    w_out_b = e_w_out[i].astype(BF16)

    y = _proj(xp, mod_p, e_norm_g[i], w_in_b, t, 1024, 1280, w_transposed=True)
    y3 = y.reshape(bsz, t, -1)
    oa, gla_p = _gla_prompt(y3, wa2_pad, e_b_a[i], e_gla_norm_g[i], 512)
    ob, ret_p = _ret_prompt(y3, cos_p, sin_p, lg_row, e_ret_norm_g[i], col_ret, 512)
    xp = _outproj(oa.reshape(bsz * t, br), 0, ob.reshape(bsz * t, br), 0, w_out_b, xp, mod_p, t, 512, d)

    y_s = _proj(xs, mod_s, e_norm_g[i], w_in_b, None, s, 1280, w_transposed=True)
    mixed_s, gla_s, ret_s = _even_decode(y_s, state_gla[i], state_ret[i], wa2_pad, e_b_a[i],
                                         e_gla_norm_g[i], e_ret_norm_g[i], cos_s, sin_s, lg_row, cols_dec)
    xs = _outproj(mixed_s, 0, mixed_s, 1, w_out_b, xs, mod_s, None, s, 1024)

    mod = _ada(c_rows, o_w_ada[i], o_b_ada[i])
    mod_s, mod_p = mod[:s], mod[s:s + bsz].reshape(bsz, 1, 3 * d)
    w_in_b = o_w_in[i].astype(BF16)
    w_out_b = o_w_out[i].astype(BF16)
    bs_full = jnp.repeat(jnp.transpose(o_b_s[i]), LANES, axis=1)

    y = _proj(xp, mod_p, o_norm_g[i], w_in_b, t, 1024, 1024)
    y3 = y.reshape(bsz, t, -1)
    qn, kn, knb, vo = _qkv_prompt(y, o_q_norm_g[i], o_k_norm_g[i], 512)
    od = _cmlp_prompt(y3, o_v_norm_g[i], o_w_s[i], bs_full, (4, 5, 6), 256)

    y_s = _proj(xs, mod_s, o_norm_g[i], w_in_b, None, s, 1024)
    w0_row = jnp.repeat(o_w_s[i][:, 0, 0], LANES).reshape(1, br)
    b0_row = jnp.repeat(o_b_s[i][:, 0], LANES).reshape(1, br)
    qn_s, kn_s, vo_s, cv_s, od_s = _odd_decode(y_s, o_q_norm_g[i], o_k_norm_g[i], o_v_norm_g[i], w0_row, b0_row)
    dc = LANES
    shp = (bsz, t, br)
    oc, oc_s = _attention(qn.reshape(shp), knb.reshape(shp), y3, o_sb_offset[i], 2 * br, 3 * br, 512, 256, 4,
                          qn_s.reshape(s, HC, dc), y_s[:, 3 * br:4 * br].reshape(s, HC, dc),
                          cache_k, cache_v, i, page_table)
    xp = _outproj(oc.reshape(bsz * t, br), 0, od.reshape(bsz * t, br), 0, w_out_b, xp, mod_p, t, 512, d)
    xs = _outproj(oc_s.reshape(s, br), 0, od_s, 0, w_out_b, xs, mod_s, None, s, 1024)

    return (xp.reshape(bsz, t, d), xs.reshape(s, 1, d),
            gla_p[None], gla_s[None], ret_p[None], ret_s[None],
            kn.reshape(1, bsz, t, HC, dc), vo.reshape(1, bsz, t, HC, dc),
            kn_s.reshape(1, s, 1, HC, dc), vo_s.reshape(1, s, 1, HC, dc),
            cv_s.reshape(1, s, 1, br))
```

```python
import functools

import jax
import jax.numpy as jnp
import numpy as np
from jax import lax
from jax.experimental import pallas as pl
from jax.experimental.pallas import tpu as pltpu

F32 = jnp.float32
BF16 = jnp.bfloat16

HA = 4
HB = 4
HC = 8
DG = 8
GLA_RANK = 16
GLA_TAU = 16.0
GLA_CHUNK = 64
GLA_SAFE_LOG_DECAY = 40.0
RET_CHUNK = 128
ROPE_BASE = 10000.0
D_CHUNK = 128
RMS_EPS = 1e-6

LANES = 128
MXU_DIM = 256
VMEM_LIMIT = 56 * 1024 * 1024
LOG2E = np.float32(np.log2(np.e))
PAGES_PER_STEP = 8


def _cparams(sem):
    return pltpu.CompilerParams(dimension_semantics=sem, vmem_limit_bytes=VMEM_LIMIT)


def _dot(a, b):
    return jnp.dot(a, b, preferred_element_type=F32)


def _dot_nt(a, b):
    return lax.dot_general(a, b, (((1,), (1,)), ((), ())), preferred_element_type=F32)


def _dot_tn(a, b):
    return lax.dot_general(a, b, (((0,), (0,)), ((), ())), preferred_element_type=F32)


def _sigmoid(x):
    return 1.0 / (1.0 + jnp.exp(-x))


def _silu(x):
    return x * _sigmoid(x)


def _gelu(x):
    c = np.float32(np.sqrt(2.0 / np.pi))
    return 0.5 * x * (1.0 + jnp.tanh(c * (x + 0.044715 * (x * x * x))))


def _softplus_neg_abs(x):
    return jnp.log(1.0 + jnp.exp(-jnp.abs(x)))


def _neg_log2_rest(z2):
    return jnp.maximum(z2, 0.0) + jnp.log(1.0 + jnp.exp2(-jnp.abs(z2))) * LOG2E


def _split_dot(x, ones_bf16, left=False):
    hi = x.astype(BF16)
    lo = (x - hi.astype(F32)).astype(BF16)
    if left:
        return _dot(ones_bf16, hi) + _dot(ones_bf16, lo)
    return _dot(hi, ones_bf16) + _dot(lo, ones_bf16)


def _rms(x, g):
    ms = jnp.mean(x * x, axis=-1, keepdims=True)
    return x * lax.rsqrt(ms + RMS_EPS) * g


def _ada_kernel(c_ref, w_ref, b_ref, o_ref):
    s = _silu(c_ref[...]).astype(BF16)
    o_ref[...] = _dot(s, w_ref[...].astype(BF16)) + b_ref[...]


def _ada(c_rows, w, b):
    r, d = c_rows.shape
    n = w.shape[1]
    tn = 512
    return pl.pallas_call(
        _ada_kernel,
        out_shape=jax.ShapeDtypeStruct((r, n), F32),
        grid=(n // tn,),
        in_specs=[pl.BlockSpec((r, d), lambda j: (0, 0)),
                  pl.BlockSpec((d, tn), lambda j: (0, j)),
                  pl.BlockSpec((1, tn), lambda j: (0, j))],
        out_specs=pl.BlockSpec((r, tn), lambda j: (0, j)),
        compiler_params=_cparams(("arbitrary",)),
        name="ada",
    )(c_rows, w, b.reshape(1, n))


def _reorder_cast_kernel(w_ref, o_ref, *, n_head, n_rank):
    n = w_ref.shape[0]
    tail = n - n_head - n_rank
    o_ref[:n_head, :] = w_ref[:n_head, :].astype(BF16)
    o_ref[n_head:n_head + tail, :] = w_ref[n_head + n_rank:, :].astype(BF16)
    o_ref[n_head + tail:n, :] = w_ref[n_head:n_head + n_rank, :].astype(BF16)
    o_ref[n:, :] = jnp.zeros((o_ref.shape[0] - n, o_ref.shape[1]), BF16)


def _reorder_cast(wt3, layer, n_head, n_rank, n_out, tc):
    _, n, d = wt3.shape
    return pl.pallas_call(
        functools.partial(_reorder_cast_kernel, n_head=n_head, n_rank=n_rank),
        out_shape=jax.ShapeDtypeStruct((n_out, d), BF16),
        grid=(d // tc,),
        in_specs=[pl.BlockSpec((None, n, tc), lambda c: (layer, 0, c))],
        out_specs=pl.BlockSpec((n_out, tc), lambda c: (0, c)),
        compiler_params=_cparams(("arbitrary",)),
        name="reorder_cast",
    )(wt3)


def _proj_kernel(x_ref, shift_ref, scale_ref, g_ref, w_ref, o_ref, h_ref, *, w_transposed, n_chunks):
    mm = _dot_nt if w_transposed else _dot
    j = pl.program_id(1)

    @pl.when(j == 0)
    def _():
        rc = x_ref.shape[0] // n_chunks
        per_row = shift_ref.shape[0] > 1
        for c in range(n_chunks):
            rs = slice(c * rc, (c + 1) * rc)
            scale = scale_ref[rs, :] if per_row else scale_ref[...]
            shift = shift_ref[rs, :] if per_row else shift_ref[...]
            h = (_rms(x_ref[rs, :], g_ref[...]) * (1.0 + scale) + shift).astype(BF16)
            h_ref[rs, :] = h
            o_ref[rs, :] = mm(h, w_ref[...]).astype(o_ref.dtype)

    @pl.when(j > 0)
    def _():
        o_ref[...] = mm(h_ref[...], w_ref[...]).astype(o_ref.dtype)


def _proj(x2d, mod, norm_g, w_bf16, rows_per_batch, tm, tn, w_transposed=False):
    m, d = x2d.shape
    out_dtype = F32 if rows_per_batch is None else BF16
    n = w_bf16.shape[0 if w_transposed else 1]
    w_spec = (pl.BlockSpec((tn, d), lambda i, j: (j, 0)) if w_transposed
              else pl.BlockSpec((d, tn), lambda i, j: (0, j)))
    if rows_per_batch is None:
        shift_spec = pl.BlockSpec((tm, d), lambda i, j: (i, 0))
        scale_spec = pl.BlockSpec((tm, d), lambda i, j: (i, 1))
    else:
        tpb = rows_per_batch // tm
        shift_spec = pl.BlockSpec((None, 1, d), lambda i, j: (i // tpb, 0, 0))
        scale_spec = pl.BlockSpec((None, 1, d), lambda i, j: (i // tpb, 0, 1))
    return pl.pallas_call(
        functools.partial(_proj_kernel, w_transposed=w_transposed, n_chunks=max(1, tm // MXU_DIM)),
        out_shape=jax.ShapeDtypeStruct((m, n), out_dtype),
        grid=(m // tm, n // tn),
        in_specs=[pl.BlockSpec((tm, d), lambda i, j: (i, 0)),
                  shift_spec, scale_spec,
                  pl.BlockSpec((1, d), lambda i, j: (0, 0)),
                  w_spec],
        out_specs=pl.BlockSpec((tm, tn), lambda i, j: (i, j)),
        scratch_shapes=[pltpu.VMEM((tm, d), BF16)],
        compiler_params=_cparams(("arbitrary", "arbitrary")),
        name="proj",
    )(x2d, mod, mod, norm_g.reshape(1, d), w_bf16)


def _outproj_kernel(ma_ref, mb_ref, wa_ref, wb_ref, x_ref, gate_ref, o_ref):
    mixed = _dot(ma_ref[...], wa_ref[...]) + _dot(mb_ref[...], wb_ref[...])
    o_ref[...] = x_ref[...] + gate_ref[...] * mixed


def _outproj(ma, ca, mb, cb, w_bf16, x2d, mod, rows_per_batch, tm, tn):
    m, d = x2d.shape
    half = w_bf16.shape[0] // 2
    gcol = 2 * (d // tn)
    if rows_per_batch is None:
        gate_spec = pl.BlockSpec((tm, tn), lambda i, j: (i, gcol + j))
    else:
        tpb = rows_per_batch // tm
        gate_spec = pl.BlockSpec((None, 1, tn), lambda i, j: (i // tpb, 0, gcol + j))
    return pl.pallas_call(
        _outproj_kernel,
        out_shape=jax.ShapeDtypeStruct((m, d), F32),
        grid=(m // tm, d // tn),
        in_specs=[pl.BlockSpec((tm, half), lambda i, j: (i, ca)),
                  pl.BlockSpec((tm, half), lambda i, j: (i, cb)),
                  pl.BlockSpec((half, tn), lambda i, j: (0, j)),
                  pl.BlockSpec((half, tn), lambda i, j: (1, j)),
                  pl.BlockSpec((tm, tn), lambda i, j: (i, j)),
                  gate_spec],
        out_specs=pl.BlockSpec((tm, tn), lambda i, j: (i, j)),
        compiler_params=_cparams(("arbitrary", "arbitrary")),
        name="outproj",
    )(ma, mb, w_bf16, w_bf16, x2d, mod)


def _gla_kernel(q_ref, k_ref, v_ref, ra_ref, g_ref, wa2_ref, ba_ref, ng_ref,
                o_ref, st_ref, stt_ref, la_ref, b_ref, *, n_sub, dk, dv):
    c = pl.program_id(1)
    cs = GLA_CHUNK

    @pl.when(c == 0)
    def _():
        stt_ref[...] = jnp.zeros_like(stt_ref)

    row = lax.broadcasted_iota(jnp.int32, (cs, cs), 0)
    col = lax.broadcasted_iota(jnp.int32, (cs, cs), 1)
    causal = row >= col
    tri = jnp.where(causal, 1.0, 0.0).astype(BF16)
    qscale = np.float32(dk ** -0.5)

    pre = _dot(ra_ref[...].astype(BF16), wa2_ref[...]) + ba_ref[...]
    la = (jnp.minimum(pre, 0.0) - _softplus_neg_abs(pre)) * np.float32(1.0 / GLA_TAU)
    la_ref[...] = la
    safe = jnp.min(la) >= np.float32(-GLA_SAFE_LOG_DECAY / cs)

    def pairwise_scores(qs, kb, hs):
        b = b_ref[:, hs]
        sublanes = 8

        def key_group(gi, acc):
            s0 = pl.multiple_of(gi * sublanes, sublanes)
            bk = b_ref[pl.ds(s0, sublanes), hs]
            for r in range(sublanes):
                e = jnp.exp(jnp.minimum(b - bk[r:r + 1, :], 0.0))
                acc = jnp.where(col == s0 + r, _dot_nt((qs * e).astype(BF16), kb), acc)
            return acc

        return lax.fori_loop(0, cs // sublanes, key_group, jnp.zeros((cs, cs), F32))

    def make_body(factorised):
        def body(i, carry):
            r0 = pl.multiple_of(i * cs, cs)
            rows = pl.ds(r0, cs)
            b = _split_dot(la_ref[rows, :], tri, left=True)
            bl = b[cs - 1:cs, :]
            k = k_ref[rows, :].astype(F32)
            qs = q_ref[rows, :].astype(F32) * qscale
            qe = (qs * jnp.exp(b)).astype(BF16)
            kd = (k * jnp.exp(bl - b)).astype(BF16)
            ebl = jnp.exp(bl)
            if factorised:
                ke = (k * jnp.exp(-b)).astype(BF16)
            else:
                b_ref[...] = b
            for h in range(HA):
                ks, vs = slice(h * dk, (h + 1) * dk), slice(h * dv, (h + 1) * dv)
                v = v_ref[rows, vs].astype(BF16)
                if factorised:
                    s = _dot_nt(qe[:, ks], ke[:, ks])
                else:
                    s = pairwise_scores(qs[:, ks], k[:, ks].astype(BF16), ks)
                s = jnp.where(causal, s, 0.0)
                stt = stt_ref[h]
                o = _dot(s.astype(BF16), v) + _dot_nt(qe[:, ks], stt.astype(BF16))
                stt_ref[h] = stt * ebl[:, ks] + _dot_tn(v, kd[:, ks])
                og = _rms(o, ng_ref[:, vs]) * _silu(g_ref[rows, vs].astype(F32))
                o_ref[rows, vs] = og.astype(o_ref.dtype)
            return carry
        return body

    @pl.when(safe)
    def _():
        lax.fori_loop(0, n_sub, make_body(True), 0, unroll=True)

    @pl.when(jnp.logical_not(safe))
    def _():
        lax.fori_loop(0, n_sub, make_body(False), 0)

    @pl.when(c == pl.num_programs(1) - 1)
    def _():
        for h in range(HA):
            st_ref[h] = stt_ref[h].T


def _gla_prompt(y3, wa2_pad, b_a, norm_g, tc):
    bsz, t, n = y3.shape
    dk, dv = LANES, 2 * LANES
    wk, wv = HA * dk, HA * dv
    kern = functools.partial(_gla_kernel, n_sub=tc // GLA_CHUNK, dk=dk, dv=dv)
    return pl.pallas_call(
        kern,
        out_shape=(jax.ShapeDtypeStruct((bsz, t, wv), BF16),
                   jax.ShapeDtypeStruct((bsz, HA, dk, dv), F32)),
        grid=(bsz, t // tc),
        in_specs=[pl.BlockSpec((None, tc, wk), lambda b, c: (b, c, 0)),
                  pl.BlockSpec((None, tc, wk), lambda b, c: (b, c, 1)),
                  pl.BlockSpec((None, tc, wv), lambda b, c: (b, c, 1)),
                  pl.BlockSpec((None, tc, LANES), lambda b, c: (b, c, (n - MXU_DIM) // LANES)),
                  pl.BlockSpec((None, tc, wv), lambda b, c: (b, c, 2)),
                  pl.BlockSpec((LANES, wk), lambda b, c: (0, 0)),
                  pl.BlockSpec((1, wk), lambda b, c: (0, 0)),
                  pl.BlockSpec((1, wv), lambda b, c: (0, 0))],
        out_specs=(pl.BlockSpec((None, tc, wv), lambda b, c: (b, c, 0)),
                   pl.BlockSpec((None, HA, dk, dv), lambda b, c: (b, 0, 0, 0))),
        scratch_shapes=[pltpu.VMEM((HA, dv, dk), F32), pltpu.VMEM((tc, wk), F32),
                        pltpu.VMEM((GLA_CHUNK, wk), F32)],
        compiler_params=_cparams(("arbitrary", "arbitrary")),
        name="gla_prompt",
    )(y3, y3, y3, y3, y3, wa2_pad, b_a.reshape(1, -1), norm_g.reshape(1, -1))


def _rot(x, cosf, sinf):
    return x * cosf + pltpu.roll(x, x.shape[-1] // 2, 1) * sinf


def _ret_kernel(q_ref, k_ref, v_ref, g_ref, cos_ref, sin_ref, lg_ref, ng_ref,
                o_ref, st_ref, stt_ref, *, n_sub, dk, dv):
    c = pl.program_id(1)
    cs = RET_CHUNK

    @pl.when(c == 0)
    def _():
        stt_ref[...] = jnp.zeros_like(stt_ref)

    row = lax.broadcasted_iota(jnp.int32, (cs, cs), 0)
    col = lax.broadcasted_iota(jnp.int32, (cs, cs), 1)
    rel = (row - col).astype(F32)
    ridx = lax.broadcasted_iota(jnp.int32, (cs, 1), 0).astype(F32)
    lgs = [lg_ref[:, h * dk:h * dk + 1] for h in range(HB)]
    decay = [jnp.where(row >= col, jnp.exp(rel * lg), 0.0) for lg in lgs]
    inter = [jnp.exp((ridx + 1.0) * lg) for lg in lgs]
    kdec = [jnp.exp((np.float32(cs - 1.0) - ridx) * lg) for lg in lgs]
    sdec = [jnp.exp(np.float32(cs) * lg) for lg in lgs]
    qscale = np.float32(dk ** -0.5)

    def body(i, carry):
        r0 = pl.multiple_of(i * cs, cs)
        rows = pl.ds(r0, cs)
        cosf = cos_ref[rows, :]
        sinf = sin_ref[rows, :]
        for h in range(HB):
            ks, vs = slice(h * dk, (h + 1) * dk), slice(h * dv, (h + 1) * dv)
            qr = (_rot(q_ref[rows, ks].astype(F32), cosf, sinf) * qscale).astype(BF16)
            kr = _rot(k_ref[rows, ks].astype(F32), cosf, sinf)
            v = v_ref[rows, vs].astype(BF16)
            s = _dot_nt(qr, kr.astype(BF16)) * decay[h]
            stt = stt_ref[h]
            o = _dot(s.astype(BF16), v) + _dot_nt(qr, stt.astype(BF16)) * inter[h]
            stt_ref[h] = stt * sdec[h] + _dot_tn(v, (kr * kdec[h]).astype(BF16))
            og = _rms(o, ng_ref[:, vs]) * _silu(g_ref[rows, vs].astype(F32))
            o_ref[rows, vs] = og.astype(o_ref.dtype)
        return carry

    lax.fori_loop(0, n_sub, body, 0, unroll=True)

    @pl.when(c == pl.num_programs(1) - 1)
    def _():
        for h in range(HB):
            st_ref[h] = stt_ref[h].T


def _ret_prompt(y3, cosf, sinf, lg_row, norm_g, col0, tc):
    bsz, t, _ = y3.shape
    dk, dv = LANES, 2 * LANES
    wk, wv = HB * dk, HB * dv
    cq, cv = col0 // wk, (col0 + 2 * wk) // wv
    kern = functools.partial(_ret_kernel, n_sub=tc // RET_CHUNK, dk=dk, dv=dv)
    return pl.pallas_call(
        kern,
        out_shape=(jax.ShapeDtypeStruct((bsz, t, wv), BF16),
                   jax.ShapeDtypeStruct((bsz, HB, dk, dv), F32)),
        grid=(bsz, t // tc),
        in_specs=[pl.BlockSpec((None, tc, wk), lambda b, c: (b, c, cq)),
                  pl.BlockSpec((None, tc, wk), lambda b, c: (b, c, cq + 1)),
                  pl.BlockSpec((None, tc, wv), lambda b, c: (b, c, cv)),
                  pl.BlockSpec((None, tc, wv), lambda b, c: (b, c, cv + 1)),
                  pl.BlockSpec((tc, dk), lambda b, c: (c, 0)),
                  pl.BlockSpec((tc, dk), lambda b, c: (c, 0)),
                  pl.BlockSpec((1, wk), lambda b, c: (0, 0)),
                  pl.BlockSpec((1, wv), lambda b, c: (0, 0))],
        out_specs=(pl.BlockSpec((None, tc, wv), lambda b, c: (b, c, 0)),
                   pl.BlockSpec((None, HB, dk, dv), lambda b, c: (b, 0, 0, 0))),
        scratch_shapes=[pltpu.VMEM((HB, dv, dk), F32)],
        compiler_params=_cparams(("arbitrary", "arbitrary")),
        name="ret_prompt",
    )(y3, y3, y3, y3, cosf, sinf, lg_row, norm_g.reshape(1, -1))


def _row_to_col(x_row):
    n = x_row.shape[-1]
    r = lax.broadcasted_iota(jnp.int32, (n, n), 0)
    c = lax.broadcasted_iota(jnp.int32, (n, n), 1)
    return jnp.sum(jnp.where(r == c, jnp.broadcast_to(x_row, (n, n)), 0.0), axis=1, keepdims=True)


def _even_decode_kernel(y_ref, sg_ref, sr_ref, wa2_ref, ba_ref, gng_ref, rng_ref, cos_ref, sin_ref,
                        lg_ref, o_ref, sgo_ref, sro_ref, *, cols):
    dk, dv = LANES, 2 * LANES
    cqa, cka, cva, cga, cqb, ckb, cvb, cgb, cra = cols
    y = y_ref[...]
    pre = _dot(y[:, cra:cra + LANES].astype(BF16), wa2_ref[...]) + ba_ref[...]
    la = (jnp.minimum(pre, 0.0) - _softplus_neg_abs(pre)) * np.float32(1.0 / GLA_TAU)
    alpha = jnp.exp(la)
    cosf = cos_ref[...]
    sinf = sin_ref[...]
    gam = jnp.exp(lg_ref[...])
    outs = []
    for h in range(HA):
        q = y[:, cqa + h * dk:cqa + (h + 1) * dk] * np.float32(dk ** -0.5)
        k = y[:, cka + h * dk:cka + (h + 1) * dk]
        v = y[:, cva + h * dv:cva + (h + 1) * dv]
        s_new = _row_to_col(alpha[:, h * dk:(h + 1) * dk]) * sg_ref[h] + _row_to_col(k) * v
        sgo_ref[h] = s_new
        o = jnp.sum(_row_to_col(q) * s_new, axis=0, keepdims=True)
        g = y[:, cga + h * dv:cga + (h + 1) * dv]
        outs.append(_rms(o, gng_ref[:, h * dv:(h + 1) * dv]) * _silu(g))
    for h in range(HB):
        q = _rot(y[:, cqb + h * dk:cqb + (h + 1) * dk], cosf, sinf) * np.float32(dk ** -0.5)
        k = _rot(y[:, ckb + h * dk:ckb + (h + 1) * dk], cosf, sinf)
        v = y[:, cvb + h * dv:cvb + (h + 1) * dv]
        s_new = _row_to_col(gam[:, h * dk:(h + 1) * dk]) * sr_ref[h] + _row_to_col(k) * v
        sro_ref[h] = s_new
        o = jnp.sum(_row_to_col(q) * s_new, axis=0, keepdims=True)
        g = y[:, cgb + h * dv:cgb + (h + 1) * dv]
        outs.append(_rms(o, rng_ref[:, h * dv:(h + 1) * dv]) * _silu(g))
    o_ref[...] = jnp.concatenate(outs, axis=-1).astype(o_ref.dtype)


def _even_decode(y_s, state_gla, state_ret, wa2_pad, b_a, gla_ng, ret_ng, cos_row, sin_row, lg_row, cols):
    bsz, n = y_s.shape
    dk, dv = LANES, 2 * LANES
    wide = (HA + HB) * dv
    kern = functools.partial(_even_decode_kernel, cols=cols)
    full = lambda a: pl.BlockSpec(a.shape, lambda b: (0,) * a.ndim)
    b_a2, gng, rng_ = b_a.reshape(1, -1), gla_ng.reshape(1, -1), ret_ng.reshape(1, -1)
    mixed, sg, sr = pl.pallas_call(
        kern,
        out_shape=(jax.ShapeDtypeStruct((bsz, 1, wide), BF16),
                   jax.ShapeDtypeStruct(state_gla.shape, F32),
                   jax.ShapeDtypeStruct(state_ret.shape, F32)),
        grid=(bsz,),
        in_specs=[pl.BlockSpec((None, 1, n), lambda b: (b, 0, 0)),
                  pl.BlockSpec((None, HA, dk, dv), lambda b: (b, 0, 0, 0)),
                  pl.BlockSpec((None, HB, dk, dv), lambda b: (b, 0, 0, 0)),
                  full(wa2_pad), full(b_a2), full(gng), full(rng_), full(cos_row), full(sin_row), full(lg_row)],
        out_specs=(pl.BlockSpec((None, 1, wide), lambda b: (b, 0, 0)),
                   pl.BlockSpec((None, HA, dk, dv), lambda b: (b, 0, 0, 0)),
                   pl.BlockSpec((None, HB, dk, dv), lambda b: (b, 0, 0, 0))),
        compiler_params=_cparams(("arbitrary",)),
        name="even_decode",
    )(y_s.reshape(bsz, 1, n), state_gla, state_ret, wa2_pad, b_a2, gng, rng_, cos_row, sin_row, lg_row)
    return mixed.reshape(bsz, wide), sg, sr


def _qkv_kernel(q_ref, k_ref, v_ref, qg_ref, kg_ref, qn_ref, kn_ref, knb_ref, vo_ref):
    dc = qg_ref.shape[-1]
    qscale = np.float32(dc ** -0.5) * LOG2E
    for h in range(q_ref.shape[-1] // dc):
        hs = slice(h * dc, (h + 1) * dc)
        qn_ref[:, hs] = (_rms(q_ref[:, hs].astype(F32), qg_ref[...]) * qscale).astype(BF16)
        kn = _rms(k_ref[:, hs].astype(F32), kg_ref[...])
        kn_ref[:, hs] = kn
        knb_ref[:, hs] = kn.astype(BF16)
    vo_ref[...] = v_ref[...].astype(F32)


def _qkv_prompt(y2, q_g, k_g, tm):
    m = y2.shape[0]
    dc = LANES
    wide = HC * dc
    blk = lambda c: pl.BlockSpec((tm, wide), lambda i: (i, c))
    gspec = pl.BlockSpec((1, dc), lambda i: (0, 0))
    return pl.pallas_call(
        _qkv_kernel,
        out_shape=(jax.ShapeDtypeStruct((m, wide), BF16),
                   jax.ShapeDtypeStruct((m, wide), F32),
                   jax.ShapeDtypeStruct((m, wide), BF16),
                   jax.ShapeDtypeStruct((m, wide), F32)),
        grid=(m // tm,),
        in_specs=[blk(0), blk(1), blk(2), gspec, gspec],
        out_specs=(blk(0), blk(0), blk(0), blk(0)),
        compiler_params=_cparams(("arbitrary",)),
        name="qkv_prompt",
    )(y2, y2, y2, q_g.reshape(1, dc), k_g.reshape(1, dc))


def _sb_step(qi, kb, q_ref, k_ref, v_ref, g_ref, qone_ref, koff_ref, o_ref, acc_ref, run_ref, *, sub, hp,
             side_work):
    tq = q_ref.shape[0]
    dc = q_ref.shape[1] // hp

    @pl.when(kb == qi)
    def _():
        acc_ref[...] = jnp.zeros_like(acc_ref)
        run_ref[...] = jnp.zeros_like(run_ref)

    def block(masked):
        side = side_work()
        r = lax.broadcasted_iota(jnp.int32, (sub, sub), 0)
        c = lax.broadcasted_iota(jnp.int32, (sub, sub), 1)
        after = jnp.where(r > c, 1.0, 0.0).astype(BF16)
        units = [(hh, s_i, s_i * sub if masked else 0) for hh in range(hp) for s_i in reversed(range(tq // sub))]
        zs, masks, log_betas, rests, firsts, laters, ws = [], [], [], [], [], [], []
        for hh, s_i, r0 in units:
            hs = slice(hh * dc, (hh + 1) * dc)
            qa = jnp.concatenate([q_ref[r0:, hs], qone_ref[r0:, :]], axis=1)
            ka = jnp.concatenate([k_ref[s_i * sub:(s_i + 1) * sub, hs], koff_ref[hh]], axis=1)
            zs.append(_dot_nt(qa, ka))
        next(side, None)
        for z in zs:
            rest = _neg_log2_rest(z)
            log_betas.append(z - rest)
            if masked:
                mask = (lax.broadcasted_iota(jnp.int32, z.shape, 1)
                        < lax.broadcasted_iota(jnp.int32, z.shape, 0))
                rest = jnp.where(mask, rest, 0.0)
                masks.append(mask)
            firsts.append(rest[:, 0:1])
            rests.append(rest.astype(BF16))
        next(side, None)
        for rb in rests:
            laters.append(_dot(rb, after))
        next(side, None)
        for u in range(len(units)):
            w = jnp.exp2(log_betas[u] - laters[u])
            if masked:
                w = jnp.where(masks[u], w, 0.0)
            ws.append(w.astype(BF16))
        next(side, None)
        for u, (hh, s_i, r0) in enumerate(units):
            hs = slice(hh * dc, (hh + 1) * dc)
            run = run_ref[hh, r0:, :]
            acc_ref[r0:, hs] += jnp.exp2(-run) * _dot(ws[u], v_ref[s_i * sub:(s_i + 1) * sub, hs])
            run_ref[hh, r0:, :] = run + (laters[u][:, 0:1] + firsts[u])

    pl.when(kb == qi)(functools.partial(block, True))
    pl.when(kb < qi)(functools.partial(block, False))

    @pl.when(kb == 0)
    def _():
        o_ref[...] = (acc_ref[...] * _silu(g_ref[...].astype(F32))).astype(o_ref.dtype)


def _cmlp_kernel(u_ref, v_ref, g_ref, vg_ref, ws_ref, bs_ref, o_ref, *, n_chunk):
    cs = D_CHUNK
    u = _gelu(u_ref[...].astype(F32))
    vn = _rms(_gelu(v_ref[...].astype(F32)), vg_ref[...]).astype(BF16)
    gate = _silu(g_ref[...].astype(F32))
    bs = bs_ref[...]
    r = lax.broadcasted_iota(jnp.int32, (cs, cs), 0)
    c = lax.broadcasted_iota(jnp.int32, (cs, cs), 1)
    for g in range(DG):
        w = jnp.where(r >= c, ws_ref[g], 0.0).astype(BF16)
        lo, hi = g * LANES, (g + 1) * LANES
        for ch in range(n_chunk):
            r0, r1 = ch * cs, (ch + 1) * cs
            mixed = _dot(w, vn[r0:r1, lo:hi]) + bs[:, lo:hi]
            o_ref[r0:r1, lo:hi] = (u[r0:r1, lo:hi] * mixed * gate[r0:r1, lo:hi]).astype(o_ref.dtype)


def _cmlp_prompt(y3, v_norm_g, w_s, bs_full, cols, tm):
    bsz, t, _ = y3.shape
    br = DG * LANES
    cu, cv, cg = cols
    kern = functools.partial(_cmlp_kernel, n_chunk=tm // D_CHUNK)
    return pl.pallas_call(
        kern,
        out_shape=jax.ShapeDtypeStruct((bsz, t, br), BF16),
        grid=(bsz, t // tm),
        in_specs=[pl.BlockSpec((None, tm, br), lambda b, i: (b, i, cu)),
                  pl.BlockSpec((None, tm, br), lambda b, i: (b, i, cv)),
                  pl.BlockSpec((None, tm, br), lambda b, i: (b, i, cg)),
                  pl.BlockSpec((1, br), lambda b, i: (0, 0)),
                  pl.BlockSpec((DG, D_CHUNK, D_CHUNK), lambda b, i: (0, 0, 0)),
                  pl.BlockSpec((D_CHUNK, br), lambda b, i: (0, 0))],
        out_specs=pl.BlockSpec((None, tm, br), lambda b, i: (b, i, 0)),
        compiler_params=_cparams(("arbitrary", "arbitrary")),
        name="cmlp_prompt",
    )(y3, y3, y3, v_norm_g.reshape(1, br), w_s, bs_full)


def _odd_decode_kernel(y_ref, qg_ref, kg_ref, vg_ref, w0_ref, b0_ref, qn_ref, kn_ref, vo_ref, cv_ref, od_ref):
    br = HC * LANES
    y = y_ref[...]
    for h in range(HC):
        lo, hi = h * LANES, (h + 1) * LANES
        qn_ref[:, lo:hi] = _rms(y[:, lo:hi], qg_ref[...])
        kn_ref[:, lo:hi] = _rms(y[:, br + lo:br + hi], kg_ref[...])
    vo_ref[...] = y[:, 2 * br:3 * br]
    u = _gelu(y[:, 4 * br:5 * br])
    vn = _rms(_gelu(y[:, 5 * br:6 * br]), vg_ref[...])
    cv_ref[...] = vn
    od = u * (w0_ref[...] * vn + b0_ref[...]) * _silu(y[:, 6 * br:7 * br])
    od_ref[...] = od.astype(od_ref.dtype)


def _odd_decode(y_s, q_g, k_g, v_norm_g, w0_row, b0_row):
    s = y_s.shape[0]
    br = HC * LANES
    args = (y_s, q_g.reshape(1, -1), k_g.reshape(1, -1), v_norm_g.reshape(1, -1), w0_row, b0_row)
    full = lambda a: pl.BlockSpec(a.shape, lambda i: (0,) * a.ndim)
    osd = lambda dt: jax.ShapeDtypeStruct((s, br), dt)
    ospec = pl.BlockSpec((s, br), lambda i: (0, 0))
    return pl.pallas_call(
        _odd_decode_kernel,
        out_shape=(osd(F32), osd(F32), osd(F32), osd(F32), osd(BF16)),
        grid=(1,),
        in_specs=[full(a) for a in args],
        out_specs=(ospec,) * 5,
        compiler_params=_cparams(("arbitrary",)),
        name="odd_decode",
    )(*args)


def _strided_suffix_sums(x, stride, axis):
    n = x.shape[axis]
    idx = lax.broadcasted_iota(jnp.int32, x.shape, axis)
    inc, tot = x, x
    k = stride
    while k < n:
        inc = inc + jnp.where(idx < n - k, pltpu.roll(inc, n - k, axis), 0.0)
        tot = tot + pltpu.roll(tot, n - k, axis)
        k *= 2
    return inc, tot


def _paged_step(j, active, q_ref, g_ref, off_ref, k_refs, v_refs, o_ref, acc_ref, run_ref):
    n_pp = len(k_refs)
    page, hc, dc = k_refs[0].shape
    rows = page * hc
    lane = lax.broadcasted_iota(jnp.int32, (hc, rows), 1)
    own = (lane & (hc - 1)) == lax.broadcasted_iota(jnp.int32, (hc, rows), 0)
    first = j == 0
    q = (q_ref[...] * (np.float32(dc ** -0.5) * LOG2E)).astype(BF16)
    off = off_ref[...] * LOG2E
    prow = lax.broadcasted_iota(jnp.int32, (n_pp, rows), 0)
    z = jnp.zeros((n_pp, rows), F32)
    for p in range(n_pp):
        kf = k_refs[p][...].reshape(rows, dc).astype(BF16)
        s = _dot_nt(q, kf)
        z = jnp.where(prow == p, jnp.sum(jnp.where(own, s, 0.0), axis=0, keepdims=True), z)
    z = z + off
    yield
    incl, tot = _strided_suffix_sums(_neg_log2_rest(z), hc, 1)
    yield
    pages_incl, _ = _strided_suffix_sums(tot, 1, 0)
    run = jnp.where(first, 0.0, run_ref[...])
    w = jnp.where(active, jnp.exp2(z - incl - (pages_incl - tot) - run), 0.0)
    run_ref[...] = run + jnp.where(active, pages_incl[0:1, :], 0.0)
    yield
    acc = jnp.where(first, 0.0, acc_ref[...])
    for p in range(n_pp):
        wm = jnp.where(own, jnp.broadcast_to(w[p:p + 1, :], (hc, rows)), 0.0).astype(BF16)
        acc = acc + _dot(wm, v_refs[p][...].reshape(rows, dc).astype(BF16))
    acc_ref[...] = acc
    o_ref[...] = (acc * _silu(g_ref[...])).astype(o_ref.dtype)


def _attention_kernel(qi_ref, kb_ref, pt_ref, q_ref, k_ref, v_ref, g_ref, qone_ref, koff_ref,
                      pq_ref, pg_ref, poff_ref, *refs, sub, hp, n_pp, n_pchunks, n_psteps):
    k_refs, v_refs = refs[:n_pp], refs[n_pp:2 * n_pp]
    o_ref, po_ref, acc_ref, run_ref, pacc_ref, prun_ref = refs[2 * n_pp:]
    p = pl.program_id(2)
    step = (pl.program_id(0) * pl.num_programs(1) + pl.program_id(1)) * pl.num_programs(2) + p

    @pl.when(step == 0)
    def _():
        pacc_ref[...] = jnp.zeros_like(pacc_ref)
        prun_ref[...] = jnp.zeros_like(prun_ref)

    def paged():
        pstep = jnp.minimum(step, n_psteps - 1)
        return _paged_step(pstep % n_pchunks, step < n_psteps, pq_ref, pg_ref, poff_ref, k_refs, v_refs,
                           po_ref, pacc_ref, prun_ref)

    _sb_step(qi_ref[p], kb_ref[p], q_ref, k_ref, v_ref, g_ref, qone_ref, koff_ref, o_ref, acc_ref, run_ref,
             sub=sub, hp=hp, side_work=paged)


def _attention(qn, kn, y3, sb_offset, cv, cg, tq, sub, hp, qn_s, gate_s, cache_k, cache_v, layer, page_table):
    bsz, t, _ = qn.shape
    dc = LANES
    wide = hp * dc
    nq = t // tq
    n_hg = HC // hp
    pairs = [(qi, kb) for qi in range(nq) for kb in range(qi, -1, -1)]
    n_pairs = len(pairs)
    qi_tab = jnp.asarray([p[0] for p in pairs], jnp.int32)
    kb_tab = jnp.asarray([p[1] for p in pairs], jnp.int32)
    off2 = sb_offset.astype(F32) * LOG2E
    terms, rem = [], off2
    for _ in range(3):
        terms.append(rem.astype(BF16))
        rem = rem - terms[-1].astype(F32)
    lane = jnp.arange(dc)
    koff = sum(jnp.where(lane == n, tm[:, None, None], 0).astype(BF16) for n, tm in enumerate(terms))
    koff = jnp.broadcast_to(koff, (HC, sub, dc))
    qone = jnp.broadcast_to(jnp.where(lane < len(terms), 1, 0).astype(BF16), (tq, dc))

    s, hc, _ = qn_s.shape
    n_pages = page_table.shape[1]
    page = cache_k.shape[2]
    n_pp = PAGES_PER_STEP
    n_pchunks = n_pages // n_pp
    n_psteps = s * n_pchunks
    assert hc & (hc - 1) == 0 and n_pages % n_pp == 0 and n_pchunks > 1 and n_psteps <= bsz * n_hg * n_pairs
    rows = page * hc

    def pstep(b, h, p):
        return jnp.minimum((b * n_hg + h) * n_pairs + p, n_psteps - 1)

    def kv_spec(pp):
        def index(b, h, p, qi, kb, pt):
            st = pstep(b, h, p)
            return (layer, pt[st // n_pchunks, n_pages - (st % n_pchunks + 1) * n_pp + pp], 0, 0, 0)
        return pl.BlockSpec((None, None, page, hc, dc), index)

    seq_blk = pl.BlockSpec((None, hc, dc), lambda b, h, p, qi, kb, pt: (pstep(b, h, p) // n_pchunks, 0, 0))
    q_map = lambda b, h, p, qi, kb, pt: (b, qi[p], h)
    k_map = lambda b, h, p, qi, kb, pt: (b, kb[p], h)
    kern = functools.partial(_attention_kernel, sub=sub, hp=hp, n_pp=n_pp, n_pchunks=n_pchunks,
                             n_psteps=n_psteps)
    return pl.pallas_call(
        kern,
        out_shape=(jax.ShapeDtypeStruct((bsz, t, HC * dc), BF16),
                   jax.ShapeDtypeStruct((s, hc, dc), BF16)),
        grid_spec=pltpu.PrefetchScalarGridSpec(
            num_scalar_prefetch=3,
            grid=(bsz, n_hg, n_pairs),
            in_specs=[pl.BlockSpec((None, tq, wide), q_map),
                      pl.BlockSpec((None, tq, wide), k_map),
                      pl.BlockSpec((None, tq, wide), lambda b, h, p, qi, kb, pt: (b, kb[p], cv // wide + h)),
                      pl.BlockSpec((None, tq, wide), lambda b, h, p, qi, kb, pt: (b, qi[p], cg // wide + h)),
                      pl.BlockSpec((tq, dc), lambda b, h, p, qi, kb, pt: (0, 0)),
                      pl.BlockSpec((hp, sub, dc), lambda b, h, p, qi, kb, pt: (h, 0, 0)),
                      seq_blk, seq_blk,
                      pl.BlockSpec((1, rows), lambda b, h, p, qi, kb, pt: (0, 0))]
            + [kv_spec(pp) for pp in range(n_pp)] * 2,
            out_specs=(pl.BlockSpec((None, tq, wide), q_map), seq_blk),
            scratch_shapes=[pltpu.VMEM((tq, wide), F32), pltpu.VMEM((hp, tq, 1), F32),
                            pltpu.VMEM((hc, dc), F32), pltpu.VMEM((1, rows), F32)]),
        compiler_params=_cparams(("arbitrary",) * 3),
        name="attention",
    )(qi_tab, kb_tab, page_table, qn, kn, y3, y3, qone, koff,
      qn_s, gate_s, jnp.tile(sb_offset, page).reshape(1, rows),
      *([cache_k] * n_pp), *([cache_v] * n_pp))


def _rope_tables(pos, half):
    inv = ROPE_BASE ** (-jnp.arange(half, dtype=F32) / half)
    ang = pos.astype(F32)[:, None] * inv[None, :]
    cos, sin = jnp.cos(ang), jnp.sin(ang)
    return jnp.concatenate([cos, cos], axis=-1), jnp.concatenate([-sin, sin], axis=-1)


def kernel(x_prompt, x_sample, state_gla, state_ret, cache_k, cache_v, page_table, c_prompt, c_sample,
           e_norm_g, e_w_ada, e_b_ada, e_w_in, e_w_a2, e_b_a, e_gla_norm_g, e_ret_norm_g, e_w_out,
           o_norm_g, o_w_ada, o_b_ada, o_w_in, o_q_norm_g, o_k_norm_g, o_sb_offset, o_v_norm_g,
           o_w_s, o_b_s, o_w_out):
    bsz, t, d = x_prompt.shape
    s = x_sample.shape[0]
    br = d // 2
    n_past = page_table.shape[1] * cache_k.shape[2]
    dk = LANES

    c_rows = jnp.concatenate([c_sample, c_prompt, jnp.zeros((16 - s - bsz, d), F32)], axis=0)

    n_qkv = 2 * HA * dk + HA * 2 * dk
    n_main = e_w_in.shape[2] - GLA_RANK
    col_ret = n_qkv + HA * 2 * dk
    cols_dec = (0, HA * dk, 2 * HA * dk, n_qkv,
                col_ret, col_ret + HB * dk, col_ret + 2 * HB * dk, col_ret + 2 * HB * dk + HB * 2 * dk,
                n_main)

    xp = x_prompt.reshape(bsz * t, d)
    xs = x_sample.reshape(s, d)
    pos_p = jnp.arange(t)
    pos_s = n_past + jnp.arange(x_sample.shape[1])
    cos_p, sin_p = _rope_tables(pos_p, dk // 2)
    cos_s, sin_s = _rope_tables(pos_s, dk // 2)
    log_gamma = jnp.log1p(-jnp.exp2(-5.0 - jnp.arange(HB, dtype=F32)))
    lg_row = jnp.repeat(log_gamma, dk).reshape(1, HB * dk)

    i = 0
    mod = _ada(c_rows, e_w_ada[i], e_b_ada[i])
    mod_s, mod_p = mod[:s], mod[s:s + bsz].reshape(bsz, 1, 3 * d)
    w_in_b = _reorder_cast(jnp.swapaxes(e_w_in, 1, 2), i, n_qkv, GLA_RANK, n_main + MXU_DIM, 256)
    wa2_pad = jnp.concatenate([e_w_a2[i], jnp.zeros((LANES - GLA_RANK, HA * dk), F32)], axis=0).astype(BF16)
    w_out_b = e_w_out[i].astype(BF16)

    y = _proj(xp, mod_p, e_norm_g[i], w_in_b, t, 1024, 1280, w_transposed=True)
    y3 = y.reshape(bsz, t, -1)
    oa, gla_p = _gla_prompt(y3, wa2_pad, e_b_a[i], e_gla_norm_g[i], 512)
    ob, ret_p = _ret_prompt(y3, cos_p, sin_p, lg_row, e_ret_norm_g[i], col_ret, 512)
    xp = _outproj(oa.reshape(bsz * t, br), 0, ob.reshape(bsz * t, br), 0, w_out_b, xp, mod_p, t, 512, d)

    y_s = _proj(xs, mod_s, e_norm_g[i], w_in_b, None, s, 1280, w_transposed=True)
    mixed_s, gla_s, ret_s = _even_decode(y_s, state_gla[i], state_ret[i], wa2_pad, e_b_a[i],
                                         e_gla_norm_g[i], e_ret_norm_g[i], cos_s, sin_s, lg_row, cols_dec)
    xs = _outproj(mixed_s, 0, mixed_s, 1, w_out_b, xs, mod_s, None, s, 1024)

    mod = _ada(c_rows, o_w_ada[i], o_b_ada[i])
    mod_s, mod_p = mod[:s], mod[s:s + bsz].reshape(bsz, 1, 3 * d)
    w_in_b = o_w_in[i].astype(BF16)
    w_out_b = o_w_out[i].astype(BF16)
    bs_full = jnp.repeat(jnp.transpose(o_b_s[i]), LANES, axis=1)

    y = _proj(xp, mod_p, o_norm_g[i], w_in_b, t, 1024, 1024)
    y3 = y.reshape(bsz, t, -1)
    qn, kn, knb, vo = _qkv_prompt(y, o_q_norm_g[i], o_k_norm_g[i], 512)
    od = _cmlp_prompt(y3, o_v_norm_g[i], o_w_s[i], bs_full, (4, 5, 6), 256)

    y_s = _proj(xs, mod_s, o_norm_g[i], w_in_b, None, s, 1024)
    w0_row = jnp.repeat(o_w_s[i][:, 0, 0], LANES).reshape(1, br)
    b0_row = jnp.repeat(o_b_s[i][:, 0], LANES).reshape(1, br)
    qn_s, kn_s, vo_s, cv_s, od_s = _odd_decode(y_s, o_q_norm_g[i], o_k_norm_g[i], o_v_norm_g[i], w0_row, b0_row)
    dc = LANES
    shp = (bsz, t, br)
    oc, oc_s = _attention(qn.reshape(shp), knb.reshape(shp), y3, o_sb_offset[i], 2 * br, 3 * br, 512, 256, 4,
                          qn_s.reshape(s, HC, dc), y_s[:, 3 * br:4 * br].reshape(s, HC, dc),
                          cache_k, cache_v, i, page_table)
    xp = _outproj(oc.reshape(bsz * t, br), 0, od.reshape(bsz * t, br), 0, w_out_b, xp, mod_p, t, 512, d)
    xs = _outproj(oc_s.reshape(s, br), 0, od_s, 0, w_out_b, xs, mod_s, None, s, 1024)

    return (xp.reshape(bsz, t, d), xs.reshape(s, 1, d),
            gla_p[None], gla_s[None], ret_p[None], ret_s[None],
            kn.reshape(1, bsz, t, HC, dc), vo.reshape(1, bsz, t, HC, dc),
            kn_s.reshape(1, s, 1, HC, dc), vo_s.reshape(1, s, 1, HC, dc),
            cv_s.reshape(1, s, 1, br))
```
